```python
import jax, jax.numpy as jnp
from jax import lax
import numpy as np

D_MODEL = 1024
BATCH = 32
SEQ = 2048
DEPTH = 1

HG_HEADS = 4
HG_DIM = 128
HG_WIDTH = HG_HEADS * HG_DIM
HG_CHUNK = 64
DIL_PAIRS = ((128, 1), (512, 4), (2048, 16))
N_DIL = len(DIL_PAIRS)
ATT_SLOTS = 4
ATT_HEAD_DIM = 64
ATT_HEADS = N_DIL * ATT_SLOTS
ATT_WIDTH = ATT_HEADS * ATT_HEAD_DIM
ATT_OUT = ATT_SLOTS * ATT_HEAD_DIM
ROT_DIM = ATT_HEAD_DIM // 4
ROPE_THETA = 500000.0
COL_SIZES = (HG_WIDTH,) * 5 + (ATT_WIDTH,) * 3 + (D_MODEL,) * 2
IN_COLS = sum(COL_SIZES)
N_EXPERTS = 256
TOP_K = 8
N_GROUPS = 8
TOPK_GROUPS = 4
EXPERT_HIDDEN = 256
SHARED_HIDDEN = 256
ROUTED_SCALE = 2.5
MOE_BLOCK = 256
PLE_DIM = 256
DN_ALPHA = (2 * DEPTH) ** 0.25
DN_BETA = (8 * DEPTH) ** -0.25
LN_EPS = 1e-5
NEG_INF = -1e30

kernel_name = 'hybrid_hgrn2_dilated_moe_encoder_block'


def layer_norm(x, g, b):
    xf = x.astype(jnp.float32)
    mu = jnp.mean(xf, axis=-1, keepdims=True)
    xc = xf - mu
    var = jnp.mean(xc * xc, axis=-1, keepdims=True)
    return (xc * lax.rsqrt(var + LN_EPS) * g + b).astype(x.dtype)


def hgrn2_chunked(q, k, v, log_f):
    B, H, L, dk = q.shape
    dv = v.shape[-1]
    n = L // HG_CHUNK
    q = q.reshape(B, H, n, HG_CHUNK, dk)
    k = k.reshape(B, H, n, HG_CHUNK, dk)
    log_f = log_f.reshape(B, H, n, HG_CHUNK, dk)
    v = v.reshape(B, H, n, HG_CHUNK, dv)
    b = jnp.cumsum(log_f, axis=3)
    b_ref = b[:, :, :, HG_CHUNK // 2 - 1:HG_CHUNK // 2]
    b_last = b[:, :, :, -1:]
    prefix = jnp.tril(jnp.ones((HG_CHUNK, HG_CHUNK), dtype=bool))
    scores = jnp.einsum('bhnce,bhnse->bhncs', q * jnp.exp(b - b_ref), k * jnp.exp(b_ref - b))
    o_intra = jnp.einsum('bhncs,bhnsv->bhncv', jnp.where(prefix, scores, 0.0), v)
    q_in = q * jnp.exp(b)
    k_out = k * jnp.exp(b_last - b)
    decay = jnp.exp(b_last[:, :, :, 0])

    def step(S, xs):
        qc, kc, vc, dc = xs
        o = jnp.einsum('bhce,bhev->bhcv', qc, S)
        S = dc[..., None] * S + jnp.einsum('bhce,bhcv->bhev', kc, vc)
        return S, o

    S0 = jnp.zeros((B, H, dk, dv), jnp.float32)
    mv = lambda t: jnp.moveaxis(t, 2, 0)
    _, o_inter = lax.scan(step, S0, (mv(q_in), mv(k_out), mv(v), mv(decay)))
    o = o_intra + jnp.moveaxis(o_inter, 0, 2)
    return o.reshape(B, H, L, dv)


def hgrn2_branch(q_raw, i_raw, ff_raw, fb_raw, g_raw, lb_fwd, lb_bwd, norm_g):
    B, L, _ = q_raw.shape
    heads = lambda t: t.astype(jnp.float32).reshape(B, L, HG_HEADS, HG_DIM).transpose(0, 2, 1, 3)
    q = jax.nn.silu(heads(q_raw))
    v = heads(i_raw)

    def gates(f_raw, lb):
        lb = lb.reshape(HG_HEADS, 1, HG_DIM)
        f = lb + (1.0 - lb) * jax.nn.sigmoid(heads(f_raw))
        return 1.0 - f, jnp.log(f)

    k_f, lf_f = gates(ff_raw, lb_fwd)
    k_b, lf_b = gates(fb_raw, lb_bwd)
    o_fwd = hgrn2_chunked(q, k_f, v, lf_f)
    rev = lambda t: jnp.flip(t, axis=2)
    o_bwd = rev(hgrn2_chunked(rev(q), rev(k_b), rev(v), rev(lf_b)))
    o = (o_fwd + o_bwd).transpose(0, 2, 1, 3)
    o = o * lax.rsqrt(jnp.mean(o * o, axis=-1, keepdims=True) + LN_EPS) * norm_g.reshape(HG_HEADS, HG_DIM)
    o = o.reshape(B, L, HG_WIDTH) * jax.nn.silu(g_raw.astype(jnp.float32))
    return o.astype(q_raw.dtype)


def partial_rotary(t, pos):
    half = ROT_DIM // 2
    inv_freq = ROPE_THETA ** (-jnp.arange(half, dtype=jnp.float32) / half)
    ang = pos[:, None] * inv_freq[None, :]
    cos = jnp.cos(ang)[:, None, :]
    sin = jnp.sin(ang)[:, None, :]
    t1 = t[..., :half]
    t2 = t[..., half:ROT_DIM]
    return jnp.concatenate([t1 * cos - t2 * sin, t2 * cos + t1 * sin, t[..., ROT_DIM:]], axis=-1)


def dilated_group_attention(q, k, v, window, dilation):
    B, L, H, dh = q.shape
    span = window // (2 * dilation)
    m_len = L // dilation
    nb = -(-m_len // span)
    mp = nb * span

    def to_sub(t):
        t = t.reshape(B, m_len, dilation, H, dh).transpose(0, 2, 3, 1, 4)
        return jnp.pad(t, ((0, 0), (0, 0), (0, 0), (0, mp - m_len), (0, 0)))

    def neighbours(t):
        t = jnp.pad(to_sub(t), ((0, 0), (0, 0), (0, 0), (span, span), (0, 0)))
        t = t.reshape(B, dilation, H, nb + 2, span, dh)
        return jnp.concatenate([t[:, :, :, j:j + nb] for j in range(3)], axis=4)

    qb = to_sub(q).reshape(B, dilation, H, nb, span, dh)
    kb = neighbours(k)
    vb = neighbours(v)
    qi = jnp.arange(nb)[:, None] * span + jnp.arange(span)[None, :]
    ki = (jnp.arange(nb)[:, None] - 1) * span + jnp.arange(3 * span)[None, :]
    mask = ((jnp.abs(qi[:, :, None] - ki[:, None, :]) <= span)
            & (ki[:, None, :] >= 0) & (ki[:, None, :] < m_len))
    s = jnp.einsum('bdhnqe,bdhnke->bdhnqk', qb, kb) * (dh ** -0.5)
    s = jnp.where(mask, s, NEG_INF)
    mx = jnp.max(s, axis=-1, keepdims=True)
    e = jnp.exp(s - mx)
    den = jnp.sum(e, axis=-1, keepdims=True)
    o = jnp.einsum('bdhnqk,bdhnke->bdhnqe', e, vb) / den
    lse = (mx + jnp.log(den))[..., 0]
    o = o.reshape(B, dilation, H, mp, dh)[:, :, :, :m_len].transpose(0, 3, 1, 2, 4).reshape(B, L, H, dh)
    lse = lse.reshape(B, dilation, H, mp)[..., :m_len].transpose(0, 3, 1, 2).reshape(B, L, H)
    return o, lse


def dilated_branch(q_raw, k_raw, v_raw):
    B, L, _ = q_raw.shape
    pos = jnp.arange(L, dtype=jnp.float32)
    sh = lambda t: t.astype(jnp.float32).reshape(B, L, ATT_HEADS, ATT_HEAD_DIM)
    q = partial_rotary(sh(q_raw), pos)
    k = partial_rotary(sh(k_raw), pos)
    v = sh(v_raw)
    outs, lses = [], []
    for g, (window, dilation) in enumerate(DIL_PAIRS):
        hs = slice(g * ATT_SLOTS, (g + 1) * ATT_SLOTS)
        o, lse = dilated_group_attention(q[:, :, hs], k[:, :, hs], v[:, :, hs], window, dilation)
        outs.append(o)
        lses.append(lse)
    o = jnp.stack(outs, axis=2)
    w = jax.nn.softmax(jnp.stack(lses, axis=2), axis=2)
    o = jnp.sum(w[..., None] * o, axis=2).reshape(B, L, ATT_OUT)
    return o.astype(q_raw.dtype)


def moe_ffn(xt, w_router, router_bias, w_gate, w_up, w_down, ws_gate, ws_up, ws_down):
    T, D = xt.shape
    scores = jax.nn.sigmoid((xt @ w_router).astype(jnp.float32))
    biased = scores + router_bias.astype(jnp.float32)
    per_group = N_EXPERTS // N_GROUPS
    group_score = jnp.sum(lax.top_k(biased.reshape(T, N_GROUPS, per_group), 2)[0], axis=-1)
    _, top_groups = lax.top_k(group_score, TOPK_GROUPS)
    group_mask = jnp.any(top_groups[:, :, None] == jnp.arange(N_GROUPS)[None, None, :], axis=1)
    expert_mask = jnp.repeat(group_mask, per_group, axis=1)
    _, top_e = lax.top_k(jnp.where(expert_mask, biased, -jnp.inf), TOP_K)
    gate = jnp.take_along_axis(scores, top_e, axis=1)
    gate = gate / jnp.sum(gate, axis=-1, keepdims=True) * ROUTED_SCALE

    n_assign = T * TOP_K
    e_flat = top_e.reshape(-1)
    tok_flat = jnp.repeat(jnp.arange(T, dtype=jnp.int32), TOP_K)
    g_flat = gate.reshape(-1)
    order = jnp.argsort(e_flat)
    e_sorted = e_flat[order]
    tok_sorted = tok_flat[order]
    g_sorted = g_flat[order]
    counts = jnp.bincount(e_flat, length=N_EXPERTS)
    starts = jnp.cumsum(counts) - counts
    padded = (counts + MOE_BLOCK - 1) // MOE_BLOCK * MOE_BLOCK
    padded_end = jnp.cumsum(padded)
    padded_start = padded_end - padded
    dest = padded_start[e_sorted] + jnp.arange(n_assign) - starts[e_sorted]
    n_blocks = (n_assign + N_EXPERTS * (MOE_BLOCK - 1) + MOE_BLOCK - 1) // MOE_BLOCK
    n_slots = n_blocks * MOE_BLOCK
    slot_tok = jnp.zeros((n_slots,), jnp.int32).at[dest].set(tok_sorted)
    slot_gate = jnp.zeros((n_slots,), jnp.float32).at[dest].set(g_sorted)
    block_expert = jnp.minimum(
        jnp.searchsorted(padded_end, jnp.arange(n_blocks) * MOE_BLOCK, side='right'), N_EXPERTS - 1)

    def expert_block(acc, blk):
        tok, g, e = blk
        xb = xt[tok]
        h = jax.nn.silu(xb @ w_gate[e]) * (xb @ w_up[e])
        y = (h @ w_down[e]) * g[:, None].astype(xt.dtype)
        return acc.at[tok].add(y), None

    routed, _ = lax.scan(expert_block, jnp.zeros_like(xt),
                         (slot_tok.reshape(n_blocks, MOE_BLOCK), slot_gate.reshape(n_blocks, MOE_BLOCK), block_expert))
    shared = (jax.nn.silu(xt @ ws_gate) * (xt @ ws_up)) @ ws_down
    return routed + shared


def setup_inputs(seed: int = 0) -> dict:
    key = jax.random.key(seed)
    ks = jax.random.split(key, 32)
    f32 = jnp.float32
    nrm = lambda k, shape, fan_in, scale=1.0: jax.random.normal(k, shape, f32) * (scale * fan_in ** -0.5)
    gain = lambda k, n: 1.0 + 0.02 * jax.random.normal(k, (DEPTH, n), f32)
    bias = lambda k, n: 0.02 * jax.random.normal(k, (DEPTH, n), f32)
    col_scale = np.ones((IN_COLS,), np.float32)
    col_scale[HG_WIDTH:2 * HG_WIDTH] = DN_BETA
    v0 = 5 * HG_WIDTH + 2 * ATT_WIDTH
    col_scale[v0:v0 + ATT_WIDTH] = DN_BETA
    return {
        'x': jax.random.normal(ks[0], (BATCH, SEQ, D_MODEL), f32),
        'p': jax.random.normal(ks[1], (DEPTH, BATCH, SEQ, PLE_DIM), f32),
        'w_in': nrm(ks[2], (DEPTH, D_MODEL, IN_COLS), D_MODEL) * jnp.asarray(col_scale),
        'hg_lb_fwd': 0.1 * jax.random.normal(ks[3], (DEPTH + 1, HG_WIDTH), f32),
        'hg_lb_bwd': 0.1 * jax.random.normal(ks[4], (DEPTH + 1, HG_WIDTH), f32),
        'hg_norm_g': gain(ks[5], HG_WIDTH),
        'w_branch_hg': nrm(ks[6], (DEPTH, HG_WIDTH, D_MODEL), HG_WIDTH, DN_BETA),
        'w_branch_att': nrm(ks[7], (DEPTH, ATT_OUT, D_MODEL), ATT_OUT, DN_BETA),
        'w_out': nrm(ks[8], (DEPTH, D_MODEL, D_MODEL), D_MODEL, DN_BETA),
        'ln1_g': gain(ks[9], D_MODEL),
        'ln1_b': bias(ks[10], D_MODEL),
        'w_router': nrm(ks[11], (DEPTH, D_MODEL, N_EXPERTS), D_MODEL),
        'router_bias': 0.01 * jax.random.normal(ks[12], (DEPTH, N_EXPERTS), f32),
        'w_exp_gate': nrm(ks[13], (DEPTH, N_EXPERTS, D_MODEL, EXPERT_HIDDEN), D_MODEL, DN_BETA),
        'w_exp_up': nrm(ks[14], (DEPTH, N_EXPERTS, D_MODEL, EXPERT_HIDDEN), D_MODEL, DN_BETA),
        'w_exp_down': nrm(ks[15], (DEPTH, N_EXPERTS, EXPERT_HIDDEN, D_MODEL), EXPERT_HIDDEN, DN_BETA),
        'w_sh_gate': nrm(ks[16], (DEPTH, D_MODEL, SHARED_HIDDEN), D_MODEL, DN_BETA),
        'w_sh_up': nrm(ks[17], (DEPTH, D_MODEL, SHARED_HIDDEN), D_MODEL, DN_BETA),
        'w_sh_down': nrm(ks[18], (DEPTH, SHARED_HIDDEN, D_MODEL), SHARED_HIDDEN, DN_BETA),
        'ln2_g': gain(ks[19], D_MODEL),
        'ln2_b': bias(ks[20], D_MODEL),
        'w_ple_gate': nrm(ks[21], (DEPTH, D_MODEL, D_MODEL), D_MODEL),
        'w_ple_proj': nrm(ks[22], (DEPTH, PLE_DIM, D_MODEL), PLE_DIM, DN_BETA),
        'ln3_g': gain(ks[23], D_MODEL),
        'ln3_b': bias(ks[24], D_MODEL),
    }


def reference(x, p, w_in, hg_lb_fwd, hg_lb_bwd, hg_norm_g, w_branch_hg, w_branch_att, w_out,
              ln1_g, ln1_b, w_router, router_bias, w_exp_gate, w_exp_up, w_exp_down,
              w_sh_gate, w_sh_up, w_sh_down, ln2_g, ln2_b, w_ple_gate, w_ple_proj, ln3_g, ln3_b):
    B, L, D = x.shape
    splits = np.cumsum(COL_SIZES)[:-1].tolist()
    lb_fwd_all = jnp.cumsum(jax.nn.softmax(hg_lb_fwd.astype(jnp.float32), axis=0), axis=0)
    lb_bwd_all = jnp.cumsum(jax.nn.softmax(hg_lb_bwd.astype(jnp.float32), axis=0), axis=0)
    for i in range(DEPTH):
        z = x @ w_in[i]
        hq, hi, hff, hfb, hg, aq, ak, av, gate_h, gate_a = jnp.split(z, splits, axis=-1)
        y_h = hgrn2_branch(hq, hi, hff, hfb, hg, lb_fwd_all[i], lb_bwd_all[i], hg_norm_g[i])
        y_a = dilated_branch(aq, ak, av)
        merged = (jax.nn.sigmoid(gate_h) * (y_h @ w_branch_hg[i])
                  + jax.nn.sigmoid(gate_a) * (y_a @ w_branch_att[i]))
        x = layer_norm(DN_ALPHA * x + merged @ w_out[i], ln1_g[i], ln1_b[i])
        ffn = moe_ffn(x.reshape(B * L, D), w_router[i], router_bias[i], w_exp_gate[i], w_exp_up[i],
                      w_exp_down[i], w_sh_gate[i], w_sh_up[i], w_sh_down[i]).reshape(B, L, D)
        x = layer_norm(DN_ALPHA * x + ffn, ln2_g[i], ln2_b[i])
        ple = jax.nn.sigmoid(x @ w_ple_gate[i]) * (p[i] @ w_ple_proj[i])
        x = layer_norm(DN_ALPHA * x + ple, ln3_g[i], ln3_b[i])
    return x
```

```python
import functools

import jax
import jax.numpy as jnp
import numpy as np
from jax import lax
from jax.experimental import pallas as pl
from jax.experimental.pallas import tpu as pltpu

F32 = jnp.float32
BF16 = jnp.bfloat16

D_MODEL = 1024
HG_HEADS = 4
HG_DIM = 128
HG_WIDTH = HG_HEADS * HG_DIM
HG_CHUNK = 64
DIL_PAIRS = ((128, 1), (512, 4), (2048, 16))
ATT_SLOTS = 4
ATT_HEAD_DIM = 64
ATT_WIDTH = len(DIL_PAIRS) * ATT_SLOTS * ATT_HEAD_DIM
ATT_OUT = ATT_SLOTS * ATT_HEAD_DIM
ROT_DIM = ATT_HEAD_DIM // 4
ROPE_THETA = 500000.0
COL_SIZES = (HG_WIDTH,) * 5 + (ATT_WIDTH,) * 3 + (D_MODEL,) * 2
COL_STARTS = tuple(int(v) for v in np.cumsum((0,) + COL_SIZES)[:-1])
IN_COLS = sum(COL_SIZES)
N_EXPERTS = 256
TOP_K = 8
N_GROUPS = 8
TOPK_GROUPS = 4
PER_GROUP = N_EXPERTS // N_GROUPS
EXPERT_HIDDEN = 256
SHARED_HIDDEN = 256
ROUTED_SCALE = 2.5
SLOT_BLOCK = 256
PLE_DIM = 256
LN_EPS = 1e-5
NEG_INF = -1e30

LANES = 128
VMEM_LIMIT = 56 * 1024 * 1024

NT_DIMS = (((1,), (1,)), ((), ()))
TN_DIMS = (((0,), (0,)), ((), ()))


def _sigmoid(v):
    return jax.nn.sigmoid(v)


def _silu(v):
    return v * jax.nn.sigmoid(v)


def _layer_norm(v, g, b):
    mu = jnp.mean(v, axis=-1, keepdims=True)
    vc = v - mu
    var = jnp.mean(vc * vc, axis=-1, keepdims=True)
    return vc * lax.rsqrt(var + LN_EPS) * g + b


def _const_spec(shape):
    return pl.BlockSpec(shape, lambda *_: (0,) * len(shape))


def _in_proj_kernel(x_ref, w_ref, lbf_ref, lbb_ref, rc_ref, rsa_ref, rsb_ref,
                    hq_ref, hv_ref, lff_ref, lfb_ref, hg_ref,
                    aq_ref, ak_ref, av_ref, gh_ref, ga_ref):
    xb = x_ref[...].astype(BF16)

    def proj(seg):
        c0, width = COL_STARTS[seg], COL_SIZES[seg]
        return jnp.dot(xb, w_ref[:, c0:c0 + width], preferred_element_type=F32)

    hq_ref[...] = _silu(proj(0)).astype(BF16)
    hv_ref[...] = proj(1).astype(BF16)
    lb = lbf_ref[...]
    lff_ref[...] = jnp.log(lb + (1.0 - lb) * _sigmoid(proj(2)))
    lb = lbb_ref[...]
    lfb_ref[...] = jnp.log(lb + (1.0 - lb) * _sigmoid(proj(3)))
    hg_ref[...] = _silu(proj(4)).astype(BF16)

    rc, rsa, rsb = rc_ref[...], rsa_ref[...], rsb_ref[...]

    def rotary(z, out_ref, scale):
        for s in range(ATT_WIDTH // LANES):
            t = z[:, s * LANES:(s + 1) * LANES]
            up = pltpu.roll(t, LANES - ROT_DIM // 2, 1)
            dn = pltpu.roll(t, ROT_DIM // 2, 1)
            r = t * rc + up * rsa + dn * rsb
            out_ref[:, s * LANES:(s + 1) * LANES] = (r * scale).astype(BF16)

    rotary(proj(5), aq_ref, ATT_HEAD_DIM ** -0.5)
    rotary(proj(6), ak_ref, 1.0)
    av_ref[...] = proj(7).astype(BF16)
    gh_ref[...] = _sigmoid(proj(8)).astype(BF16)
    ga_ref[...] = _sigmoid(proj(9)).astype(BF16)


def _in_proj(x2d, w_b, lbf, lbb, rc, rsa, rsb, seq_len):
    T = x2d.shape[0]
    tm = 512
    n_pos_blocks = seq_len // tm
    row = lambda w: pl.BlockSpec((tm, w), lambda i: (i, 0))
    tab = pl.BlockSpec((tm, LANES), lambda i: (i % n_pos_blocks, 0))
    out_w = (HG_WIDTH, HG_WIDTH, HG_WIDTH, HG_WIDTH, HG_WIDTH, ATT_WIDTH, ATT_WIDTH, ATT_WIDTH, D_MODEL, D_MODEL)
    out_dt = (BF16, BF16, F32, F32, BF16, BF16, BF16, BF16, BF16, BF16)
    return pl.pallas_call(
        _in_proj_kernel,
        grid=(T // tm,),
        in_specs=[row(D_MODEL),
                  pl.BlockSpec((D_MODEL, IN_COLS), lambda i: (0, 0), pipeline_mode=pl.Buffered(1)),
                  _const_spec((1, HG_WIDTH)), _const_spec((1, HG_WIDTH)), tab, tab, tab],
        out_specs=[row(w) for w in out_w],
        out_shape=[jax.ShapeDtypeStruct((T, w), dt) for w, dt in zip(out_w, out_dt)],
        compiler_params=pltpu.CompilerParams(dimension_semantics=("parallel",), vmem_limit_bytes=VMEM_LIMIT),
        name="in_proj",
    )(x2d, w_b, lbf, lbb, rc, rsa, rsb)


def _split3(a):
    hi = a.astype(BF16)
    r1 = a - hi.astype(F32)
    mid = r1.astype(BF16)
    lo = (r1 - mid.astype(F32)).astype(BF16)
    return jnp.concatenate([hi, mid, lo], axis=1)


def _hgrn_kernel(q_ref, v_ref, lff_ref, lfb_ref, sg_ref, ng_ref, y_ref, of_scr, ob_scr):
    L = q_ref.shape[1]
    C = HG_CHUNK
    n = L // C
    r_i = lax.broadcasted_iota(jnp.int32, (C, C), 0)
    c_i = lax.broadcasted_iota(jnp.int32, (C, C), 1)
    lower = c_i <= r_i
    upper = c_i >= r_i
    lower_m = lower.astype(BF16)
    upper_m = upper.astype(BF16)

    def one_dir(c0, lf_ref, tri_m, mask, ref_row, last_row, st):
        q = q_ref[0, pl.ds(c0, C), :].astype(F32)
        v = v_ref[0, pl.ds(c0, C), :]
        lf = lf_ref[0, pl.ds(c0, C), :]
        p = jnp.dot(tri_m, _split3(lf), preferred_element_type=F32)
        b = p[:, :HG_DIM] + p[:, HG_DIM:2 * HG_DIM] + p[:, 2 * HG_DIM:]
        b_ref = b[ref_row:ref_row + 1, :]
        b_last = b[last_row:last_row + 1, :]
        k = 1.0 - jnp.exp(lf)
        a_q = (q * jnp.exp(b - b_ref)).astype(BF16)
        a_k = (k * jnp.exp(b_ref - b)).astype(BF16)
        s = lax.dot_general(a_q, a_k, NT_DIMS, preferred_element_type=F32)
        s = jnp.where(mask, s, 0.0).astype(BF16)
        o = jnp.dot(s, v, preferred_element_type=F32)
        q_in = (q * jnp.exp(b)).astype(BF16)
        o = o + lax.dot_general(q_in, st.astype(BF16), NT_DIMS, preferred_element_type=F32)
        k_out = (k * jnp.exp(b_last - b)).astype(BF16)
        st_new = st * jnp.exp(b_last) + lax.dot_general(v, k_out, TN_DIMS, preferred_element_type=F32)
        return o, st_new

    def body(c, carry):
        st_f, st_b = carry
        i0 = pl.multiple_of(c * C, C)
        j0 = pl.multiple_of((n - 1 - c) * C, C)
        o_f, st_f = one_dir(i0, lff_ref, lower_m, lower, C // 2 - 1, C - 1, st_f)
        o_b, st_b = one_dir(j0, lfb_ref, upper_m, upper, C // 2, 0, st_b)
        of_scr[pl.ds(i0, C), :] = o_f
        ob_scr[pl.ds(j0, C), :] = o_b
        return st_f, st_b

    z = jnp.zeros((HG_DIM, HG_DIM), F32)
    lax.fori_loop(0, n, body, (z, z))

    o = of_scr[...] + ob_scr[...]
    o = o * lax.rsqrt(jnp.mean(o * o, axis=-1, keepdims=True) + LN_EPS) * ng_ref[...]
    y_ref[0] = (o * sg_ref[0].astype(F32)).astype(BF16)


def _hgrn(hq, hv, lff, lfb, hg, norm_g, batch, seq_len):
    blk = pl.BlockSpec((1, seq_len, HG_DIM), lambda b, h: (b, 0, h))
    r3 = lambda a: a.reshape(batch, seq_len, HG_WIDTH)
    return pl.pallas_call(
        _hgrn_kernel,
        grid=(batch, HG_HEADS),
        in_specs=[blk, blk, blk, blk, blk, pl.BlockSpec((1, HG_DIM), lambda b, h: (0, h))],
        out_specs=blk,
        out_shape=jax.ShapeDtypeStruct((batch, seq_len, HG_WIDTH), BF16),
        scratch_shapes=[pltpu.VMEM((seq_len, HG_DIM), F32), pltpu.VMEM((seq_len, HG_DIM), F32)],
        compiler_params=pltpu.CompilerParams(dimension_semantics=("parallel", "parallel"),
                                             vmem_limit_bytes=VMEM_LIMIT),
        name="hgrn",
    )(r3(hq), r3(hv), r3(lff), r3(lfb), r3(hg), norm_g.reshape(1, HG_WIDTH))


def _attn_kernel(q_ref, k_ref, v_ref, o_ref, lse_ref, *, m_len, qb, kw, span):
    lane = lax.broadcasted_iota(jnp.int32, (1, LANES), 1)
    head0 = lane < ATT_HEAD_DIM
    qi_l = lax.broadcasted_iota(jnp.int32, (qb, kw), 0)
    ki_l = lax.broadcasted_iota(jnp.int32, (qb, kw), 1)

    def block(i, carry):
        q0 = pl.multiple_of(i * qb, qb)
        ks = pl.multiple_of(jnp.clip(q0 - span, 0, m_len - kw), span)
        mask = jnp.abs((q0 + qi_l) - (ks + ki_l)) <= span
        for slab in range(ATT_OUT // LANES):
            cols = slice(slab * LANES, (slab + 1) * LANES)
            q2 = q_ref[0, pl.ds(q0, qb), cols]
            k2 = k_ref[0, pl.ds(ks, kw), cols]
            v2 = v_ref[0, pl.ds(ks, kw), cols]
            outs, lses = [], []
            for hm in (head0, jnp.logical_not(head0)):
                qm = jnp.where(hm, q2, jnp.zeros_like(q2))
                s = lax.dot_general(qm, k2, NT_DIMS, preferred_element_type=F32)
                s = jnp.where(mask, s, NEG_INF)
                mx = jnp.max(s, axis=-1, keepdims=True)
                e = jnp.exp(s - mx)
                den = jnp.sum(e, axis=-1, keepdims=True)
                outs.append(jnp.dot(e.astype(BF16), v2, preferred_element_type=F32) / den)
                lses.append(mx + jnp.log(den))
            o_ref[0, pl.ds(q0, qb), cols] = jnp.where(head0, outs[0], outs[1]).astype(BF16)
            lse_ref[0, pl.ds(q0, qb), cols] = jnp.where(head0, lses[0], lses[1])
        return carry

    lax.fori_loop(0, m_len // qb, block, 0)


def _attn_group(aq, ak, av, g, batch, seq_len):
    window, dil = DIL_PAIRS[g]
    span = window // (2 * dil)
    m_len = seq_len // dil
    qb = min(2 * span, m_len)
    kw = min(qb + 2 * span, m_len)
    n_slab = ATT_WIDTH // ATT_OUT
    view = lambda a: a.reshape(batch, m_len, dil * ATT_WIDTH)
    in_blk = pl.BlockSpec((1, m_len, ATT_OUT), lambda b, r: (b, 0, r * n_slab + g))
    out_blk = pl.BlockSpec((1, m_len, ATT_OUT), lambda b, r: (b, 0, r))
    o, lse = pl.pallas_call(
        functools.partial(_attn_kernel, m_len=m_len, qb=qb, kw=kw, span=span),
        grid=(batch, dil),
        in_specs=[in_blk, in_blk, in_blk],
        out_specs=[out_blk, out_blk],
        out_shape=[jax.ShapeDtypeStruct((batch, m_len, dil * ATT_OUT), BF16),
                   jax.ShapeDtypeStruct((batch, m_len, dil * ATT_OUT), F32)],
        compiler_params=pltpu.CompilerParams(dimension_semantics=("parallel", "parallel"),
                                             vmem_limit_bytes=VMEM_LIMIT),
        name=f"attn_g{g}",
    )(view(aq), view(ak), view(av))
    T = batch * seq_len
    return o.reshape(T, ATT_OUT), lse.reshape(T, ATT_OUT)


def _mix_kernel(x_ref, yh_ref, o0_ref, o1_ref, o2_ref, l0_ref, l1_ref, l2_ref, gh_ref, ga_ref,
                wbh_ref, wba_ref, wo_ref, g1_ref, b1_ref, wrt_ref, rb_ref,
                x1_ref, tope_ref, gate_ref, rank_ref, cnt_ref, carry_scr, *, alpha):
    tm = x_ref.shape[0]

    @pl.when(pl.program_id(0) == 0)
    def _():
        carry_scr[...] = jnp.zeros_like(carry_scr)

    l0, l1, l2 = l0_ref[...], l1_ref[...], l2_ref[...]
    lm = jnp.maximum(jnp.maximum(l0, l1), l2)
    e0, e1, e2 = jnp.exp(l0 - lm), jnp.exp(l1 - lm), jnp.exp(l2 - lm)
    es = e0 + e1 + e2
    y_a = ((e0 / es) * o0_ref[...].astype(F32) + (e1 / es) * o1_ref[...].astype(F32)
           + (e2 / es) * o2_ref[...].astype(F32)).astype(BF16)
    merged = (gh_ref[...].astype(F32) * jnp.dot(yh_ref[...], wbh_ref[...], preferred_element_type=F32)
              + ga_ref[...].astype(F32) * jnp.dot(y_a, wba_ref[...], preferred_element_type=F32))
    mixed = jnp.dot(merged.astype(BF16), wo_ref[...], preferred_element_type=F32)
    x1 = _layer_norm(alpha * x_ref[...] + mixed, g1_ref[...], b1_ref[...])
    x1_ref[...] = x1

    logit = lax.dot_general(wrt_ref[...], x1.astype(BF16), NT_DIMS, preferred_element_type=F32)
    score = _sigmoid(logit)
    biased = score + rb_ref[...]
    b3 = biased.reshape(N_GROUPS, PER_GROUP, tm)
    i3 = lax.broadcasted_iota(jnp.int32, (N_GROUPS, PER_GROUP, tm), 1)
    m1 = jnp.max(b3, axis=1, keepdims=True)
    idx1 = jnp.min(jnp.where(b3 == m1, i3, PER_GROUP), axis=1, keepdims=True)
    m2 = jnp.max(jnp.where(i3 == idx1, -jnp.inf, b3), axis=1, keepdims=True)
    gs = (m1 + m2).reshape(N_GROUPS, tm)
    gi = lax.broadcasted_iota(jnp.int32, (N_GROUPS, tm), 0)
    sel = jnp.zeros((N_GROUPS, tm), F32)
    cur = gs
    for _ in range(TOPK_GROUPS):
        m = jnp.max(cur, axis=0, keepdims=True)
        ix = jnp.min(jnp.where(cur == m, gi, N_GROUPS), axis=0, keepdims=True)
        hit = gi == ix
        sel = jnp.where(hit, 1.0, sel)
        cur = jnp.where(hit, -jnp.inf, cur)
    cur = jnp.where(sel.reshape(N_GROUPS, 1, tm) > 0.0, b3, -jnp.inf).reshape(N_EXPERTS, tm)
    ei = lax.broadcasted_iota(jnp.int32, (N_EXPERTS, tm), 0)
    chosen = jnp.zeros((N_EXPERTS, tm), F32)
    top_e, top_s = [], []
    for _ in range(TOP_K):
        m = jnp.max(cur, axis=0, keepdims=True)
        ix = jnp.min(jnp.where(cur == m, ei, N_EXPERTS), axis=0, keepdims=True)
        hit = ei == ix
        top_e.append(ix)
        top_s.append(jnp.sum(jnp.where(hit, score, 0.0), axis=0, keepdims=True))
        chosen = jnp.where(hit, 1.0, chosen)
        cur = jnp.where(hit, -jnp.inf, cur)
    s_sum = top_s[0]
    for s in top_s[1:]:
        s_sum = s_sum + s
    tope_ref[...] = jnp.concatenate(top_e, axis=0)
    gate_ref[...] = jnp.concatenate([s / s_sum * ROUTED_SCALE for s in top_s], axis=0)

    su = lax.broadcasted_iota(jnp.int32, (tm, tm), 0)
    tu = lax.broadcasted_iota(jnp.int32, (tm, tm), 1)
    before = (su < tu).astype(BF16)
    prior = carry_scr[:, 0:1] + jnp.dot(chosen.astype(BF16), before, preferred_element_type=F32)
    ranks = [jnp.sum(jnp.where(ei == ix, prior, 0.0), axis=0, keepdims=True) for ix in top_e]
    rank_ref[...] = jnp.concatenate(ranks, axis=0).astype(jnp.int32)
    carry_scr[...] = carry_scr[...] + jnp.sum(chosen, axis=1, keepdims=True)
    cnt_ref[...] = carry_scr[...]


def _mix(x2d, y_h, att_o, att_l, gh, ga, wbh, wba, wo, g1, b1, wrt, rb, alpha):
    T = x2d.shape[0]
    tm = 256
    row = lambda w: pl.BlockSpec((tm, w), lambda i: (i, 0))
    col = pl.BlockSpec((TOP_K, tm), lambda i: (0, i))
    return pl.pallas_call(
        functools.partial(_mix_kernel, alpha=alpha),
        grid=(T // tm,),
        in_specs=[row(D_MODEL), row(HG_WIDTH)] + [row(ATT_OUT)] * 6 + [row(D_MODEL), row(D_MODEL),
                  _const_spec((HG_WIDTH, D_MODEL)), _const_spec((ATT_OUT, D_MODEL)),
                  _const_spec((D_MODEL, D_MODEL)), _const_spec((1, D_MODEL)), _const_spec((1, D_MODEL)),
                  _const_spec((N_EXPERTS, D_MODEL)), _const_spec((N_EXPERTS, 1))],
        out_specs=[row(D_MODEL), col, col, col, _const_spec((N_EXPERTS, LANES))],
        out_shape=[jax.ShapeDtypeStruct((T, D_MODEL), F32),
                   jax.ShapeDtypeStruct((TOP_K, T), jnp.int32),
                   jax.ShapeDtypeStruct((TOP_K, T), F32),
                   jax.ShapeDtypeStruct((TOP_K, T), jnp.int32),
                   jax.ShapeDtypeStruct((N_EXPERTS, LANES), F32)],
        scratch_shapes=[pltpu.VMEM((N_EXPERTS, LANES), F32)],
        compiler_params=pltpu.CompilerParams(dimension_semantics=("arbitrary",), vmem_limit_bytes=VMEM_LIMIT),
        name="mix",
    )(x2d, y_h, *att_o, *att_l, gh, ga, wbh, wba, wo, g1, b1, wrt, rb)


def _dispatch_kernel(dest_ref, x_hbm, xs_hbm, sem):
    tb = dest_ref.shape[1]
    base = pl.program_id(0) * tb

    def row_copy(t, k):
        return pltpu.make_async_copy(x_hbm.at[pl.ds(base + t, 1), :],
                                     xs_hbm.at[pl.ds(dest_ref[k, t], 1), :], sem)

    def issue(t, c):
        for k in range(TOP_K):
            row_copy(t, k).start()
        return c

    def drain(t, c):
        for k in range(TOP_K):
            row_copy(t, k).wait()
        return c

    lax.fori_loop(0, tb, issue, 0)
    lax.fori_loop(0, tb, drain, 0)


def _dispatch(dest, x1, n_slots):
    T = x1.shape[0]
    tb = 512
    return pl.pallas_call(
        _dispatch_kernel,
        grid=(T // tb,),
        in_specs=[pl.BlockSpec((TOP_K, tb), lambda i: (0, i), memory_space=pltpu.SMEM),
                  pl.BlockSpec(memory_space=pl.ANY)],
        out_specs=pl.BlockSpec(memory_space=pl.ANY),
        out_shape=jax.ShapeDtypeStruct((n_slots, D_MODEL), F32),
        scratch_shapes=[pltpu.SemaphoreType.DMA],
        compiler_params=pltpu.CompilerParams(dimension_semantics=("arbitrary",)),
        name="dispatch",
    )(dest, x1)


def _expert_kernel(be_ref, bv_ref, nu_ref, xs_ref, wg_ref, wu_ref, wd_ref, ys_ref, wg_b, wu_b, wd_b):
    i = pl.program_id(0)
    valid = bv_ref[i]

    @pl.when(valid > 0)
    def _():
        prev = be_ref[jnp.maximum(i - 1, 0)]

        @pl.when(jnp.logical_or(i == 0, be_ref[i] != prev))
        def _():
            wg_b[...] = wg_ref[0].astype(BF16)
            wu_b[...] = wu_ref[0].astype(BF16)
            wd_b[...] = wd_ref[0].astype(BF16)

        rows = lax.broadcasted_iota(jnp.int32, (SLOT_BLOCK, 1), 0)
        xb = jnp.where(rows < valid, xs_ref[...], 0.0).astype(BF16)
        h = _silu(jnp.dot(xb, wg_b[...], preferred_element_type=F32)) * jnp.dot(xb, wu_b[...], preferred_element_type=F32)
        ys_ref[...] = jnp.dot(h.astype(BF16), wd_b[...], preferred_element_type=F32)


def _experts(block_expert, block_valid, n_used, xs, w_gate, w_up, w_down):
    n_blocks = xs.shape[0] // SLOT_BLOCK
    slot_blk = pl.BlockSpec((SLOT_BLOCK, D_MODEL), lambda i, be, bv, nu: (jnp.minimum(i, nu[0] - 1), 0))
    return pl.pallas_call(
        _expert_kernel,
        grid_spec=pltpu.PrefetchScalarGridSpec(
            num_scalar_prefetch=3,
            grid=(n_blocks,),
            in_specs=[slot_blk,
                      pl.BlockSpec((1, D_MODEL, EXPERT_HIDDEN), lambda i, be, bv, nu: (be[i], 0, 0)),
                      pl.BlockSpec((1, D_MODEL, EXPERT_HIDDEN), lambda i, be, bv, nu: (be[i], 0, 0)),
                      pl.BlockSpec((1, EXPERT_HIDDEN, D_MODEL), lambda i, be, bv, nu: (be[i], 0, 0))],
            out_specs=slot_blk,
            scratch_shapes=[pltpu.VMEM((D_MODEL, EXPERT_HIDDEN), BF16),
                            pltpu.VMEM((D_MODEL, EXPERT_HIDDEN), BF16),
                            pltpu.VMEM((EXPERT_HIDDEN, D_MODEL), BF16)],
        ),
        out_shape=jax.ShapeDtypeStruct(xs.shape, F32),
        compiler_params=pltpu.CompilerParams(dimension_semantics=("arbitrary",), vmem_limit_bytes=VMEM_LIMIT),
        name="experts",
    )(block_expert, block_valid, n_used, xs, w_gate, w_up, w_down)


def _combine_kernel(dest_ref, x1_ref, gate_ref, p_ref, ys_hbm,
                    wsg_ref, wsu_ref, wsd_ref, g2_ref, b2_ref, wpg_ref, wpp_ref, g3_ref, b3_ref,
                    out_ref, yg, sem, *, alpha):
    tm = x1_ref.shape[0]

    def row_copy(t, k):
        return pltpu.make_async_copy(ys_hbm.at[pl.ds(dest_ref[k, t], 1), :],
                                     yg.at[k, pl.ds(t, 1), :], sem)

    def issue(t, c):
        for k in range(TOP_K):
            row_copy(t, k).start()
        return c

    def drain(t, c):
        for k in range(TOP_K):
            row_copy(t, k).wait()
        return c

    lax.fori_loop(0, tm, issue, 0)

    x1 = x1_ref[...]
    xb = x1.astype(BF16)
    hs = _silu(jnp.dot(xb, wsg_ref[...], preferred_element_type=F32)) * jnp.dot(xb, wsu_ref[...], preferred_element_type=F32)
    ffn = jnp.dot(hs.astype(BF16), wsd_ref[...], preferred_element_type=F32)

    lax.fori_loop(0, tm, drain, 0)
    gate = gate_ref[...]
    for k in range(TOP_K):
        ffn = ffn + gate[:, k:k + 1] * yg[k]
    x2 = _layer_norm(alpha * x1 + ffn, g2_ref[...], b2_ref[...])
    ple = (_sigmoid(jnp.dot(x2.astype(BF16), wpg_ref[...], preferred_element_type=F32))
           * jnp.dot(p_ref[...].astype(BF16), wpp_ref[...], preferred_element_type=F32))
    out_ref[...] = _layer_norm(alpha * x2 + ple, g3_ref[...], b3_ref[...])


def _combine(dest, x1, gate_t, p2d, ys, wsg, wsu, wsd, g2, b2, wpg, wpp, g3, b3, alpha):
    T = x1.shape[0]
    tm = 256
    row = lambda w: pl.BlockSpec((tm, w), lambda i: (i, 0))
    return pl.pallas_call(
        functools.partial(_combine_kernel, alpha=alpha),
        grid=(T // tm,),
        in_specs=[pl.BlockSpec((TOP_K, tm), lambda i: (0, i), memory_space=pltpu.SMEM),
                  row(D_MODEL), row(TOP_K), row(PLE_DIM),
                  pl.BlockSpec(memory_space=pl.ANY),
                  _const_spec((D_MODEL, SHARED_HIDDEN)), _const_spec((D_MODEL, SHARED_HIDDEN)),
                  _const_spec((SHARED_HIDDEN, D_MODEL)), _const_spec((1, D_MODEL)), _const_spec((1, D_MODEL)),
                  _const_spec((D_MODEL, D_MODEL)), _const_spec((PLE_DIM, D_MODEL)),
                  _const_spec((1, D_MODEL)), _const_spec((1, D_MODEL))],
        out_specs=row(D_MODEL),
        out_shape=jax.ShapeDtypeStruct((T, D_MODEL), F32),
        scratch_shapes=[pltpu.VMEM((TOP_K, tm, D_MODEL), F32), pltpu.SemaphoreType.DMA],
        compiler_params=pltpu.CompilerParams(dimension_semantics=("arbitrary",), vmem_limit_bytes=VMEM_LIMIT),
        name="combine",
    )(dest, x1, gate_t, p2d, ys, wsg, wsu, wsd, g2, b2, wpg, wpp, g3, b3)


def _rotary_tables(seq_len):
    half = ROT_DIM // 2
    inv_freq = ROPE_THETA ** (-jnp.arange(half, dtype=F32) / half)
    ang = jnp.arange(seq_len, dtype=F32)[:, None] * inv_freq[None, :]
    cos, sin = jnp.cos(ang), jnp.sin(ang)
    pad = ATT_HEAD_DIM - ROT_DIM
    one = jnp.ones((seq_len, pad), F32)
    zero = jnp.zeros((seq_len, pad + half), F32)
    rc = jnp.concatenate([cos, cos, one], axis=1)
    rsa = jnp.concatenate([-sin, zero], axis=1)
    rsb = jnp.concatenate([jnp.zeros((seq_len, half), F32), sin, jnp.zeros((seq_len, pad), F32)], axis=1)
    rep = lambda a: jnp.tile(a, (1, LANES // ATT_HEAD_DIM))
    return rep(rc), rep(rsa), rep(rsb)


def _slot_layout(counts, n_blocks):
    padded = (counts + SLOT_BLOCK - 1) // SLOT_BLOCK * SLOT_BLOCK
    padded_end = jnp.cumsum(padded)
    padded_start = padded_end - padded
    blk0 = jnp.arange(n_blocks, dtype=jnp.int32) * SLOT_BLOCK
    block_expert = jnp.minimum(jnp.searchsorted(padded_end, blk0, side="right"), N_EXPERTS - 1).astype(jnp.int32)
    n_used = (padded_end[-1] // SLOT_BLOCK).astype(jnp.int32)
    valid = jnp.clip(counts[block_expert] - (blk0 - padded_start[block_expert]), 0, SLOT_BLOCK)
    block_valid = jnp.where(jnp.arange(n_blocks) < n_used, valid, 0).astype(jnp.int32)
    return padded_start.astype(jnp.int32), block_expert, block_valid, n_used.reshape(1)


def kernel(x, p, w_in, hg_lb_fwd, hg_lb_bwd, hg_norm_g, w_branch_hg, w_branch_att, w_out, ln1_g, ln1_b, w_router, router_bias, w_exp_gate, w_exp_up, w_exp_down, w_sh_gate, w_sh_up, w_sh_down, ln2_g, ln2_b, w_ple_gate, w_ple_proj, ln3_g, ln3_b):
    B, L, D = x.shape
    depth = w_in.shape[0]
    T = B * L
    alpha = (2 * depth) ** 0.25
    n_assign = T * TOP_K
    n_blocks = (n_assign + N_EXPERTS * (SLOT_BLOCK - 1) + SLOT_BLOCK - 1) // SLOT_BLOCK
    lb_fwd_all = jnp.cumsum(jax.nn.softmax(hg_lb_fwd.astype(F32), axis=0), axis=0)
    lb_bwd_all = jnp.cumsum(jax.nn.softmax(hg_lb_bwd.astype(F32), axis=0), axis=0)
    rc, rsa, rsb = _rotary_tables(L)
    row = lambda a: a.reshape(1, -1)

    x2d = x.reshape(T, D)
    for i in range(depth):
        (hq, hv, lff, lfb, hg, aq, ak, av, gh, ga) = _in_proj(
            x2d, w_in[i].astype(BF16), row(lb_fwd_all[i]), row(lb_bwd_all[i]), rc, rsa, rsb, L)
        y_h = _hgrn(hq, hv, lff, lfb, hg, hg_norm_g[i], B, L).reshape(T, HG_WIDTH)
        att = [_attn_group(aq, ak, av, g, B, L) for g in range(len(DIL_PAIRS))]
        x1, top_e, gate, rank, cnt = _mix(
            x2d, y_h, [a[0] for a in att], [a[1] for a in att], gh, ga,
            w_branch_hg[i].astype(BF16), w_branch_att[i].astype(BF16), w_out[i].astype(BF16),
            row(ln1_g[i]), row(ln1_b[i]), w_router[i].T.astype(BF16), router_bias[i].reshape(N_EXPERTS, 1), alpha)
        counts = cnt[:, 0].astype(jnp.int32)
        padded_start, block_expert, block_valid, n_used = _slot_layout(counts, n_blocks)
        dest = padded_start[top_e] + rank
        xs = _dispatch(dest, x1, n_blocks * SLOT_BLOCK)
        ys = _experts(block_expert, block_valid, n_used, xs, w_exp_gate[i], w_exp_up[i], w_exp_down[i])
        x2d = _combine(dest, x1, gate.T, p[i].reshape(T, PLE_DIM), ys,
                       w_sh_gate[i].astype(BF16), w_sh_up[i].astype(BF16), w_sh_down[i].astype(BF16),
                       row(ln2_g[i]), row(ln2_b[i]), w_ple_gate[i].astype(BF16), w_ple_proj[i].astype(BF16),
                       row(ln3_g[i]), row(ln3_b[i]), alpha)
    return x2d.reshape(B, L, D)
```

```python
import functools

import jax
import jax.numpy as jnp
import numpy as np
from jax import lax
from jax.experimental import pallas as pl
from jax.experimental.pallas import tpu as pltpu

F32 = jnp.float32
BF16 = jnp.bfloat16

D_MODEL = 1024
HG_HEADS = 4
HG_DIM = 128
HG_WIDTH = HG_HEADS * HG_DIM
HG_CHUNK = 64
DIL_PAIRS = ((128, 1), (512, 4), (2048, 16))
ATT_SLOTS = 4
ATT_HEAD_DIM = 64
ATT_WIDTH = len(DIL_PAIRS) * ATT_SLOTS * ATT_HEAD_DIM
ATT_OUT = ATT_SLOTS * ATT_HEAD_DIM
ROT_DIM = ATT_HEAD_DIM // 4
ROPE_THETA = 500000.0
COL_SIZES = (HG_WIDTH,) * 5 + (ATT_WIDTH,) * 3 + (D_MODEL,) * 2
COL_STARTS = tuple(int(v) for v in np.cumsum((0,) + COL_SIZES)[:-1])
IN_COLS = sum(COL_SIZES)
N_EXPERTS = 256
TOP_K = 8
N_GROUPS = 8
TOPK_GROUPS = 4
PER_GROUP = N_EXPERTS // N_GROUPS
EXPERT_HIDDEN = 256
SHARED_HIDDEN = 256
ROUTED_SCALE = 2.5
SLOT_BLOCK = 256
PLE_DIM = 256
LN_EPS = 1e-5
NEG_INF = -1e30

LANES = 128
VMEM_LIMIT = 56 * 1024 * 1024

NT_DIMS = (((1,), (1,)), ((), ()))
TN_DIMS = (((0,), (0,)), ((), ()))


def _sigmoid(v):
    return jax.nn.sigmoid(v)


def _silu(v):
    return v * jax.nn.sigmoid(v)


def _layer_norm(v, g, b):
    mu = jnp.mean(v, axis=-1, keepdims=True)
    vc = v - mu
    var = jnp.mean(vc * vc, axis=-1, keepdims=True)
    return vc * lax.rsqrt(var + LN_EPS) * g + b


def _const_spec(shape):
    return pl.BlockSpec(shape, lambda *_: (0,) * len(shape))


def _in_proj_kernel(x_ref, w_ref, lbf_ref, lbb_ref, rc_ref, rsa_ref, rsb_ref,
                    hq_ref, hv_ref, lff_ref, lfb_ref, hg_ref,
                    aq_ref, ak_ref, av_ref, gh_ref, ga_ref):
    xb = x_ref[...].astype(BF16)

    def proj(seg):
        c0, width = COL_STARTS[seg], COL_SIZES[seg]
        return jnp.dot(xb, w_ref[:, c0:c0 + width], preferred_element_type=F32)

    hq_ref[...] = _silu(proj(0)).astype(BF16)
    hv_ref[...] = proj(1).astype(BF16)
    lb = lbf_ref[...]
    lff_ref[...] = jnp.log(lb + (1.0 - lb) * _sigmoid(proj(2)))
    lb = lbb_ref[...]
    lfb_ref[...] = jnp.log(lb + (1.0 - lb) * _sigmoid(proj(3)))
    hg_ref[...] = _silu(proj(4)).astype(BF16)

    rc, rsa, rsb = rc_ref[...], rsa_ref[...], rsb_ref[...]

    def rotary(z, out_ref, scale):
        for s in range(ATT_WIDTH // LANES):
            t = z[:, s * LANES:(s + 1) * LANES]
            up = pltpu.roll(t, LANES - ROT_DIM // 2, 1)
            dn = pltpu.roll(t, ROT_DIM // 2, 1)
            r = t * rc + up * rsa + dn * rsb
            out_ref[:, s * LANES:(s + 1) * LANES] = (r * scale).astype(BF16)

    rotary(proj(5), aq_ref, ATT_HEAD_DIM ** -0.5)
    rotary(proj(6), ak_ref, 1.0)
    av_ref[...] = proj(7).astype(BF16)
    gh_ref[...] = _sigmoid(proj(8)).astype(BF16)
    ga_ref[...] = _sigmoid(proj(9)).astype(BF16)


def _in_proj(x2d, w_b, lbf, lbb, rc, rsa, rsb, seq_len):
    T = x2d.shape[0]
    tm = 512
    n_pos_blocks = seq_len // tm
    row = lambda w: pl.BlockSpec((tm, w), lambda i: (i, 0))
    tab = pl.BlockSpec((tm, LANES), lambda i: (i % n_pos_blocks, 0))
    out_w = (HG_WIDTH, HG_WIDTH, HG_WIDTH, HG_WIDTH, HG_WIDTH, ATT_WIDTH, ATT_WIDTH, ATT_WIDTH, D_MODEL, D_MODEL)
    out_dt = (BF16, BF16, F32, F32, BF16, BF16, BF16, BF16, BF16, BF16)
    return pl.pallas_call(
        _in_proj_kernel,
        grid=(T // tm,),
        in_specs=[row(D_MODEL),
                  pl.BlockSpec((D_MODEL, IN_COLS), lambda i: (0, 0), pipeline_mode=pl.Buffered(1)),
                  _const_spec((1, HG_WIDTH)), _const_spec((1, HG_WIDTH)), tab, tab, tab],
        out_specs=[row(w) for w in out_w],
        out_shape=[jax.ShapeDtypeStruct((T, w), dt) for w, dt in zip(out_w, out_dt)],
        compiler_params=pltpu.CompilerParams(dimension_semantics=("parallel",), vmem_limit_bytes=VMEM_LIMIT),
        name="in_proj",
    )(x2d, w_b, lbf, lbb, rc, rsa, rsb)


def _split3(a):
    hi = a.astype(BF16)
    r1 = a - hi.astype(F32)
    mid = r1.astype(BF16)
    lo = (r1 - mid.astype(F32)).astype(BF16)
    return jnp.concatenate([hi, mid, lo], axis=1)


def _hgrn_kernel(q_ref, v_ref, lff_ref, lfb_ref, sg_ref, ng_ref, y_ref, of_scr, ob_scr):
    L = q_ref.shape[1]
    C = HG_CHUNK
    n = L // C
    r_i = lax.broadcasted_iota(jnp.int32, (C, C), 0)
    c_i = lax.broadcasted_iota(jnp.int32, (C, C), 1)
    lower = c_i <= r_i
    upper = c_i >= r_i
    lower_m = lower.astype(BF16)
    upper_m = upper.astype(BF16)

    def one_dir(c0, lf_ref, tri_m, mask, ref_row, last_row, st):
        q = q_ref[0, pl.ds(c0, C), :].astype(F32)
        v = v_ref[0, pl.ds(c0, C), :]
        lf = lf_ref[0, pl.ds(c0, C), :]
        p = jnp.dot(tri_m, _split3(lf), preferred_element_type=F32)
        b = p[:, :HG_DIM] + p[:, HG_DIM:2 * HG_DIM] + p[:, 2 * HG_DIM:]
        b_ref = b[ref_row:ref_row + 1, :]
        b_last = b[last_row:last_row + 1, :]
        k = 1.0 - jnp.exp(lf)
        a_q = (q * jnp.exp(b - b_ref)).astype(BF16)
        a_k = (k * jnp.exp(b_ref - b)).astype(BF16)
        s = lax.dot_general(a_q, a_k, NT_DIMS, preferred_element_type=F32)
        s = jnp.where(mask, s, 0.0).astype(BF16)
        o = jnp.dot(s, v, preferred_element_type=F32)
        q_in = (q * jnp.exp(b)).astype(BF16)
        o = o + lax.dot_general(q_in, st.astype(BF16), NT_DIMS, preferred_element_type=F32)
        k_out = (k * jnp.exp(b_last - b)).astype(BF16)
        st_new = st * jnp.exp(b_last) + lax.dot_general(v, k_out, TN_DIMS, preferred_element_type=F32)
        return o, st_new

    def body(c, carry):
        st_f, st_b = carry
        i0 = pl.multiple_of(c * C, C)
        j0 = pl.multiple_of((n - 1 - c) * C, C)
        o_f, st_f = one_dir(i0, lff_ref, lower_m, lower, C // 2 - 1, C - 1, st_f)
        o_b, st_b = one_dir(j0, lfb_ref, upper_m, upper, C // 2, 0, st_b)
        of_scr[pl.ds(i0, C), :] = o_f
        ob_scr[pl.ds(j0, C), :] = o_b
        return st_f, st_b

    z = jnp.zeros((HG_DIM, HG_DIM), F32)
    lax.fori_loop(0, n, body, (z, z))

    o = of_scr[...] + ob_scr[...]
    o = o * lax.rsqrt(jnp.mean(o * o, axis=-1, keepdims=True) + LN_EPS) * ng_ref[...]
    y_ref[0] = (o * sg_ref[0].astype(F32)).astype(BF16)


def _hgrn(hq, hv, lff, lfb, hg, norm_g, batch, seq_len):
    blk = pl.BlockSpec((1, seq_len, HG_DIM), lambda b, h: (b, 0, h))
    r3 = lambda a: a.reshape(batch, seq_len, HG_WIDTH)
    return pl.pallas_call(
        _hgrn_kernel,
        grid=(batch, HG_HEADS),
        in_specs=[blk, blk, blk, blk, blk, pl.BlockSpec((1, HG_DIM), lambda b, h: (0, h))],
        out_specs=blk,
        out_shape=jax.ShapeDtypeStruct((batch, seq_len, HG_WIDTH), BF16),
        scratch_shapes=[pltpu.VMEM((seq_len, HG_DIM), F32), pltpu.VMEM((seq_len, HG_DIM), F32)],
        compiler_params=pltpu.CompilerParams(dimension_semantics=("parallel", "parallel"),
                                             vmem_limit_bytes=VMEM_LIMIT),
        name="hgrn",
    )(r3(hq), r3(hv), r3(lff), r3(lfb), r3(hg), norm_g.reshape(1, HG_WIDTH))


def _attn_kernel(q_ref, k_ref, v_ref, o_ref, lse_ref, *scratch, m_len, dil, qb, kw, span):
    lane = lax.broadcasted_iota(jnp.int32, (1, LANES), 1)
    head0 = lane < ATT_HEAD_DIM
    qi_l = lax.broadcasted_iota(jnp.int32, (qb, kw), 0)
    ki_l = lax.broadcasted_iota(jnp.int32, (qb, kw), 1)

    def run_class(load_q, load_k, load_v, store_o, store_l):
        def block(i, carry):
            q0 = pl.multiple_of(i * qb, qb)
            ks = pl.multiple_of(jnp.clip(q0 - span, 0, m_len - kw), span)
            mask = jnp.abs((q0 + qi_l) - (ks + ki_l)) <= span
            for slab in range(ATT_OUT // LANES):
                cols = slice(slab * LANES, (slab + 1) * LANES)
                q2, k2, v2 = load_q(q0, qb, cols), load_k(ks, kw, cols), load_v(ks, kw, cols)
                outs, lses = [], []
                for hm in (head0, jnp.logical_not(head0)):
                    qm = jnp.where(hm, q2, jnp.zeros_like(q2))
                    s = lax.dot_general(qm, k2, NT_DIMS, preferred_element_type=F32)
                    s = jnp.where(mask, s, NEG_INF)
                    mx = jnp.max(s, axis=-1, keepdims=True)
                    e = jnp.exp(s - mx)
                    den = jnp.sum(e, axis=-1, keepdims=True)
                    outs.append(jnp.dot(e.astype(BF16), v2, preferred_element_type=F32) / den)
                    lses.append(mx + jnp.log(den))
                store_o(q0, qb, cols, jnp.where(head0, outs[0], outs[1]))
                store_l(q0, qb, cols, jnp.where(head0, lses[0], lses[1]))
            return carry

        lax.fori_loop(0, m_len // qb, block, 0)

    def loader(ref):
        return lambda r0, n, cols: ref[0, pl.ds(r0, n), cols]

    if dil == 1:
        def store_o(r0, n, cols, val):
            o_ref[0, pl.ds(r0, n), cols] = val.astype(BF16)

        def store_l(r0, n, cols, val):
            lse_ref[0, pl.ds(r0, n), cols] = val

        run_class(loader(q_ref), loader(k_ref), loader(v_ref), store_o, store_l)
        return

    q32, k32, v32, o32, l32, qc, kc, vc, oc, lc = scratch
    n_slab = ATT_OUT // LANES
    for slab in range(n_slab):
        cols = slice(slab * LANES, (slab + 1) * LANES)
        q32[slab] = q_ref[0, :, cols].astype(F32)
        k32[slab] = k_ref[0, :, cols].astype(F32)
        v32[slab] = v_ref[0, :, cols].astype(F32)

    def cls_loader(ref):
        return lambda r0, n, cols: ref[pl.ds(r0, n), cols]

    def store_oc(r0, n, cols, val):
        oc[pl.ds(r0, n), cols] = val

    def store_lc(r0, n, cols, val):
        lc[pl.ds(r0, n), cols] = val

    for r in range(dil):
        rows = pl.ds(r, m_len, stride=dil)
        for slab in range(n_slab):
            cols = slice(slab * LANES, (slab + 1) * LANES)
            qc[:, cols] = q32[slab, rows, :].astype(BF16)
            kc[:, cols] = k32[slab, rows, :].astype(BF16)
            vc[:, cols] = v32[slab, rows, :].astype(BF16)
        run_class(cls_loader(qc), cls_loader(kc), cls_loader(vc), store_oc, store_lc)
        for slab in range(n_slab):
            cols = slice(slab * LANES, (slab + 1) * LANES)
            o32[slab, rows, :] = oc[:, cols]
            l32[slab, rows, :] = lc[:, cols]
    for slab in range(n_slab):
        cols = slice(slab * LANES, (slab + 1) * LANES)
        o_ref[0, :, cols] = o32[slab].astype(BF16)
        lse_ref[0, :, cols] = l32[slab]


def _attn_group(aq, ak, av, g, batch, seq_len):
    window, dil = DIL_PAIRS[g]
    span = window // (2 * dil)
    m_len = seq_len // dil
    qb = min(2 * span, m_len)
    kw = min(qb + 2 * span, m_len)
    view = lambda a: a.reshape(batch, seq_len, ATT_WIDTH)
    in_blk = pl.BlockSpec((1, seq_len, ATT_OUT), lambda b: (b, 0, g))
    out_blk = pl.BlockSpec((1, seq_len, ATT_OUT), lambda b: (b, 0, 0))
    scratch = []
    if dil > 1:
        scratch = ([pltpu.VMEM((ATT_OUT // LANES, seq_len, LANES), F32)] * 5 + [pltpu.VMEM((m_len, ATT_OUT), BF16)] * 3
                   + [pltpu.VMEM((m_len, ATT_OUT), F32)] * 2)
    o, lse = pl.pallas_call(
        functools.partial(_attn_kernel, m_len=m_len, dil=dil, qb=qb, kw=kw, span=span),
        grid=(batch,),
        in_specs=[in_blk, in_blk, in_blk],
        out_specs=[out_blk, out_blk],
        out_shape=[jax.ShapeDtypeStruct((batch, seq_len, ATT_OUT), BF16),
                   jax.ShapeDtypeStruct((batch, seq_len, ATT_OUT), F32)],
        scratch_shapes=scratch,
        compiler_params=pltpu.CompilerParams(dimension_semantics=("parallel",), vmem_limit_bytes=VMEM_LIMIT),
        name=f"attn_g{g}",
    )(view(aq), view(ak), view(av))
    T = batch * seq_len
    return o.reshape(T, ATT_OUT), lse.reshape(T, ATT_OUT)


def _mix_kernel(x_ref, yh_ref, o0_ref, o1_ref, o2_ref, l0_ref, l1_ref, l2_ref, gh_ref, ga_ref,
                wbh_ref, wba_ref, wo_ref, g1_ref, b1_ref, wrt_ref, rb_ref,
                x1_ref, tope_ref, gate_ref, rank_ref, cnt_ref, carry_scr, *, alpha):
    tm = x_ref.shape[0]

    @pl.when(pl.program_id(0) == 0)
    def _():
        carry_scr[...] = jnp.zeros_like(carry_scr)

    l0, l1, l2 = l0_ref[...], l1_ref[...], l2_ref[...]
    lm = jnp.maximum(jnp.maximum(l0, l1), l2)
    e0, e1, e2 = jnp.exp(l0 - lm), jnp.exp(l1 - lm), jnp.exp(l2 - lm)
    es = e0 + e1 + e2
    y_a = ((e0 / es) * o0_ref[...].astype(F32) + (e1 / es) * o1_ref[...].astype(F32)
           + (e2 / es) * o2_ref[...].astype(F32)).astype(BF16)
    merged = (gh_ref[...].astype(F32) * jnp.dot(yh_ref[...], wbh_ref[...], preferred_element_type=F32)
              + ga_ref[...].astype(F32) * jnp.dot(y_a, wba_ref[...], preferred_element_type=F32))
    mixed = jnp.dot(merged.astype(BF16), wo_ref[...], preferred_element_type=F32)
    x1 = _layer_norm(alpha * x_ref[...] + mixed, g1_ref[...], b1_ref[...])
    x1_ref[...] = x1

    logit = lax.dot_general(wrt_ref[...], x1.astype(BF16), NT_DIMS, preferred_element_type=F32)
    score = _sigmoid(logit)
    biased = score + rb_ref[...]
    b3 = biased.reshape(N_GROUPS, PER_GROUP, tm)
    i3 = lax.broadcasted_iota(jnp.int32, (N_GROUPS, PER_GROUP, tm), 1)
    m1 = jnp.max(b3, axis=1, keepdims=True)
    idx1 = jnp.min(jnp.where(b3 == m1, i3, PER_GROUP), axis=1, keepdims=True)
    m2 = jnp.max(jnp.where(i3 == idx1, -jnp.inf, b3), axis=1, keepdims=True)
    gs = (m1 + m2).reshape(N_GROUPS, tm)
    gi = lax.broadcasted_iota(jnp.int32, (N_GROUPS, tm), 0)
    sel = jnp.zeros((N_GROUPS, tm), F32)
    cur = gs
    for _ in range(TOPK_GROUPS):
        m = jnp.max(cur, axis=0, keepdims=True)
        ix = jnp.min(jnp.where(cur == m, gi, N_GROUPS), axis=0, keepdims=True)
        hit = gi == ix
        sel = jnp.where(hit, 1.0, sel)
        cur = jnp.where(hit, -jnp.inf, cur)
    cur = jnp.where(sel.reshape(N_GROUPS, 1, tm) > 0.0, b3, -jnp.inf).reshape(N_EXPERTS, tm)
    ei = lax.broadcasted_iota(jnp.int32, (N_EXPERTS, tm), 0)
    chosen = jnp.zeros((N_EXPERTS, tm), F32)
    top_e, top_s = [], []
    for _ in range(TOP_K):
        m = jnp.max(cur, axis=0, keepdims=True)
        ix = jnp.min(jnp.where(cur == m, ei, N_EXPERTS), axis=0, keepdims=True)
        hit = ei == ix
        top_e.append(ix)
        top_s.append(jnp.sum(jnp.where(hit, score, 0.0), axis=0, keepdims=True))
        chosen = jnp.where(hit, 1.0, chosen)
        cur = jnp.where(hit, -jnp.inf, cur)
    s_sum = top_s[0]
    for s in top_s[1:]:
        s_sum = s_sum + s
    tope_ref[...] = jnp.concatenate(top_e, axis=0)
    gate_ref[...] = jnp.concatenate([s / s_sum * ROUTED_SCALE for s in top_s], axis=0)

    su = lax.broadcasted_iota(jnp.int32, (tm, tm), 0)
    tu = lax.broadcasted_iota(jnp.int32, (tm, tm), 1)
    before = (su < tu).astype(BF16)
    prior = carry_scr[:, 0:1] + jnp.dot(chosen.astype(BF16), before, preferred_element_type=F32)
    ranks = [jnp.sum(jnp.where(ei == ix, prior, 0.0), axis=0, keepdims=True) for ix in top_e]
    rank_ref[...] = jnp.concatenate(ranks, axis=0).astype(jnp.int32)
    carry_scr[...] = carry_scr[...] + jnp.sum(chosen, axis=1, keepdims=True)
    cnt_ref[...] = carry_scr[...]


def _mix(x2d, y_h, att_o, att_l, gh, ga, wbh, wba, wo, g1, b1, wrt, rb, alpha):
    T = x2d.shape[0]
    tm = 256
    row = lambda w: pl.BlockSpec((tm, w), lambda i: (i, 0))
    col = pl.BlockSpec((TOP_K, tm), lambda i: (0, i))
    return pl.pallas_call(
        functools.partial(_mix_kernel, alpha=alpha),
        grid=(T // tm,),
        in_specs=[row(D_MODEL), row(HG_WIDTH)] + [row(ATT_OUT)] * 6 + [row(D_MODEL), row(D_MODEL),
                  _const_spec((HG_WIDTH, D_MODEL)), _const_spec((ATT_OUT, D_MODEL)),
                  _const_spec((D_MODEL, D_MODEL)), _const_spec((1, D_MODEL)), _const_spec((1, D_MODEL)),
                  _const_spec((N_EXPERTS, D_MODEL)), _const_spec((N_EXPERTS, 1))],
        out_specs=[row(D_MODEL), col, col, col, _const_spec((N_EXPERTS, LANES))],
        out_shape=[jax.ShapeDtypeStruct((T, D_MODEL), F32),
                   jax.ShapeDtypeStruct((TOP_K, T), jnp.int32),
                   jax.ShapeDtypeStruct((TOP_K, T), F32),
                   jax.ShapeDtypeStruct((TOP_K, T), jnp.int32),
                   jax.ShapeDtypeStruct((N_EXPERTS, LANES), F32)],
        scratch_shapes=[pltpu.VMEM((N_EXPERTS, LANES), F32)],
        compiler_params=pltpu.CompilerParams(dimension_semantics=("arbitrary",), vmem_limit_bytes=VMEM_LIMIT),
        name="mix",
    )(x2d, y_h, *att_o, *att_l, gh, ga, wbh, wba, wo, g1, b1, wrt, rb)


def _dest_kernel(tope_ref, rank_ref, ps_ref, dest_ref):
    tb = tope_ref.shape[1]
    ei = lax.broadcasted_iota(jnp.int32, (N_EXPERTS, tb), 0)
    ps = ps_ref[...]
    starts = [jnp.sum(jnp.where(ei == tope_ref[k:k + 1, :], ps, 0.0), axis=0, keepdims=True)
              for k in range(TOP_K)]
    dest_ref[...] = jnp.concatenate(starts, axis=0).astype(jnp.int32) + rank_ref[...]


def _dest(top_e, rank, padded_start):
    T = top_e.shape[1]
    tb = 512
    col = pl.BlockSpec((TOP_K, tb), lambda i: (0, i))
    return pl.pallas_call(
        _dest_kernel,
        grid=(T // tb,),
        in_specs=[col, col, _const_spec((N_EXPERTS, 1))],
        out_specs=col,
        out_shape=jax.ShapeDtypeStruct((TOP_K, T), jnp.int32),
        compiler_params=pltpu.CompilerParams(dimension_semantics=("parallel",)),
        name="dest",
    )(top_e, rank, padded_start.astype(F32).reshape(N_EXPERTS, 1))


def _dispatch_kernel(dest_ref, x_ref, xs_hbm, sem):
    tb = x_ref.shape[0]

    def issue(t, c):
        for k in range(TOP_K):
            pltpu.make_async_copy(x_ref.at[pl.ds(t, 1), :], xs_hbm.at[pl.ds(dest_ref[k, t], 1), :], sem).start()
        return c

    lax.fori_loop(0, tb, issue, 0)
    for _ in range(TOP_K):
        pltpu.make_async_copy(x_ref, xs_hbm.at[pl.ds(0, tb), :], sem).wait()


def _dispatch(dest, x1, n_slots):
    T = x1.shape[0]
    tb = 512
    return pl.pallas_call(
        _dispatch_kernel,
        grid=(T // tb,),
        in_specs=[pl.BlockSpec((TOP_K, tb), lambda i: (0, i), memory_space=pltpu.SMEM),
                  pl.BlockSpec((tb, D_MODEL), lambda i: (i, 0))],
        out_specs=pl.BlockSpec(memory_space=pl.ANY),
        out_shape=jax.ShapeDtypeStruct((n_slots, D_MODEL), F32),
        scratch_shapes=[pltpu.SemaphoreType.DMA],
        compiler_params=pltpu.CompilerParams(dimension_semantics=("arbitrary",)),
        name="dispatch",
    )(dest, x1)


def _expert_kernel(be_ref, bv_ref, nu_ref, xs_ref, wg_ref, wu_ref, wd_ref, ys_ref, wg_b, wu_b, wd_b):
    i = pl.program_id(0)
    valid = bv_ref[i]

    @pl.when(valid > 0)
    def _():
        prev = be_ref[jnp.maximum(i - 1, 0)]

        @pl.when(jnp.logical_or(i == 0, be_ref[i] != prev))
        def _():
            wg_b[...] = wg_ref[0].astype(BF16)
            wu_b[...] = wu_ref[0].astype(BF16)
            wd_b[...] = wd_ref[0].astype(BF16)

        rows = lax.broadcasted_iota(jnp.int32, (SLOT_BLOCK, 1), 0)
        xb = jnp.where(rows < valid, xs_ref[...], 0.0).astype(BF16)
        h = _silu(jnp.dot(xb, wg_b[...], preferred_element_type=F32)) * jnp.dot(xb, wu_b[...], preferred_element_type=F32)
        ys_ref[...] = jnp.dot(h.astype(BF16), wd_b[...], preferred_element_type=F32)


def _experts(block_expert, block_valid, n_used, xs, w_gate, w_up, w_down):
    n_blocks = xs.shape[0] // SLOT_BLOCK
    slot_blk = pl.BlockSpec((SLOT_BLOCK, D_MODEL), lambda i, be, bv, nu: (jnp.minimum(i, nu[0] - 1), 0))
    return pl.pallas_call(
        _expert_kernel,
        grid_spec=pltpu.PrefetchScalarGridSpec(
            num_scalar_prefetch=3,
            grid=(n_blocks,),
            in_specs=[slot_blk,
                      pl.BlockSpec((1, D_MODEL, EXPERT_HIDDEN), lambda i, be, bv, nu: (be[i], 0, 0)),
                      pl.BlockSpec((1, D_MODEL, EXPERT_HIDDEN), lambda i, be, bv, nu: (be[i], 0, 0)),
                      pl.BlockSpec((1, EXPERT_HIDDEN, D_MODEL), lambda i, be, bv, nu: (be[i], 0, 0))],
            out_specs=slot_blk,
            scratch_shapes=[pltpu.VMEM((D_MODEL, EXPERT_HIDDEN), BF16),
                            pltpu.VMEM((D_MODEL, EXPERT_HIDDEN), BF16),
                            pltpu.VMEM((EXPERT_HIDDEN, D_MODEL), BF16)],
        ),
        out_shape=jax.ShapeDtypeStruct(xs.shape, F32),
        compiler_params=pltpu.CompilerParams(dimension_semantics=("arbitrary",), vmem_limit_bytes=VMEM_LIMIT),
        name="experts",
    )(block_expert, block_valid, n_used, xs, w_gate, w_up, w_down)


def _combine_kernel(dest_ref, x1_ref, gate_ref, p_ref, ys_hbm,
                    wsg_ref, wsu_ref, wsd_ref, g2_ref, b2_ref, wpg_ref, wpp_ref, g3_ref, b3_ref,
                    out_ref, yg, sem, *, alpha):
    tm = x1_ref.shape[0]

    def row_copy(t, k):
        return pltpu.make_async_copy(ys_hbm.at[pl.ds(dest_ref[k, t], 1), :],
                                     yg.at[k, pl.ds(t, 1), :], sem)

    def issue(t, c):
        for k in range(TOP_K):
            row_copy(t, k).start()
        return c

    def drain(t, c):
        for k in range(TOP_K):
            row_copy(t, k).wait()
        return c

    lax.fori_loop(0, tm, issue, 0)

    x1 = x1_ref[...]
    xb = x1.astype(BF16)
    hs = _silu(jnp.dot(xb, wsg_ref[...], preferred_element_type=F32)) * jnp.dot(xb, wsu_ref[...], preferred_element_type=F32)
    ffn = jnp.dot(hs.astype(BF16), wsd_ref[...], preferred_element_type=F32)

    lax.fori_loop(0, tm, drain, 0)
    gate = gate_ref[...]
    for k in range(TOP_K):
        ffn = ffn + gate[:, k:k + 1] * yg[k]
    x2 = _layer_norm(alpha * x1 + ffn, g2_ref[...], b2_ref[...])
    ple = (_sigmoid(jnp.dot(x2.astype(BF16), wpg_ref[...], preferred_element_type=F32))
           * jnp.dot(p_ref[...].astype(BF16), wpp_ref[...], preferred_element_type=F32))
    out_ref[...] = _layer_norm(alpha * x2 + ple, g3_ref[...], b3_ref[...])


def _combine(dest, x1, gate_t, p2d, ys, wsg, wsu, wsd, g2, b2, wpg, wpp, g3, b3, alpha):
    T = x1.shape[0]
    tm = 256
    row = lambda w: pl.BlockSpec((tm, w), lambda i: (i, 0))
    return pl.pallas_call(
        functools.partial(_combine_kernel, alpha=alpha),
        grid=(T // tm,),
        in_specs=[pl.BlockSpec((TOP_K, tm), lambda i: (0, i), memory_space=pltpu.SMEM),
                  row(D_MODEL), row(TOP_K), row(PLE_DIM),
                  pl.BlockSpec(memory_space=pl.ANY),
                  _const_spec((D_MODEL, SHARED_HIDDEN)), _const_spec((D_MODEL, SHARED_HIDDEN)),
                  _const_spec((SHARED_HIDDEN, D_MODEL)), _const_spec((1, D_MODEL)), _const_spec((1, D_MODEL)),
                  _const_spec((D_MODEL, D_MODEL)), _const_spec((PLE_DIM, D_MODEL)),
                  _const_spec((1, D_MODEL)), _const_spec((1, D_MODEL))],
        out_specs=row(D_MODEL),
        out_shape=jax.ShapeDtypeStruct((T, D_MODEL), F32),
        scratch_shapes=[pltpu.VMEM((TOP_K, tm, D_MODEL), F32), pltpu.SemaphoreType.DMA],
        compiler_params=pltpu.CompilerParams(dimension_semantics=("arbitrary",), vmem_limit_bytes=VMEM_LIMIT),
        name="combine",
    )(dest, x1, gate_t, p2d, ys, wsg, wsu, wsd, g2, b2, wpg, wpp, g3, b3)


def _rotary_tables(seq_len):
    half = ROT_DIM // 2
    inv_freq = ROPE_THETA ** (-jnp.arange(half, dtype=F32) / half)
    ang = jnp.arange(seq_len, dtype=F32)[:, None] * inv_freq[None, :]
    cos, sin = jnp.cos(ang), jnp.sin(ang)
    pad = ATT_HEAD_DIM - ROT_DIM
    one = jnp.ones((seq_len, pad), F32)
    zero = jnp.zeros((seq_len, pad + half), F32)
    rc = jnp.concatenate([cos, cos, one], axis=1)
    rsa = jnp.concatenate([-sin, zero], axis=1)
    rsb = jnp.concatenate([jnp.zeros((seq_len, half), F32), sin, jnp.zeros((seq_len, pad), F32)], axis=1)
    rep = lambda a: jnp.tile(a, (1, LANES // ATT_HEAD_DIM))
    return rep(rc), rep(rsa), rep(rsb)


def _slot_layout(counts, n_blocks):
    padded = (counts + SLOT_BLOCK - 1) // SLOT_BLOCK * SLOT_BLOCK
    padded_end = jnp.cumsum(padded)
    padded_start = padded_end - padded
    blk0 = jnp.arange(n_blocks, dtype=jnp.int32) * SLOT_BLOCK
    block_expert = jnp.minimum(jnp.searchsorted(padded_end, blk0, side="right"), N_EXPERTS - 1).astype(jnp.int32)
    n_used = (padded_end[-1] // SLOT_BLOCK).astype(jnp.int32)
    valid = jnp.clip(counts[block_expert] - (blk0 - padded_start[block_expert]), 0, SLOT_BLOCK)
    block_valid = jnp.where(jnp.arange(n_blocks) < n_used, valid, 0).astype(jnp.int32)
    return padded_start.astype(jnp.int32), block_expert, block_valid, n_used.reshape(1)


def kernel(x, p, w_in, hg_lb_fwd, hg_lb_bwd, hg_norm_g, w_branch_hg, w_branch_att, w_out, ln1_g, ln1_b, w_router, router_bias, w_exp_gate, w_exp_up, w_exp_down, w_sh_gate, w_sh_up, w_sh_down, ln2_g, ln2_b, w_ple_gate, w_ple_proj, ln3_g, ln3_b):
    B, L, D = x.shape
    depth = w_in.shape[0]
    T = B * L
    alpha = (2 * depth) ** 0.25
    n_assign = T * TOP_K
    n_blocks = (n_assign + N_EXPERTS * (SLOT_BLOCK - 1) + SLOT_BLOCK - 1) // SLOT_BLOCK
    lb_fwd_all = jnp.cumsum(jax.nn.softmax(hg_lb_fwd.astype(F32), axis=0), axis=0)
    lb_bwd_all = jnp.cumsum(jax.nn.softmax(hg_lb_bwd.astype(F32), axis=0), axis=0)
    rc, rsa, rsb = _rotary_tables(L)
    row = lambda a: a.reshape(1, -1)

    x2d = x.reshape(T, D)
    for i in range(depth):
        (hq, hv, lff, lfb, hg, aq, ak, av, gh, ga) = _in_proj(
            x2d, w_in[i].astype(BF16), row(lb_fwd_all[i]), row(lb_bwd_all[i]), rc, rsa, rsb, L)
        y_h = _hgrn(hq, hv, lff, lfb, hg, hg_norm_g[i], B, L).reshape(T, HG_WIDTH)
        att = [_attn_group(aq, ak, av, g, B, L) for g in range(len(DIL_PAIRS))]
        x1, top_e, gate, rank, cnt = _mix(
            x2d, y_h, [a[0] for a in att], [a[1] for a in att], gh, ga,
            w_branch_hg[i].astype(BF16), w_branch_att[i].astype(BF16), w_out[i].astype(BF16),
            row(ln1_g[i]), row(ln1_b[i]), w_router[i].T.astype(BF16), router_bias[i].reshape(N_EXPERTS, 1), alpha)
        counts = cnt[:, 0].astype(jnp.int32)
        padded_start, block_expert, block_valid, n_used = _slot_layout(counts, n_blocks)
        dest = _dest(top_e, rank, padded_start)
        xs = _dispatch(dest, x1, n_blocks * SLOT_BLOCK)
        ys = _experts(block_expert, block_valid, n_used, xs, w_exp_gate[i], w_exp_up[i], w_exp_down[i])
        x2d = _combine(dest, x1, gate.T, p[i].reshape(T, PLE_DIM), ys,
                       w_sh_gate[i].astype(BF16), w_sh_up[i].astype(BF16), w_sh_down[i].astype(BF16),
                       row(ln2_g[i]), row(ln2_b[i]), w_ple_gate[i].astype(BF16), w_ple_proj[i].astype(BF16),
                       row(ln3_g[i]), row(ln3_b[i]), alpha)
    return x2d.reshape(B, L, D)
```

```python
import functools

import jax
import jax.numpy as jnp
import numpy as np
from jax import lax
from jax.experimental import pallas as pl
from jax.experimental.pallas import tpu as pltpu

F32 = jnp.float32
BF16 = jnp.bfloat16
U32 = jnp.uint32

D_MODEL = 1024
HG_HEADS = 4
HG_DIM = 128
HG_WIDTH = HG_HEADS * HG_DIM
HG_CHUNK = 64
DIL_PAIRS = ((128, 1), (512, 4), (2048, 16))
ATT_SLOTS = 4
ATT_HEAD_DIM = 64
ATT_WIDTH = len(DIL_PAIRS) * ATT_SLOTS * ATT_HEAD_DIM
ATT_OUT = ATT_SLOTS * ATT_HEAD_DIM
ROT_DIM = ATT_HEAD_DIM // 4
ROPE_THETA = 500000.0
COL_SIZES = (HG_WIDTH,) * 5 + (ATT_WIDTH,) * 3 + (D_MODEL,) * 2
COL_STARTS = tuple(int(v) for v in np.cumsum((0,) + COL_SIZES)[:-1])
IN_COLS = sum(COL_SIZES)
N_EXPERTS = 256
TOP_K = 8
N_GROUPS = 8
TOPK_GROUPS = 4
PER_GROUP = N_EXPERTS // N_GROUPS
EXPERT_HIDDEN = 256
SHARED_HIDDEN = 256
ROUTED_SCALE = 2.5
SLOT_BLOCK = 256
D_PACKED = D_MODEL // 2
PLE_DIM = 256
LN_EPS = 1e-5
NEG_INF = -1e30

LANES = 128
VMEM_LIMIT = 56 * 1024 * 1024

NT_DIMS = (((1,), (1,)), ((), ()))
TN_DIMS = (((0,), (0,)), ((), ()))


def _sigmoid(v):
    return jax.nn.sigmoid(v)


def _silu(v):
    return v * jax.nn.sigmoid(v)


def _layer_norm(v, g, b):
    mu = jnp.mean(v, axis=-1, keepdims=True)
    vc = v - mu
    var = jnp.mean(vc * vc, axis=-1, keepdims=True)
    return vc * lax.rsqrt(var + LN_EPS) * g + b


def _pack_halves(v):
    w = v.shape[1] // 2
    lo = lax.bitcast_convert_type(v[:, :w].astype(BF16).astype(F32), U32)
    hi = lax.bitcast_convert_type(v[:, w:].astype(BF16).astype(F32), U32)
    return jnp.bitwise_or(jnp.bitwise_and(hi, jnp.uint32(0xFFFF0000)), jnp.right_shift(lo, jnp.uint32(16)))


def _unpack_halves(p):
    lo = lax.bitcast_convert_type(jnp.left_shift(p, jnp.uint32(16)), F32)
    hi = lax.bitcast_convert_type(jnp.bitwise_and(p, jnp.uint32(0xFFFF0000)), F32)
    return lo, hi


def _const_spec(shape):
    return pl.BlockSpec(shape, lambda *_: (0,) * len(shape))


def _in_proj_kernel(x_ref, w_ref, lbf_ref, lbb_ref, rc_ref, rsa_ref, rsb_ref,
                    hq_ref, hv_ref, lff_ref, lfb_ref, hg_ref,
                    aq_ref, ak_ref, av_ref, gh_ref, ga_ref):
    xb = x_ref[...].astype(BF16)

    def proj(seg):
        c0, width = COL_STARTS[seg], COL_SIZES[seg]
        return jnp.dot(xb, w_ref[:, c0:c0 + width], preferred_element_type=F32)

    hq_ref[...] = _silu(proj(0)).astype(BF16)
    hv_ref[...] = proj(1).astype(BF16)
    lb = lbf_ref[...]
    lff_ref[...] = jnp.log(lb + (1.0 - lb) * _sigmoid(proj(2)))
    lb = lbb_ref[...]
    lfb_ref[...] = jnp.log(lb + (1.0 - lb) * _sigmoid(proj(3)))
    hg_ref[...] = _silu(proj(4)).astype(BF16)

    rc, rsa, rsb = rc_ref[...], rsa_ref[...], rsb_ref[...]

    def rotary(z, out_ref, scale):
        for s in range(ATT_WIDTH // LANES):
            t = z[:, s * LANES:(s + 1) * LANES]
            up = pltpu.roll(t, LANES - ROT_DIM // 2, 1)
            dn = pltpu.roll(t, ROT_DIM // 2, 1)
            r = t * rc + up * rsa + dn * rsb
            out_ref[:, s * LANES:(s + 1) * LANES] = (r * scale).astype(BF16)

    rotary(proj(5), aq_ref, ATT_HEAD_DIM ** -0.5)
    rotary(proj(6), ak_ref, 1.0)
    av_ref[...] = proj(7).astype(BF16)
    gh_ref[...] = _sigmoid(proj(8)).astype(BF16)
    ga_ref[...] = _sigmoid(proj(9)).astype(BF16)


def _in_proj(x2d, w_b, lbf, lbb, rc, rsa, rsb, seq_len):
    T = x2d.shape[0]
    tm = 512
    n_pos_blocks = seq_len // tm
    row = lambda w: pl.BlockSpec((tm, w), lambda i: (i, 0))
    tab = pl.BlockSpec((tm, LANES), lambda i: (i % n_pos_blocks, 0))
    out_w = (HG_WIDTH, HG_WIDTH, HG_WIDTH, HG_WIDTH, HG_WIDTH, ATT_WIDTH, ATT_WIDTH, ATT_WIDTH, D_MODEL, D_MODEL)
    out_dt = (BF16, BF16, F32, F32, BF16, BF16, BF16, BF16, BF16, BF16)
    return pl.pallas_call(
        _in_proj_kernel,
        grid=(T // tm,),
        in_specs=[row(D_MODEL),
                  pl.BlockSpec((D_MODEL, IN_COLS), lambda i: (0, 0), pipeline_mode=pl.Buffered(1)),
                  _const_spec((1, HG_WIDTH)), _const_spec((1, HG_WIDTH)), tab, tab, tab],
        out_specs=[row(w) for w in out_w],
        out_shape=[jax.ShapeDtypeStruct((T, w), dt) for w, dt in zip(out_w, out_dt)],
        compiler_params=pltpu.CompilerParams(dimension_semantics=("parallel",), vmem_limit_bytes=VMEM_LIMIT),
        name="in_proj",
    )(x2d, w_b, lbf, lbb, rc, rsa, rsb)


def _split3(a):
    hi = a.astype(BF16)
    r1 = a - hi.astype(F32)
    mid = r1.astype(BF16)
    lo = (r1 - mid.astype(F32)).astype(BF16)
    return jnp.concatenate([hi, mid, lo], axis=1)


HG_SUPER = 4 * HG_CHUNK


def _hgrn_kernel(q_ref, v_ref, lff_ref, lfb_ref, sg_ref, ng_ref, y_ref,
                 of_scr, ob_scr, qf_scr, qb_scr, df_scr, db_scr, mf_scr, mb_scr):
    L = q_ref.shape[1]
    C, SC = HG_CHUNK, HG_SUPER
    n = L // C
    r_i = lax.broadcasted_iota(jnp.int32, (SC, SC), 0)
    c_i = lax.broadcasted_iota(jnp.int32, (SC, SC), 1)
    same = (r_i // C) == (c_i // C)
    r_l, c_l = r_i % C, c_i % C
    mask_f = jnp.logical_and(same, c_l <= r_l)
    mask_b = jnp.logical_and(same, c_l >= r_l)
    tri_f = mask_f.astype(BF16)
    tri_b = mask_b.astype(BF16)

    def chunk_rows(b, row):
        return jnp.concatenate([jnp.broadcast_to(b[c * C + row:c * C + row + 1, :], (C, HG_DIM))
                                for c in range(SC // C)], axis=0)

    def local_part(i0, lf_ref, tri_m, mask, ref_row, last_row, o_scr, qin_scr, dec_scr, m_scr):
        rows = pl.ds(i0, SC)
        q = q_ref[0, rows, :].astype(F32)
        v = v_ref[0, rows, :]
        lf = lf_ref[0, rows, :]
        p = jnp.dot(tri_m, _split3(lf), preferred_element_type=F32)
        b = p[:, :HG_DIM] + p[:, HG_DIM:2 * HG_DIM] + p[:, 2 * HG_DIM:]
        b_ref, b_last = chunk_rows(b, ref_row), chunk_rows(b, last_row)
        k = 1.0 - jnp.exp(lf)
        a_q = (q * jnp.exp(b - b_ref)).astype(BF16)
        a_k = (k * jnp.exp(b_ref - b)).astype(BF16)
        s = lax.dot_general(a_q, a_k, NT_DIMS, preferred_element_type=F32)
        s = jnp.where(mask, s, 0.0).astype(BF16)
        o_scr[rows, :] = jnp.dot(s, v, preferred_element_type=F32)
        qin_scr[rows, :] = (q * jnp.exp(b)).astype(BF16)
        dec_scr[rows, :] = b_last
        k_out = (k * jnp.exp(b_last - b)).astype(BF16)
        for c in range(SC // C):
            m_scr[i0 // C + c] = lax.dot_general(v[c * C:(c + 1) * C], k_out[c * C:(c + 1) * C], TN_DIMS,
                                                 preferred_element_type=F32)

    def local_body(i, carry):
        i0 = pl.multiple_of(i * SC, SC)
        local_part(i0, lff_ref, tri_f, mask_f, C // 2 - 1, C - 1, of_scr, qf_scr, df_scr, mf_scr)
        local_part(i0, lfb_ref, tri_b, mask_b, C // 2, 0, ob_scr, qb_scr, db_scr, mb_scr)
        return carry

    lax.fori_loop(0, L // SC, local_body, 0)

    def carry_part(c, st, o_scr, qin_scr, dec_scr, m_scr):
        rows = pl.ds(pl.multiple_of(c * C, C), C)
        o_scr[rows, :] = o_scr[rows, :] + lax.dot_general(qin_scr[rows, :], st.astype(BF16), NT_DIMS,
                                                          preferred_element_type=F32)
        return st * jnp.exp(dec_scr[pl.ds(pl.multiple_of(c * C, C), 1), :]) + m_scr[c]

    def carry_body(c, carry):
        st_f, st_b = carry
        st_f = carry_part(c, st_f, of_scr, qf_scr, df_scr, mf_scr)
        st_b = carry_part(n - 1 - c, st_b, ob_scr, qb_scr, db_scr, mb_scr)
        return st_f, st_b

    z = jnp.zeros((HG_DIM, HG_DIM), F32)
    lax.fori_loop(0, n, carry_body, (z, z), unroll=4)

    o = of_scr[...] + ob_scr[...]
    o = o * lax.rsqrt(jnp.mean(o * o, axis=-1, keepdims=True) + LN_EPS) * ng_ref[...]
    y_ref[0] = (o * sg_ref[0].astype(F32)).astype(BF16)


def _hgrn(hq, hv, lff, lfb, hg, norm_g, batch, seq_len):
    blk = pl.BlockSpec((1, seq_len, HG_DIM), lambda b, h: (b, 0, h))
    r3 = lambda a: a.reshape(batch, seq_len, HG_WIDTH)
    per_dir = [pltpu.VMEM((seq_len, HG_DIM), F32), pltpu.VMEM((seq_len, HG_DIM), BF16),
               pltpu.VMEM((seq_len, HG_DIM), F32), pltpu.VMEM((seq_len // HG_CHUNK, HG_DIM, HG_DIM), F32)]
    scratch = [s for pair in zip(per_dir, per_dir) for s in pair]
    return pl.pallas_call(
        _hgrn_kernel,
        grid=(batch, HG_HEADS),
        in_specs=[blk, blk, blk, blk, blk, pl.BlockSpec((1, HG_DIM), lambda b, h: (0, h))],
        out_specs=blk,
        out_shape=jax.ShapeDtypeStruct((batch, seq_len, HG_WIDTH), BF16),
        scratch_shapes=scratch,
        compiler_params=pltpu.CompilerParams(dimension_semantics=("parallel", "parallel"),
                                             vmem_limit_bytes=VMEM_LIMIT),
        name="hgrn",
    )(r3(hq), r3(hv), r3(lff), r3(lfb), r3(hg), norm_g.reshape(1, HG_WIDTH))


def _attn_kernel(q_ref, k_ref, v_ref, o_ref, lse_ref, *scratch, m_len, dil, qb, kw, span):
    lane = lax.broadcasted_iota(jnp.int32, (1, LANES), 1)
    head0 = lane < ATT_HEAD_DIM
    qi_l = lax.broadcasted_iota(jnp.int32, (qb, kw), 0)
    ki_l = lax.broadcasted_iota(jnp.int32, (qb, kw), 1)

    def run_class(load_q, load_k, load_v, store_o, store_l):
        def block(i, carry):
            q0 = pl.multiple_of(i * qb, qb)
            ks = pl.multiple_of(jnp.clip(q0 - span, 0, m_len - kw), span)
            mask = jnp.abs((q0 + qi_l) - (ks + ki_l)) <= span
            for slab in range(ATT_OUT // LANES):
                cols = slice(slab * LANES, (slab + 1) * LANES)
                q2, k2, v2 = load_q(q0, qb, cols), load_k(ks, kw, cols), load_v(ks, kw, cols)
                outs, lses = [], []
                for hm in (head0, jnp.logical_not(head0)):
                    qm = jnp.where(hm, q2, jnp.zeros_like(q2))
                    s = lax.dot_general(qm, k2, NT_DIMS, preferred_element_type=F32)
                    s = jnp.where(mask, s, NEG_INF)
                    mx = jnp.max(s, axis=-1, keepdims=True)
                    e = jnp.exp(s - mx)
                    den = jnp.sum(e, axis=-1, keepdims=True)
                    outs.append(jnp.dot(e.astype(BF16), v2, preferred_element_type=F32) / den)
                    lses.append(mx + jnp.log(den))
                store_o(q0, qb, cols, jnp.where(head0, outs[0], outs[1]))
                store_l(q0, qb, cols, jnp.where(head0, lses[0], lses[1]))
            return carry

        lax.fori_loop(0, m_len // qb, block, 0)

    def loader(ref):
        return lambda r0, n, cols: ref[0, pl.ds(r0, n), cols]

    if dil == 1:
        def store_o(r0, n, cols, val):
            o_ref[0, pl.ds(r0, n), cols] = val.astype(BF16)

        def store_l(r0, n, cols, val):
            lse_ref[0, pl.ds(r0, n), cols] = val

        run_class(loader(q_ref), loader(k_ref), loader(v_ref), store_o, store_l)
        return

    q32, k32, v32, o32, l32, qc, kc, vc, oc, lc = scratch
    n_slab = ATT_OUT // LANES
    for slab in range(n_slab):
        cols = slice(slab * LANES, (slab + 1) * LANES)
        q32[slab] = q_ref[0, :, cols].astype(F32)
        k32[slab] = k_ref[0, :, cols].astype(F32)
        v32[slab] = v_ref[0, :, cols].astype(F32)

    def cls_loader(ref):
        return lambda r0, n, cols: ref[pl.ds(r0, n), cols]

    def store_oc(r0, n, cols, val):
        oc[pl.ds(r0, n), cols] = val

    def store_lc(r0, n, cols, val):
        lc[pl.ds(r0, n), cols] = val

    for r in range(dil):
        rows = pl.ds(r, m_len, stride=dil)
        for slab in range(n_slab):
            cols = slice(slab * LANES, (slab + 1) * LANES)
            qc[:, cols] = q32[slab, rows, :].astype(BF16)
            kc[:, cols] = k32[slab, rows, :].astype(BF16)
            vc[:, cols] = v32[slab, rows, :].astype(BF16)
        run_class(cls_loader(qc), cls_loader(kc), cls_loader(vc), store_oc, store_lc)
        for slab in range(n_slab):
            cols = slice(slab * LANES, (slab + 1) * LANES)
            o32[slab, rows, :] = oc[:, cols]
            l32[slab, rows, :] = lc[:, cols]
    for slab in range(n_slab):
        cols = slice(slab * LANES, (slab + 1) * LANES)
        o_ref[0, :, cols] = o32[slab].astype(BF16)
        lse_ref[0, :, cols] = l32[slab]


def _attn_group(aq, ak, av, g, batch, seq_len):
    window, dil = DIL_PAIRS[g]
    span = window // (2 * dil)
    m_len = seq_len // dil
    qb = min(2 * span, m_len)
    kw = min(qb + 2 * span, m_len)
    view = lambda a: a.reshape(batch, seq_len, ATT_WIDTH)
    in_blk = pl.BlockSpec((1, seq_len, ATT_OUT), lambda b: (b, 0, g))
    out_blk = pl.BlockSpec((1, seq_len, ATT_OUT), lambda b: (b, 0, 0))
    scratch = []
    if dil > 1:
        scratch = ([pltpu.VMEM((ATT_OUT // LANES, seq_len, LANES), F32)] * 5 + [pltpu.VMEM((m_len, ATT_OUT), BF16)] * 3
                   + [pltpu.VMEM((m_len, ATT_OUT), F32)] * 2)
    o, lse = pl.pallas_call(
        functools.partial(_attn_kernel, m_len=m_len, dil=dil, qb=qb, kw=kw, span=span),
        grid=(batch,),
        in_specs=[in_blk, in_blk, in_blk],
        out_specs=[out_blk, out_blk],
        out_shape=[jax.ShapeDtypeStruct((batch, seq_len, ATT_OUT), BF16),
                   jax.ShapeDtypeStruct((batch, seq_len, ATT_OUT), F32)],
        scratch_shapes=scratch,
        compiler_params=pltpu.CompilerParams(dimension_semantics=("parallel",), vmem_limit_bytes=VMEM_LIMIT),
        name=f"attn_g{g}",
    )(view(aq), view(ak), view(av))
    T = batch * seq_len
    return o.reshape(T, ATT_OUT), lse.reshape(T, ATT_OUT)


def _mix_kernel(x_ref, yh_ref, o0_ref, o1_ref, o2_ref, l0_ref, l1_ref, l2_ref, gh_ref, ga_ref,
                wbh_ref, wba_ref, wo_ref, g1_ref, b1_ref, wrt_ref, rb_ref,
                x1_ref, x1p_ref, tope_ref, gate_ref, rank_ref, cnt_ref, carry_scr, *, alpha):
    tm = x_ref.shape[0]

    @pl.when(pl.program_id(0) == 0)
    def _():
        carry_scr[...] = jnp.zeros_like(carry_scr)

    l0, l1, l2 = l0_ref[...], l1_ref[...], l2_ref[...]
    lm = jnp.maximum(jnp.maximum(l0, l1), l2)
    e0, e1, e2 = jnp.exp(l0 - lm), jnp.exp(l1 - lm), jnp.exp(l2 - lm)
    es = e0 + e1 + e2
    y_a = ((e0 / es) * o0_ref[...].astype(F32) + (e1 / es) * o1_ref[...].astype(F32)
           + (e2 / es) * o2_ref[...].astype(F32)).astype(BF16)
    merged = (gh_ref[...].astype(F32) * jnp.dot(yh_ref[...], wbh_ref[...], preferred_element_type=F32)
              + ga_ref[...].astype(F32) * jnp.dot(y_a, wba_ref[...], preferred_element_type=F32))
    mixed = jnp.dot(merged.astype(BF16), wo_ref[...], preferred_element_type=F32)
    x1 = _layer_norm(alpha * x_ref[...] + mixed, g1_ref[...], b1_ref[...])
    x1_ref[...] = x1
    x1p_ref[...] = _pack_halves(x1)

    logit = lax.dot_general(wrt_ref[...], x1.astype(BF16), NT_DIMS, preferred_element_type=F32)
    score = _sigmoid(logit)
    biased = score + rb_ref[...]
    b3 = biased.reshape(N_GROUPS, PER_GROUP, tm)
    i3 = lax.broadcasted_iota(jnp.int32, (N_GROUPS, PER_GROUP, tm), 1)
    m1 = jnp.max(b3, axis=1, keepdims=True)
    idx1 = jnp.min(jnp.where(b3 == m1, i3, PER_GROUP), axis=1, keepdims=True)
    m2 = jnp.max(jnp.where(i3 == idx1, -jnp.inf, b3), axis=1, keepdims=True)
    gs = (m1 + m2).reshape(N_GROUPS, tm)
    gi = lax.broadcasted_iota(jnp.int32, (N_GROUPS, tm), 0)
    sel = jnp.zeros((N_GROUPS, tm), F32)
    cur = gs
    for _ in range(TOPK_GROUPS):
        m = jnp.max(cur, axis=0, keepdims=True)
        ix = jnp.min(jnp.where(cur == m, gi, N_GROUPS), axis=0, keepdims=True)
        hit = gi == ix
        sel = jnp.where(hit, 1.0, sel)
        cur = jnp.where(hit, -jnp.inf, cur)
    cur = jnp.where(sel.reshape(N_GROUPS, 1, tm) > 0.0, b3, -jnp.inf).reshape(N_EXPERTS, tm)
    ei = lax.broadcasted_iota(jnp.int32, (N_EXPERTS, tm), 0)
    chosen = jnp.zeros((N_EXPERTS, tm), F32)
    top_e, top_s = [], []
    for _ in range(TOP_K):
        m = jnp.max(cur, axis=0, keepdims=True)
        ix = jnp.min(jnp.where(cur == m, ei, N_EXPERTS), axis=0, keepdims=True)
        hit = ei == ix
        top_e.append(ix)
        top_s.append(jnp.sum(jnp.where(hit, score, 0.0), axis=0, keepdims=True))
        chosen = jnp.where(hit, 1.0, chosen)
        cur = jnp.where(hit, -jnp.inf, cur)
    s_sum = top_s[0]
    for s in top_s[1:]:
        s_sum = s_sum + s
    tope_ref[...] = jnp.concatenate(top_e, axis=0)
    gate_ref[...] = jnp.concatenate([s / s_sum * ROUTED_SCALE for s in top_s], axis=0)

    su = lax.broadcasted_iota(jnp.int32, (tm, tm), 0)
    tu = lax.broadcasted_iota(jnp.int32, (tm, tm), 1)
    before = (su < tu).astype(BF16)
    prior = carry_scr[:, 0:1] + jnp.dot(chosen.astype(BF16), before, preferred_element_type=F32)
    ranks = [jnp.sum(jnp.where(ei == ix, prior, 0.0), axis=0, keepdims=True) for ix in top_e]
    rank_ref[...] = jnp.concatenate(ranks, axis=0).astype(jnp.int32)
    carry_scr[...] = carry_scr[...] + jnp.sum(chosen, axis=1, keepdims=True)
    cnt_ref[...] = carry_scr[...]


def _mix(x2d, y_h, att_o, att_l, gh, ga, wbh, wba, wo, g1, b1, wrt, rb, alpha):
    T = x2d.shape[0]
    tm = 256
    row = lambda w: pl.BlockSpec((tm, w), lambda i: (i, 0))
    col = pl.BlockSpec((TOP_K, tm), lambda i: (0, i))
    return pl.pallas_call(
        functools.partial(_mix_kernel, alpha=alpha),
        grid=(T // tm,),
        in_specs=[row(D_MODEL), row(HG_WIDTH)] + [row(ATT_OUT)] * 6 + [row(D_MODEL), row(D_MODEL),
                  _const_spec((HG_WIDTH, D_MODEL)), _const_spec((ATT_OUT, D_MODEL)),
                  _const_spec((D_MODEL, D_MODEL)), _const_spec((1, D_MODEL)), _const_spec((1, D_MODEL)),
                  _const_spec((N_EXPERTS, D_MODEL)), _const_spec((N_EXPERTS, 1))],
        out_specs=[row(D_MODEL), row(D_PACKED), col, col, col, _const_spec((N_EXPERTS, LANES))],
        out_shape=[jax.ShapeDtypeStruct((T, D_MODEL), F32),
                   jax.ShapeDtypeStruct((T, D_PACKED), U32),
                   jax.ShapeDtypeStruct((TOP_K, T), jnp.int32),
                   jax.ShapeDtypeStruct((TOP_K, T), F32),
                   jax.ShapeDtypeStruct((TOP_K, T), jnp.int32),
                   jax.ShapeDtypeStruct((N_EXPERTS, LANES), F32)],
        scratch_shapes=[pltpu.VMEM((N_EXPERTS, LANES), F32)],
        compiler_params=pltpu.CompilerParams(dimension_semantics=("arbitrary",), vmem_limit_bytes=VMEM_LIMIT),
        name="mix",
    )(x2d, y_h, *att_o, *att_l, gh, ga, wbh, wba, wo, g1, b1, wrt, rb)


def _dest_kernel(tope_ref, rank_ref, ps_ref, dest_ref):
    tb = tope_ref.shape[1]
    ei = lax.broadcasted_iota(jnp.int32, (N_EXPERTS, tb), 0)
    ps = ps_ref[...]
    starts = [jnp.sum(jnp.where(ei == tope_ref[k:k + 1, :], ps, 0.0), axis=0, keepdims=True)
              for k in range(TOP_K)]
    dest_ref[...] = jnp.concatenate(starts, axis=0).astype(jnp.int32) + rank_ref[...]


def _dest(top_e, rank, padded_start):
    T = top_e.shape[1]
    tb = 512
    col = pl.BlockSpec((TOP_K, tb), lambda i: (0, i))
    return pl.pallas_call(
        _dest_kernel,
        grid=(T // tb,),
        in_specs=[col, col, _const_spec((N_EXPERTS, 1))],
        out_specs=col,
        out_shape=jax.ShapeDtypeStruct((TOP_K, T), jnp.int32),
        compiler_params=pltpu.CompilerParams(dimension_semantics=("parallel",)),
        name="dest",
    )(top_e, rank, padded_start.astype(F32).reshape(N_EXPERTS, 1))


def _dispatch_kernel(dest_ref, x_ref, xs_hbm, sem):
    tb = x_ref.shape[0]

    def issue(t, c):
        for k in range(TOP_K):
            pltpu.make_async_copy(x_ref.at[pl.ds(t, 1), :], xs_hbm.at[pl.ds(dest_ref[k, t], 1), :], sem).start()
        return c

    lax.fori_loop(0, tb, issue, 0)
    for _ in range(TOP_K):
        pltpu.make_async_copy(x_ref, xs_hbm.at[pl.ds(0, tb), :], sem).wait()


def _dispatch(dest, x1, n_slots):
    T = x1.shape[0]
    tb = 512
    return pl.pallas_call(
        _dispatch_kernel,
        grid=(T // tb,),
        in_specs=[pl.BlockSpec((TOP_K, tb), lambda i: (0, i), memory_space=pltpu.SMEM),
                  pl.BlockSpec((tb, D_PACKED), lambda i: (i, 0))],
        out_specs=pl.BlockSpec(memory_space=pl.ANY),
        out_shape=jax.ShapeDtypeStruct((n_slots, D_PACKED), U32),
        scratch_shapes=[pltpu.SemaphoreType.DMA],
        compiler_params=pltpu.CompilerParams(dimension_semantics=("arbitrary",)),
        name="dispatch",
    )(dest, x1)


def _expert_kernel(be_ref, bv_ref, nu_ref, xs_ref, wg_ref, wu_ref, wd_ref, ys_ref, wg_b, wu_b, wd_b):
    i = pl.program_id(0)
    valid = bv_ref[i]

    @pl.when(valid > 0)
    def _():
        prev = be_ref[jnp.maximum(i - 1, 0)]

        @pl.when(jnp.logical_or(i == 0, be_ref[i] != prev))
        def _():
            wg_b[...] = wg_ref[0].astype(BF16)
            wu_b[...] = wu_ref[0].astype(BF16)
            wd_b[...] = wd_ref[0].astype(BF16)

        rows = lax.broadcasted_iota(jnp.int32, (SLOT_BLOCK, 1), 0)
        lo, hi = _unpack_halves(jnp.where(rows < valid, xs_ref[...], jnp.uint32(0)))
        xb = jnp.concatenate([lo.astype(BF16), hi.astype(BF16)], axis=1)
        h = _silu(jnp.dot(xb, wg_b[...], preferred_element_type=F32)) * jnp.dot(xb, wu_b[...], preferred_element_type=F32)
        ys_ref[...] = _pack_halves(jnp.dot(h.astype(BF16), wd_b[...], preferred_element_type=F32))


def _experts(block_expert, block_valid, n_used, xs, w_gate, w_up, w_down):
    n_blocks = xs.shape[0] // SLOT_BLOCK
    slot_blk = pl.BlockSpec((SLOT_BLOCK, D_PACKED), lambda i, be, bv, nu: (jnp.minimum(i, nu[0] - 1), 0))
    return pl.pallas_call(
        _expert_kernel,
        grid_spec=pltpu.PrefetchScalarGridSpec(
            num_scalar_prefetch=3,
            grid=(n_blocks,),
            in_specs=[slot_blk,
                      pl.BlockSpec((1, D_MODEL, EXPERT_HIDDEN), lambda i, be, bv, nu: (be[i], 0, 0)),
                      pl.BlockSpec((1, D_MODEL, EXPERT_HIDDEN), lambda i, be, bv, nu: (be[i], 0, 0)),
                      pl.BlockSpec((1, EXPERT_HIDDEN, D_MODEL), lambda i, be, bv, nu: (be[i], 0, 0))],
            out_specs=slot_blk,
            scratch_shapes=[pltpu.VMEM((D_MODEL, EXPERT_HIDDEN), BF16),
                            pltpu.VMEM((D_MODEL, EXPERT_HIDDEN), BF16),
                            pltpu.VMEM((EXPERT_HIDDEN, D_MODEL), BF16)],
        ),
        out_shape=jax.ShapeDtypeStruct(xs.shape, U32),
        compiler_params=pltpu.CompilerParams(dimension_semantics=("arbitrary",), vmem_limit_bytes=VMEM_LIMIT),
        name="experts",
    )(block_expert, block_valid, n_used, xs, w_gate, w_up, w_down)


def _combine_kernel(dest_ref, x1_ref, gate_ref, p_ref, ys_hbm,
                    wsg_ref, wsu_ref, wsd_ref, g2_ref, b2_ref, wpg_ref, wpp_ref, g3_ref, b3_ref,
                    out_ref, yg, sem, *, alpha):
    tm = x1_ref.shape[0]

    def issue(t, c):
        for k in range(TOP_K):
            pltpu.make_async_copy(ys_hbm.at[pl.ds(dest_ref[k, t], 1), :], yg.at[k, pl.ds(t, 1), :], sem).start()
        return c

    lax.fori_loop(0, tm, issue, 0)

    x1 = x1_ref[...]
    xb = x1.astype(BF16)
    hs = _silu(jnp.dot(xb, wsg_ref[...], preferred_element_type=F32)) * jnp.dot(xb, wsu_ref[...], preferred_element_type=F32)
    ffn = jnp.dot(hs.astype(BF16), wsd_ref[...], preferred_element_type=F32)

    for k in range(TOP_K):
        pltpu.make_async_copy(ys_hbm.at[pl.ds(0, tm), :], yg.at[k], sem).wait()
    gate = gate_ref[...]
    routed_lo = jnp.zeros((tm, D_PACKED), F32)
    routed_hi = jnp.zeros((tm, D_PACKED), F32)
    for k in range(TOP_K):
        lo, hi = _unpack_halves(yg[k])
        routed_lo = routed_lo + gate[:, k:k + 1] * lo
        routed_hi = routed_hi + gate[:, k:k + 1] * hi
    ffn = ffn + jnp.concatenate([routed_lo, routed_hi], axis=1)
    x2 = _layer_norm(alpha * x1 + ffn, g2_ref[...], b2_ref[...])
    ple = (_sigmoid(jnp.dot(x2.astype(BF16), wpg_ref[...], preferred_element_type=F32))
           * jnp.dot(p_ref[...].astype(BF16), wpp_ref[...], preferred_element_type=F32))
    out_ref[...] = _layer_norm(alpha * x2 + ple, g3_ref[...], b3_ref[...])


def _combine(dest, x1, gate_t, p2d, ys, wsg, wsu, wsd, g2, b2, wpg, wpp, g3, b3, alpha):
    T = x1.shape[0]
    tm = 256
    row = lambda w: pl.BlockSpec((tm, w), lambda i: (i, 0))
    return pl.pallas_call(
        functools.partial(_combine_kernel, alpha=alpha),
        grid=(T // tm,),
        in_specs=[pl.BlockSpec((TOP_K, tm), lambda i: (0, i), memory_space=pltpu.SMEM),
                  row(D_MODEL), row(TOP_K), row(PLE_DIM),
                  pl.BlockSpec(memory_space=pl.ANY),
                  _const_spec((D_MODEL, SHARED_HIDDEN)), _const_spec((D_MODEL, SHARED_HIDDEN)),
                  _const_spec((SHARED_HIDDEN, D_MODEL)), _const_spec((1, D_MODEL)), _const_spec((1, D_MODEL)),
                  _const_spec((D_MODEL, D_MODEL)), _const_spec((PLE_DIM, D_MODEL)),
                  _const_spec((1, D_MODEL)), _const_spec((1, D_MODEL))],
        out_specs=row(D_MODEL),
        out_shape=jax.ShapeDtypeStruct((T, D_MODEL), F32),
        scratch_shapes=[pltpu.VMEM((TOP_K, tm, D_PACKED), U32), pltpu.SemaphoreType.DMA],
        compiler_params=pltpu.CompilerParams(dimension_semantics=("arbitrary",), vmem_limit_bytes=VMEM_LIMIT),
        name="combine",
    )(dest, x1, gate_t, p2d, ys, wsg, wsu, wsd, g2, b2, wpg, wpp, g3, b3)


def _rotary_tables(seq_len):
    half = ROT_DIM // 2
    inv_freq = ROPE_THETA ** (-jnp.arange(half, dtype=F32) / half)
    ang = jnp.arange(seq_len, dtype=F32)[:, None] * inv_freq[None, :]
    cos, sin = jnp.cos(ang), jnp.sin(ang)
    pad = ATT_HEAD_DIM - ROT_DIM
    one = jnp.ones((seq_len, pad), F32)
    zero = jnp.zeros((seq_len, pad + half), F32)
    rc = jnp.concatenate([cos, cos, one], axis=1)
    rsa = jnp.concatenate([-sin, zero], axis=1)
    rsb = jnp.concatenate([jnp.zeros((seq_len, half), F32), sin, jnp.zeros((seq_len, pad), F32)], axis=1)
    rep = lambda a: jnp.tile(a, (1, LANES // ATT_HEAD_DIM))
    return rep(rc), rep(rsa), rep(rsb)


def _slot_layout(counts, n_blocks):
    padded = (counts + SLOT_BLOCK - 1) // SLOT_BLOCK * SLOT_BLOCK
    padded_end = jnp.cumsum(padded)
    padded_start = padded_end - padded
    blk0 = jnp.arange(n_blocks, dtype=jnp.int32) * SLOT_BLOCK
    block_expert = jnp.minimum(jnp.searchsorted(padded_end, blk0, side="right"), N_EXPERTS - 1).astype(jnp.int32)
    n_used = (padded_end[-1] // SLOT_BLOCK).astype(jnp.int32)
    valid = jnp.clip(counts[block_expert] - (blk0 - padded_start[block_expert]), 0, SLOT_BLOCK)
    block_valid = jnp.where(jnp.arange(n_blocks) < n_used, valid, 0).astype(jnp.int32)
    return padded_start.astype(jnp.int32), block_expert, block_valid, n_used.reshape(1)


def kernel(x, p, w_in, hg_lb_fwd, hg_lb_bwd, hg_norm_g, w_branch_hg, w_branch_att, w_out, ln1_g, ln1_b, w_router, router_bias, w_exp_gate, w_exp_up, w_exp_down, w_sh_gate, w_sh_up, w_sh_down, ln2_g, ln2_b, w_ple_gate, w_ple_proj, ln3_g, ln3_b):
    B, L, D = x.shape
    depth = w_in.shape[0]
    T = B * L
    alpha = (2 * depth) ** 0.25
    n_assign = T * TOP_K
    n_blocks = (n_assign + N_EXPERTS * (SLOT_BLOCK - 1) + SLOT_BLOCK - 1) // SLOT_BLOCK
    lb_fwd_all = jnp.cumsum(jax.nn.softmax(hg_lb_fwd.astype(F32), axis=0), axis=0)
    lb_bwd_all = jnp.cumsum(jax.nn.softmax(hg_lb_bwd.astype(F32), axis=0), axis=0)
    rc, rsa, rsb = _rotary_tables(L)
    row = lambda a: a.reshape(1, -1)

    x2d = x.reshape(T, D)
    for i in range(depth):
        (hq, hv, lff, lfb, hg, aq, ak, av, gh, ga) = _in_proj(
            x2d, w_in[i].astype(BF16), row(lb_fwd_all[i]), row(lb_bwd_all[i]), rc, rsa, rsb, L)
        y_h = _hgrn(hq, hv, lff, lfb, hg, hg_norm_g[i], B, L).reshape(T, HG_WIDTH)
        att = [_attn_group(aq, ak, av, g, B, L) for g in range(len(DIL_PAIRS))]
        x1, x1p, top_e, gate, rank, cnt = _mix(
            x2d, y_h, [a[0] for a in att], [a[1] for a in att], gh, ga,
            w_branch_hg[i].astype(BF16), w_branch_att[i].astype(BF16), w_out[i].astype(BF16),
            row(ln1_g[i]), row(ln1_b[i]), w_router[i].T.astype(BF16), router_bias[i].reshape(N_EXPERTS, 1), alpha)
        counts = cnt[:, 0].astype(jnp.int32)
        padded_start, block_expert, block_valid, n_used = _slot_layout(counts, n_blocks)
        dest = _dest(top_e, rank, padded_start)
        xs = _dispatch(dest, x1p, n_blocks * SLOT_BLOCK)
        ys = _experts(block_expert, block_valid, n_used, xs, w_exp_gate[i], w_exp_up[i], w_exp_down[i])
        x2d = _combine(dest, x1, gate.T, p[i].reshape(T, PLE_DIM), ys,
                       w_sh_gate[i].astype(BF16), w_sh_up[i].astype(BF16), w_sh_down[i].astype(BF16),
                       row(ln2_g[i]), row(ln2_b[i]), w_ple_gate[i].astype(BF16), w_ple_proj[i].astype(BF16),
                       row(ln3_g[i]), row(ln3_b[i]), alpha)
    return x2d.reshape(B, L, D)
```

```python
import functools

import jax
import jax.numpy as jnp
import numpy as np
from jax import lax
from jax.experimental import pallas as pl
from jax.experimental.pallas import tpu as pltpu

F32 = jnp.float32
BF16 = jnp.bfloat16

D_MODEL = 1024
HG_HEADS = 4
HG_DIM = 128
HG_WIDTH = HG_HEADS * HG_DIM
HG_CHUNK = 64
DIL_PAIRS = ((128, 1), (512, 4), (2048, 16))
ATT_SLOTS = 4
ATT_HEAD_DIM = 64
ATT_WIDTH = len(DIL_PAIRS) * ATT_SLOTS * ATT_HEAD_DIM
ATT_OUT = ATT_SLOTS * ATT_HEAD_DIM
ROT_DIM = ATT_HEAD_DIM // 4
ROPE_THETA = 500000.0
COL_SIZES = (HG_WIDTH,) * 5 + (ATT_WIDTH,) * 3 + (D_MODEL,) * 2
COL_STARTS = tuple(int(v) for v in np.cumsum((0,) + COL_SIZES)[:-1])
IN_COLS = sum(COL_SIZES)
N_EXPERTS = 256
TOP_K = 8
N_GROUPS = 8
TOPK_GROUPS = 4
PER_GROUP = N_EXPERTS // N_GROUPS
EXPERT_HIDDEN = 256
SHARED_HIDDEN = 256
ROUTED_SCALE = 2.5
SLOT_BLOCK = 512
PLE_DIM = 256
LN_EPS = 1e-5
NEG_INF = -1e30

LANES = 128
SUBLANES = 8
VMEM_LIMIT = 56 * 1024 * 1024

NT_DIMS = (((1,), (1,)), ((), ()))
TN_DIMS = (((0,), (0,)), ((), ()))


def _sigmoid(v):
    return jax.nn.sigmoid(v)


def _silu(v):
    return v * jax.nn.sigmoid(v)


def _layer_norm(v, g, b):
    mu = jnp.mean(v, axis=-1, keepdims=True)
    vc = v - mu
    var = jnp.mean(vc * vc, axis=-1, keepdims=True)
    return vc * lax.rsqrt(var + LN_EPS) * g + b


def _const_spec(shape):
    return pl.BlockSpec(shape, lambda *_: (0,) * len(shape))


def _in_proj_kernel(x_ref, w_ref, lbf_ref, lbb_ref, rc_ref, rsa_ref, rsb_ref,
                    hq_ref, hv_ref, lff_ref, lfb_ref, hg_ref,
                    aq_ref, ak_ref, av_ref, gh_ref, ga_ref):
    xb = x_ref[...].astype(BF16)

    def proj(seg):
        c0, width = COL_STARTS[seg], COL_SIZES[seg]
        return jnp.dot(xb, w_ref[:, c0:c0 + width], preferred_element_type=F32)

    hq_ref[...] = _silu(proj(0)).astype(BF16)
    hv_ref[...] = proj(1).astype(BF16)
    lb = lbf_ref[...]
    lff_ref[...] = jnp.log(lb + (1.0 - lb) * _sigmoid(proj(2)))
    lb = lbb_ref[...]
    lfb_ref[...] = jnp.log(lb + (1.0 - lb) * _sigmoid(proj(3)))
    hg_ref[...] = _silu(proj(4)).astype(BF16)

    rc, rsa, rsb = rc_ref[...], rsa_ref[...], rsb_ref[...]

    def rotary(z, out_ref, scale):
        for s in range(ATT_WIDTH // LANES):
            t = z[:, s * LANES:(s + 1) * LANES]
            up = pltpu.roll(t, LANES - ROT_DIM // 2, 1)
            dn = pltpu.roll(t, ROT_DIM // 2, 1)
            r = t * rc + up * rsa + dn * rsb
            out_ref[:, s * LANES:(s + 1) * LANES] = (r * scale).astype(BF16)

    rotary(proj(5), aq_ref, ATT_HEAD_DIM ** -0.5)
    rotary(proj(6), ak_ref, 1.0)
    av_ref[...] = proj(7).astype(BF16)
    gh_ref[...] = _sigmoid(proj(8)).astype(BF16)
    ga_ref[...] = _sigmoid(proj(9)).astype(BF16)


def _in_proj(x2d, w_b, lbf, lbb, rc, rsa, rsb, seq_len):
    T = x2d.shape[0]
    tm = 512
    n_pos_blocks = seq_len // tm
    row = lambda w: pl.BlockSpec((tm, w), lambda i: (i, 0))
    tab = pl.BlockSpec((tm, LANES), lambda i: (i % n_pos_blocks, 0))
    out_w = (HG_WIDTH, HG_WIDTH, HG_WIDTH, HG_WIDTH, HG_WIDTH, ATT_WIDTH, ATT_WIDTH, ATT_WIDTH, D_MODEL, D_MODEL)
    out_dt = (BF16, BF16, F32, F32, BF16, BF16, BF16, BF16, BF16, BF16)
    return pl.pallas_call(
        _in_proj_kernel,
        grid=(T // tm,),
        in_specs=[row(D_MODEL),
                  pl.BlockSpec((D_MODEL, IN_COLS), lambda i: (0, 0), pipeline_mode=pl.Buffered(1)),
                  _const_spec((1, HG_WIDTH)), _const_spec((1, HG_WIDTH)), tab, tab, tab],
        out_specs=[row(w) for w in out_w],
        out_shape=[jax.ShapeDtypeStruct((T, w), dt) for w, dt in zip(out_w, out_dt)],
        compiler_params=pltpu.CompilerParams(dimension_semantics=("parallel",), vmem_limit_bytes=VMEM_LIMIT),
        name="in_proj",
    )(x2d, w_b, lbf, lbb, rc, rsa, rsb)


def _split3(a):
    hi = a.astype(BF16)
    r1 = a - hi.astype(F32)
    mid = r1.astype(BF16)
    lo = (r1 - mid.astype(F32)).astype(BF16)
    return jnp.concatenate([hi, mid, lo], axis=1)


HG_SUPER = 4 * HG_CHUNK


def _hgrn_kernel(q_ref, v_ref, lff_ref, lfb_ref, sg_ref, ng_ref, y_ref,
                 of_scr, ob_scr, qf_scr, qb_scr, df_scr, db_scr, mf_scr, mb_scr):
    L = q_ref.shape[1]
    C, SC = HG_CHUNK, HG_SUPER
    n = L // C
    r_i = lax.broadcasted_iota(jnp.int32, (SC, SC), 0)
    c_i = lax.broadcasted_iota(jnp.int32, (SC, SC), 1)
    same = (r_i // C) == (c_i // C)
    r_l, c_l = r_i % C, c_i % C
    mask_f = jnp.logical_and(same, c_l <= r_l)
    mask_b = jnp.logical_and(same, c_l >= r_l)
    tri_f = mask_f.astype(BF16)
    tri_b = mask_b.astype(BF16)

    def chunk_rows(b, row):
        return jnp.concatenate([jnp.broadcast_to(b[c * C + row:c * C + row + 1, :], (C, HG_DIM))
                                for c in range(SC // C)], axis=0)

    def local_part(i0, lf_ref, tri_m, mask, ref_row, last_row, o_scr, qin_scr, dec_scr, m_scr):
        rows = pl.ds(i0, SC)
        q = q_ref[0, rows, :].astype(F32)
        v = v_ref[0, rows, :]
        lf = lf_ref[0, rows, :]
        p = jnp.dot(tri_m, _split3(lf), preferred_element_type=F32)
        b = p[:, :HG_DIM] + p[:, HG_DIM:2 * HG_DIM] + p[:, 2 * HG_DIM:]
        b_ref, b_last = chunk_rows(b, ref_row), chunk_rows(b, last_row)
        k = 1.0 - jnp.exp(lf)
        a_q = (q * jnp.exp(b - b_ref)).astype(BF16)
        a_k = (k * jnp.exp(b_ref - b)).astype(BF16)
        s = lax.dot_general(a_q, a_k, NT_DIMS, preferred_element_type=F32)
        s = jnp.where(mask, s, 0.0).astype(BF16)
        o_scr[rows, :] = jnp.dot(s, v, preferred_element_type=F32)
        qin_scr[rows, :] = (q * jnp.exp(b)).astype(BF16)
        dec_scr[rows, :] = b_last
        k_out = (k * jnp.exp(b_last - b)).astype(BF16)
        for c in range(SC // C):
            m_scr[i0 // C + c] = lax.dot_general(v[c * C:(c + 1) * C], k_out[c * C:(c + 1) * C], TN_DIMS,
                                                 preferred_element_type=F32)

    def local_body(i, carry):
        i0 = pl.multiple_of(i * SC, SC)
        local_part(i0, lff_ref, tri_f, mask_f, C // 2 - 1, C - 1, of_scr, qf_scr, df_scr, mf_scr)
        local_part(i0, lfb_ref, tri_b, mask_b, C // 2, 0, ob_scr, qb_scr, db_scr, mb_scr)
        return carry

    lax.fori_loop(0, L // SC, local_body, 0)

    def carry_part(c, st, o_scr, qin_scr, dec_scr, m_scr):
        rows = pl.ds(pl.multiple_of(c * C, C), C)
        o_scr[rows, :] = o_scr[rows, :] + lax.dot_general(qin_scr[rows, :], st.astype(BF16), NT_DIMS,
                                                          preferred_element_type=F32)
        return st * jnp.exp(dec_scr[pl.ds(pl.multiple_of(c * C, C), 1), :]) + m_scr[c]

    def carry_body(c, carry):
        st_f, st_b = carry
        st_f = carry_part(c, st_f, of_scr, qf_scr, df_scr, mf_scr)
        st_b = carry_part(n - 1 - c, st_b, ob_scr, qb_scr, db_scr, mb_scr)
        return st_f, st_b

    z = jnp.zeros((HG_DIM, HG_DIM), F32)
    lax.fori_loop(0, n, carry_body, (z, z), unroll=4)

    o = of_scr[...] + ob_scr[...]
    o = o * lax.rsqrt(jnp.mean(o * o, axis=-1, keepdims=True) + LN_EPS) * ng_ref[...]
    y_ref[0] = (o * sg_ref[0].astype(F32)).astype(BF16)


def _hgrn(hq, hv, lff, lfb, hg, norm_g, batch, seq_len):
    blk = pl.BlockSpec((1, seq_len, HG_DIM), lambda b, h: (b, 0, h))
    r3 = lambda a: a.reshape(batch, seq_len, HG_WIDTH)
    per_dir = [pltpu.VMEM((seq_len, HG_DIM), F32), pltpu.VMEM((seq_len, HG_DIM), BF16),
               pltpu.VMEM((seq_len, HG_DIM), F32), pltpu.VMEM((seq_len // HG_CHUNK, HG_DIM, HG_DIM), F32)]
    scratch = [s for pair in zip(per_dir, per_dir) for s in pair]
    return pl.pallas_call(
        _hgrn_kernel,
        grid=(batch, HG_HEADS),
        in_specs=[blk, blk, blk, blk, blk, pl.BlockSpec((1, HG_DIM), lambda b, h: (0, h))],
        out_specs=blk,
        out_shape=jax.ShapeDtypeStruct((batch, seq_len, HG_WIDTH), BF16),
        scratch_shapes=scratch,
        compiler_params=pltpu.CompilerParams(dimension_semantics=("parallel", "parallel"),
                                             vmem_limit_bytes=VMEM_LIMIT),
        name="hgrn",
    )(r3(hq), r3(hv), r3(lff), r3(lfb), r3(hg), norm_g.reshape(1, HG_WIDTH))


def _attn_kernel(q_ref, k_ref, v_ref, o_ref, lse_ref, *scratch, m_len, dil, qb, kw, span):
    lane = lax.broadcasted_iota(jnp.int32, (1, LANES), 1)
    head0 = lane < ATT_HEAD_DIM
    qi_l = lax.broadcasted_iota(jnp.int32, (qb, kw), 0)
    ki_l = lax.broadcasted_iota(jnp.int32, (qb, kw), 1)

    def run_class(load_q, load_k, load_v, store_o, store_l):
        def block(i, carry):
            q0 = pl.multiple_of(i * qb, qb)
            ks = pl.multiple_of(jnp.clip(q0 - span, 0, m_len - kw), span)
            mask = jnp.abs((q0 + qi_l) - (ks + ki_l)) <= span
            for slab in range(ATT_OUT // LANES):
                cols = slice(slab * LANES, (slab + 1) * LANES)
                q2, k2, v2 = load_q(q0, qb, cols), load_k(ks, kw, cols), load_v(ks, kw, cols)
                outs, lses = [], []
                for hm in (head0, jnp.logical_not(head0)):
                    qm = jnp.where(hm, q2, jnp.zeros_like(q2))
                    s = lax.dot_general(qm, k2, NT_DIMS, preferred_element_type=F32)
                    s = jnp.where(mask, s, NEG_INF)
                    mx = jnp.max(s, axis=-1, keepdims=True)
                    e = jnp.exp(s - mx)
                    den = jnp.sum(e, axis=-1, keepdims=True)
                    outs.append(jnp.dot(e.astype(BF16), v2, preferred_element_type=F32) / den)
                    lses.append(mx + jnp.log(den))
                store_o(q0, qb, cols, jnp.where(head0, outs[0], outs[1]))
                store_l(q0, qb, cols, jnp.where(head0, lses[0], lses[1]))
            return carry

        lax.fori_loop(0, m_len // qb, block, 0)

    def loader(ref):
        return lambda r0, n, cols: ref[0, pl.ds(r0, n), cols]

    if dil == 1:
        def store_o(r0, n, cols, val):
            o_ref[0, pl.ds(r0, n), cols] = val.astype(BF16)

        def store_l(r0, n, cols, val):
            lse_ref[0, pl.ds(r0, n), cols] = val

        run_class(loader(q_ref), loader(k_ref), loader(v_ref), store_o, store_l)
        return

    q32, k32, v32, o32, l32, qc, kc, vc, oc, lc = scratch
    n_slab = ATT_OUT // LANES
    for slab in range(n_slab):
        cols = slice(slab * LANES, (slab + 1) * LANES)
        q32[slab] = q_ref[0, :, cols].astype(F32)
        k32[slab] = k_ref[0, :, cols].astype(F32)
        v32[slab] = v_ref[0, :, cols].astype(F32)

    def cls_loader(ref):
        return lambda r0, n, cols: ref[pl.ds(r0, n), cols]

    def store_oc(r0, n, cols, val):
        oc[pl.ds(r0, n), cols] = val

    def store_lc(r0, n, cols, val):
        lc[pl.ds(r0, n), cols] = val

    for r in range(dil):
        rows = pl.ds(r, m_len, stride=dil)
        for slab in range(n_slab):
            cols = slice(slab * LANES, (slab + 1) * LANES)
            qc[:, cols] = q32[slab, rows, :].astype(BF16)
            kc[:, cols] = k32[slab, rows, :].astype(BF16)
            vc[:, cols] = v32[slab, rows, :].astype(BF16)
        run_class(cls_loader(qc), cls_loader(kc), cls_loader(vc), store_oc, store_lc)
        for slab in range(n_slab):
            cols = slice(slab * LANES, (slab + 1) * LANES)
            o32[slab, rows, :] = oc[:, cols]
            l32[slab, rows, :] = lc[:, cols]
    for slab in range(n_slab):
        cols = slice(slab * LANES, (slab + 1) * LANES)
        o_ref[0, :, cols] = o32[slab].astype(BF16)
        lse_ref[0, :, cols] = l32[slab]


def _attn_group(aq, ak, av, g, batch, seq_len):
    window, dil = DIL_PAIRS[g]
    span = window // (2 * dil)
    m_len = seq_len // dil
    qb = min(2 * span, m_len)
    kw = min(qb + 2 * span, m_len)
    view = lambda a: a.reshape(batch, seq_len, ATT_WIDTH)
    in_blk = pl.BlockSpec((1, seq_len, ATT_OUT), lambda b: (b, 0, g))
    out_blk = pl.BlockSpec((1, seq_len, ATT_OUT), lambda b: (b, 0, 0))
    scratch = []
    if dil > 1:
        scratch = ([pltpu.VMEM((ATT_OUT // LANES, seq_len, LANES), F32)] * 5 + [pltpu.VMEM((m_len, ATT_OUT), BF16)] * 3
                   + [pltpu.VMEM((m_len, ATT_OUT), F32)] * 2)
    o, lse = pl.pallas_call(
        functools.partial(_attn_kernel, m_len=m_len, dil=dil, qb=qb, kw=kw, span=span),
        grid=(batch,),
        in_specs=[in_blk, in_blk, in_blk],
        out_specs=[out_blk, out_blk],
        out_shape=[jax.ShapeDtypeStruct((batch, seq_len, ATT_OUT), BF16),
                   jax.ShapeDtypeStruct((batch, seq_len, ATT_OUT), F32)],
        scratch_shapes=scratch,
        compiler_params=pltpu.CompilerParams(dimension_semantics=("parallel",), vmem_limit_bytes=VMEM_LIMIT),
        name=f"attn_g{g}",
    )(view(aq), view(ak), view(av))
    T = batch * seq_len
    return o.reshape(T, ATT_OUT), lse.reshape(T, ATT_OUT)


def _mix_kernel(x_ref, yh_ref, o0_ref, o1_ref, o2_ref, l0_ref, l1_ref, l2_ref, gh_ref, ga_ref,
                wbh_ref, wba_ref, wo_ref, g1_ref, b1_ref, wrt_ref, rb_ref,
                x1_ref, tope_ref, gate_ref, rank_ref, cnt_ref, carry_scr, *, alpha):
    tm = x_ref.shape[0]

    @pl.when(pl.program_id(0) == 0)
    def _():
        carry_scr[...] = jnp.zeros_like(carry_scr)

    l0, l1, l2 = l0_ref[...], l1_ref[...], l2_ref[...]
    lm = jnp.maximum(jnp.maximum(l0, l1), l2)
    e0, e1, e2 = jnp.exp(l0 - lm), jnp.exp(l1 - lm), jnp.exp(l2 - lm)
    es = e0 + e1 + e2
    y_a = ((e0 / es) * o0_ref[...].astype(F32) + (e1 / es) * o1_ref[...].astype(F32)
           + (e2 / es) * o2_ref[...].astype(F32)).astype(BF16)
    merged = (gh_ref[...].astype(F32) * jnp.dot(yh_ref[...], wbh_ref[...], preferred_element_type=F32)
              + ga_ref[...].astype(F32) * jnp.dot(y_a, wba_ref[...], preferred_element_type=F32))
    mixed = jnp.dot(merged.astype(BF16), wo_ref[...], preferred_element_type=F32)
    x1 = _layer_norm(alpha * x_ref[...] + mixed, g1_ref[...], b1_ref[...])
    x1_ref[...] = x1

    logit = lax.dot_general(wrt_ref[...], x1.astype(BF16), NT_DIMS, preferred_element_type=F32)
    score = _sigmoid(logit)
    biased = score + rb_ref[...]
    b3 = biased.reshape(N_GROUPS, PER_GROUP, tm)
    i3 = lax.broadcasted_iota(jnp.int32, (N_GROUPS, PER_GROUP, tm), 1)
    m1 = jnp.max(b3, axis=1, keepdims=True)
    idx1 = jnp.min(jnp.where(b3 == m1, i3, PER_GROUP), axis=1, keepdims=True)
    m2 = jnp.max(jnp.where(i3 == idx1, -jnp.inf, b3), axis=1, keepdims=True)
    gs = (m1 + m2).reshape(N_GROUPS, tm)
    gi = lax.broadcasted_iota(jnp.int32, (N_GROUPS, tm), 0)
    sel = jnp.zeros((N_GROUPS, tm), F32)
    cur = gs
    for _ in range(TOPK_GROUPS):
        m = jnp.max(cur, axis=0, keepdims=True)
        ix = jnp.min(jnp.where(cur == m, gi, N_GROUPS), axis=0, keepdims=True)
        hit = gi == ix
        sel = jnp.where(hit, 1.0, sel)
        cur = jnp.where(hit, -jnp.inf, cur)
    cur = jnp.where(sel.reshape(N_GROUPS, 1, tm) > 0.0, b3, -jnp.inf).reshape(N_EXPERTS, tm)
    ei = lax.broadcasted_iota(jnp.int32, (N_EXPERTS, tm), 0)
    chosen = jnp.zeros((N_EXPERTS, tm), F32)
    top_e, top_s = [], []
    for _ in range(TOP_K):
        m = jnp.max(cur, axis=0, keepdims=True)
        ix = jnp.min(jnp.where(cur == m, ei, N_EXPERTS), axis=0, keepdims=True)
        hit = ei == ix
        top_e.append(ix)
        top_s.append(jnp.sum(jnp.where(hit, score, 0.0), axis=0, keepdims=True))
        chosen = jnp.where(hit, 1.0, chosen)
        cur = jnp.where(hit, -jnp.inf, cur)
    s_sum = top_s[0]
    for s in top_s[1:]:
        s_sum = s_sum + s
    tope_ref[...] = jnp.concatenate(top_e, axis=0)
    gate_ref[...] = jnp.concatenate([s / s_sum * ROUTED_SCALE for s in top_s], axis=0)

    su = lax.broadcasted_iota(jnp.int32, (tm, tm), 0)
    tu = lax.broadcasted_iota(jnp.int32, (tm, tm), 1)
    before = (su < tu).astype(BF16)
    prior = carry_scr[:, 0:1] + jnp.dot(chosen.astype(BF16), before, preferred_element_type=F32)
    ranks = [jnp.sum(jnp.where(ei == ix, prior, 0.0), axis=0, keepdims=True) for ix in top_e]
    rank_ref[...] = jnp.concatenate(ranks, axis=0).astype(jnp.int32)
    carry_scr[...] = carry_scr[...] + jnp.sum(chosen, axis=1, keepdims=True)
    cnt_ref[...] = carry_scr[...]


def _mix(x2d, y_h, att_o, att_l, gh, ga, wbh, wba, wo, g1, b1, wrt, rb, alpha):
    T = x2d.shape[0]
    tm = 256
    row = lambda w: pl.BlockSpec((tm, w), lambda i: (i, 0))
    col = pl.BlockSpec((TOP_K, tm), lambda i: (0, i))
    return pl.pallas_call(
        functools.partial(_mix_kernel, alpha=alpha),
        grid=(T // tm,),
        in_specs=[row(D_MODEL), row(HG_WIDTH)] + [row(ATT_OUT)] * 6 + [row(D_MODEL), row(D_MODEL),
                  _const_spec((HG_WIDTH, D_MODEL)), _const_spec((ATT_OUT, D_MODEL)),
                  _const_spec((D_MODEL, D_MODEL)), _const_spec((1, D_MODEL)), _const_spec((1, D_MODEL)),
                  _const_spec((N_EXPERTS, D_MODEL)), _const_spec((N_EXPERTS, 1))],
        out_specs=[row(D_MODEL), col, col, col, _const_spec((N_EXPERTS, LANES))],
        out_shape=[jax.ShapeDtypeStruct((T, D_MODEL), F32),
                   jax.ShapeDtypeStruct((TOP_K, T), jnp.int32),
                   jax.ShapeDtypeStruct((TOP_K, T), F32),
                   jax.ShapeDtypeStruct((TOP_K, T), jnp.int32),
                   jax.ShapeDtypeStruct((N_EXPERTS, LANES), F32)],
        scratch_shapes=[pltpu.VMEM((N_EXPERTS, LANES), F32)],
        compiler_params=pltpu.CompilerParams(dimension_semantics=("arbitrary",), vmem_limit_bytes=VMEM_LIMIT),
        name="mix",
    )(x2d, y_h, *att_o, *att_l, gh, ga, wbh, wba, wo, g1, b1, wrt, rb)


def _dest_kernel(tope_ref, rank_ref, ps_ref, dest_ref):
    tb = tope_ref.shape[1]
    ei = lax.broadcasted_iota(jnp.int32, (N_EXPERTS, tb), 0)
    ps = ps_ref[...]
    starts = [jnp.sum(jnp.where(ei == tope_ref[k:k + 1, :], ps, 0.0), axis=0, keepdims=True)
              for k in range(TOP_K)]
    dest_ref[...] = jnp.concatenate(starts, axis=0).astype(jnp.int32) + rank_ref[...]


def _dest(top_e, rank, padded_start):
    T = top_e.shape[1]
    tb = 512
    col = pl.BlockSpec((TOP_K, tb), lambda i: (0, i))
    return pl.pallas_call(
        _dest_kernel,
        grid=(T // tb,),
        in_specs=[col, col, _const_spec((N_EXPERTS, 1))],
        out_specs=col,
        out_shape=jax.ShapeDtypeStruct((TOP_K, T), jnp.int32),
        compiler_params=pltpu.CompilerParams(dimension_semantics=("parallel",)),
        name="dest",
    )(top_e, rank, padded_start.astype(F32).reshape(N_EXPERTS, 1))


def _dispatch_kernel(dest_ref, x_ref, xs_hbm, sem):
    tb = x_ref.shape[0]

    def issue(g, c):
        t0 = pl.multiple_of(g * SUBLANES, SUBLANES)
        for j in range(SUBLANES):
            for k in range(TOP_K):
                pltpu.make_async_copy(x_ref.at[pl.ds(t0 + j, 1), :], xs_hbm.at[dest_ref[k, t0 + j]], sem).start(priority=k % 2)
        return c

    lax.fori_loop(0, tb // SUBLANES, issue, 0)
    for _ in range(TOP_K):
        pltpu.make_async_copy(x_ref, xs_hbm.at[pl.ds(0, tb), 0, :], sem).wait()


def _dispatch(dest, x1, n_slots):
    T = x1.shape[0]
    tb = 512
    return pl.pallas_call(
        _dispatch_kernel,
        grid=(T // tb,),
        in_specs=[pl.BlockSpec((TOP_K, tb), lambda i: (0, i), memory_space=pltpu.SMEM),
                  pl.BlockSpec((tb, D_MODEL), lambda i: (i, 0))],
        out_specs=pl.BlockSpec(memory_space=pl.ANY),
        out_shape=jax.ShapeDtypeStruct((n_slots, 1, D_MODEL), F32),
        scratch_shapes=[pltpu.SemaphoreType.DMA],
        compiler_params=pltpu.CompilerParams(dimension_semantics=("arbitrary",)),
        name="dispatch",
    )(dest, x1)


def _expert_kernel(be_ref, bv_ref, nu_ref, xs_hbm, wg_ref, wu_ref, wd_ref, ys_hbm,
                   xbuf, ybuf, wg_b, wu_b, wd_b, in_sem, out_sem):
    i = pl.program_id(0)
    n_used = nu_ref[0]
    slot = i % 2

    def in_copy(blk, s):
        return pltpu.make_async_copy(xs_hbm.at[pl.ds(blk * SLOT_BLOCK, SLOT_BLOCK), 0, :], xbuf.at[s], in_sem.at[s])

    def out_copy(blk, s):
        return pltpu.make_async_copy(ybuf.at[s], ys_hbm.at[pl.ds(blk * SLOT_BLOCK, SLOT_BLOCK), 0, :], out_sem.at[s])

    @pl.when(i == 0)
    def _():
        in_copy(0, 0).start()

    @pl.when(i < n_used)
    def _():
        in_copy(i, slot).wait()

        @pl.when(i + 1 < n_used)
        def _():
            in_copy(i + 1, 1 - slot).start()

        prev = be_ref[jnp.maximum(i - 1, 0)]

        @pl.when(jnp.logical_or(i == 0, be_ref[i] != prev))
        def _():
            wg_b[...] = wg_ref[0].astype(BF16)
            wu_b[...] = wu_ref[0].astype(BF16)
            wd_b[...] = wd_ref[0].astype(BF16)

        rows = lax.broadcasted_iota(jnp.int32, (SLOT_BLOCK, 1), 0)
        xb = jnp.where(rows < bv_ref[i], xbuf[slot], 0.0).astype(BF16)
        h = _silu(jnp.dot(xb, wg_b[...], preferred_element_type=F32)) * jnp.dot(xb, wu_b[...], preferred_element_type=F32)
        y = jnp.dot(h.astype(BF16), wd_b[...], preferred_element_type=F32)

        @pl.when(i >= 2)
        def _():
            out_copy(i - 2, slot).wait()

        ybuf[slot] = y
        out_copy(i, slot).start()

        @pl.when(i == n_used - 1)
        def _():
            @pl.when(i >= 1)
            def _():
                out_copy(i - 1, 1 - slot).wait()

            out_copy(i, slot).wait()


def _experts(block_expert, block_valid, n_used, xs, w_gate, w_up, w_down):
    n_blocks = xs.shape[0] // SLOT_BLOCK
    return pl.pallas_call(
        _expert_kernel,
        grid_spec=pltpu.PrefetchScalarGridSpec(
            num_scalar_prefetch=3,
            grid=(n_blocks,),
            in_specs=[pl.BlockSpec(memory_space=pl.ANY),
                      pl.BlockSpec((1, D_MODEL, EXPERT_HIDDEN), lambda i, be, bv, nu: (be[i], 0, 0)),
                      pl.BlockSpec((1, D_MODEL, EXPERT_HIDDEN), lambda i, be, bv, nu: (be[i], 0, 0)),
                      pl.BlockSpec((1, EXPERT_HIDDEN, D_MODEL), lambda i, be, bv, nu: (be[i], 0, 0))],
            out_specs=pl.BlockSpec(memory_space=pl.ANY),
            scratch_shapes=[pltpu.VMEM((2, SLOT_BLOCK, D_MODEL), F32),
                            pltpu.VMEM((2, SLOT_BLOCK, D_MODEL), F32),
                            pltpu.VMEM((D_MODEL, EXPERT_HIDDEN), BF16),
                            pltpu.VMEM((D_MODEL, EXPERT_HIDDEN), BF16),
                            pltpu.VMEM((EXPERT_HIDDEN, D_MODEL), BF16),
                            pltpu.SemaphoreType.DMA((2,)), pltpu.SemaphoreType.DMA((2,))],
        ),
        out_shape=jax.ShapeDtypeStruct(xs.shape, F32),
        compiler_params=pltpu.CompilerParams(dimension_semantics=("arbitrary",), vmem_limit_bytes=VMEM_LIMIT),
        name="experts",
    )(block_expert, block_valid, n_used, xs, w_gate, w_up, w_down)


def _combine_kernel(dest_ref, x1_ref, gate_ref, p_ref, ys_hbm,
                    wsg_ref, wsu_ref, wsd_ref, g2_ref, b2_ref, wpg_ref, wpp_ref, g3_ref, b3_ref,
                    out_ref, yg, sem, *, alpha):
    tm = x1_ref.shape[0]

    def issue(g, c):
        t0 = pl.multiple_of(g * SUBLANES, SUBLANES)
        for j in range(SUBLANES):
            for k in range(TOP_K):
                pltpu.make_async_copy(ys_hbm.at[dest_ref[k, t0 + j]], yg.at[k, pl.ds(t0 + j, 1), :], sem).start(priority=k % 2)
        return c

    lax.fori_loop(0, tm // SUBLANES, issue, 0)

    x1 = x1_ref[...]
    xb = x1.astype(BF16)
    hs = _silu(jnp.dot(xb, wsg_ref[...], preferred_element_type=F32)) * jnp.dot(xb, wsu_ref[...], preferred_element_type=F32)
    ffn = jnp.dot(hs.astype(BF16), wsd_ref[...], preferred_element_type=F32)

    for k in range(TOP_K):
        pltpu.make_async_copy(ys_hbm.at[pl.ds(0, tm), 0, :], yg.at[k], sem).wait()
    gate = gate_ref[...]
    for k in range(TOP_K):
        ffn = ffn + gate[:, k:k + 1] * yg[k]
    x2 = _layer_norm(alpha * x1 + ffn, g2_ref[...], b2_ref[...])
    ple = (_sigmoid(jnp.dot(x2.astype(BF16), wpg_ref[...], preferred_element_type=F32))
           * jnp.dot(p_ref[...].astype(BF16), wpp_ref[...], preferred_element_type=F32))
    out_ref[...] = _layer_norm(alpha * x2 + ple, g3_ref[...], b3_ref[...])


def _combine(dest, x1, gate_t, p2d, ys, wsg, wsu, wsd, g2, b2, wpg, wpp, g3, b3, alpha):
    T = x1.shape[0]
    tm = 256
    row = lambda w: pl.BlockSpec((tm, w), lambda i: (i, 0))
    return pl.pallas_call(
        functools.partial(_combine_kernel, alpha=alpha),
        grid=(T // tm,),
        in_specs=[pl.BlockSpec((TOP_K, tm), lambda i: (0, i), memory_space=pltpu.SMEM),
                  row(D_MODEL), row(TOP_K), row(PLE_DIM),
                  pl.BlockSpec(memory_space=pl.ANY),
                  _const_spec((D_MODEL, SHARED_HIDDEN)), _const_spec((D_MODEL, SHARED_HIDDEN)),
                  _const_spec((SHARED_HIDDEN, D_MODEL)), _const_spec((1, D_MODEL)), _const_spec((1, D_MODEL)),
                  _const_spec((D_MODEL, D_MODEL)), _const_spec((PLE_DIM, D_MODEL)),
                  _const_spec((1, D_MODEL)), _const_spec((1, D_MODEL))],
        out_specs=row(D_MODEL),
        out_shape=jax.ShapeDtypeStruct((T, D_MODEL), F32),
        scratch_shapes=[pltpu.VMEM((TOP_K, tm, D_MODEL), F32), pltpu.SemaphoreType.DMA],
        compiler_params=pltpu.CompilerParams(dimension_semantics=("arbitrary",), vmem_limit_bytes=VMEM_LIMIT),
        name="combine",
    )(dest, x1, gate_t, p2d, ys, wsg, wsu, wsd, g2, b2, wpg, wpp, g3, b3)


def _rotary_tables(seq_len):
    half = ROT_DIM // 2
    inv_freq = ROPE_THETA ** (-jnp.arange(half, dtype=F32) / half)
    ang = jnp.arange(seq_len, dtype=F32)[:, None] * inv_freq[None, :]
    cos, sin = jnp.cos(ang), jnp.sin(ang)
    pad = ATT_HEAD_DIM - ROT_DIM
    one = jnp.ones((seq_len, pad), F32)
    zero = jnp.zeros((seq_len, pad + half), F32)
    rc = jnp.concatenate([cos, cos, one], axis=1)
    rsa = jnp.concatenate([-sin, zero], axis=1)
    rsb = jnp.concatenate([jnp.zeros((seq_len, half), F32), sin, jnp.zeros((seq_len, pad), F32)], axis=1)
    rep = lambda a: jnp.tile(a, (1, LANES // ATT_HEAD_DIM))
    return rep(rc), rep(rsa), rep(rsb)


def _slot_layout(counts, n_blocks):
    padded = (counts + SLOT_BLOCK - 1) // SLOT_BLOCK * SLOT_BLOCK
    padded_end = jnp.cumsum(padded)
    padded_start = padded_end - padded
    blk0 = jnp.arange(n_blocks, dtype=jnp.int32) * SLOT_BLOCK
    block_expert = jnp.minimum(jnp.sum(blk0[:, None] >= padded_end[None, :], axis=1), N_EXPERTS - 1).astype(jnp.int32)
    n_used = (padded_end[-1] // SLOT_BLOCK).astype(jnp.int32)
    own = block_expert[:, None] == jnp.arange(N_EXPERTS, dtype=jnp.int32)[None, :]
    pick = lambda tab: jnp.sum(jnp.where(own, tab[None, :], 0), axis=1)
    valid = jnp.clip(pick(counts) - (blk0 - pick(padded_start)), 0, SLOT_BLOCK)
    block_valid = jnp.where(jnp.arange(n_blocks) < n_used, valid, 0).astype(jnp.int32)
    return padded_start.astype(jnp.int32), block_expert, block_valid, n_used.reshape(1)


def kernel(x, p, w_in, hg_lb_fwd, hg_lb_bwd, hg_norm_g, w_branch_hg, w_branch_att, w_out, ln1_g, ln1_b, w_router, router_bias, w_exp_gate, w_exp_up, w_exp_down, w_sh_gate, w_sh_up, w_sh_down, ln2_g, ln2_b, w_ple_gate, w_ple_proj, ln3_g, ln3_b):
    B, L, D = x.shape
    depth = w_in.shape[0]
    T = B * L
    alpha = (2 * depth) ** 0.25
    n_assign = T * TOP_K
    n_blocks = (n_assign + N_EXPERTS * (SLOT_BLOCK - 1) + SLOT_BLOCK - 1) // SLOT_BLOCK
    lb_fwd_all = jnp.cumsum(jax.nn.softmax(hg_lb_fwd.astype(F32), axis=0), axis=0)
    lb_bwd_all = jnp.cumsum(jax.nn.softmax(hg_lb_bwd.astype(F32), axis=0), axis=0)
    rc, rsa, rsb = _rotary_tables(L)
    row = lambda a: a.reshape(1, -1)

    x2d = x.reshape(T, D)
    for i in range(depth):
        (hq, hv, lff, lfb, hg, aq, ak, av, gh, ga) = _in_proj(
            x2d, w_in[i].astype(BF16), row(lb_fwd_all[i]), row(lb_bwd_all[i]), rc, rsa, rsb, L)
        y_h = _hgrn(hq, hv, lff, lfb, hg, hg_norm_g[i], B, L).reshape(T, HG_WIDTH)
        att = [_attn_group(aq, ak, av, g, B, L) for g in range(len(DIL_PAIRS))]
        x1, top_e, gate, rank, cnt = _mix(
            x2d, y_h, [a[0] for a in att], [a[1] for a in att], gh, ga,
            w_branch_hg[i].astype(BF16), w_branch_att[i].astype(BF16), w_out[i].astype(BF16),
            row(ln1_g[i]), row(ln1_b[i]), w_router[i].T.astype(BF16), router_bias[i].reshape(N_EXPERTS, 1), alpha)
        counts = cnt[:, 0].astype(jnp.int32)
        padded_start, block_expert, block_valid, n_used = _slot_layout(counts, n_blocks)
        dest = _dest(top_e, rank, padded_start)
        xs = _dispatch(dest, x1, n_blocks * SLOT_BLOCK)
        ys = _experts(block_expert, block_valid, n_used, xs, w_exp_gate[i], w_exp_up[i], w_exp_down[i])
        x2d = _combine(dest, x1, gate.T, p[i].reshape(T, PLE_DIM), ys,
                       w_sh_gate[i].astype(BF16), w_sh_up[i].astype(BF16), w_sh_down[i].astype(BF16),
                       row(ln2_g[i]), row(ln2_b[i]), w_ple_gate[i].astype(BF16), w_ple_proj[i].astype(BF16),
                       row(ln3_g[i]), row(ln3_b[i]), alpha)
    return x2d.reshape(B, L, D)
```

```python
import functools

import jax
import jax.numpy as jnp
import numpy as np
from jax import lax
from jax.experimental import pallas as pl
from jax.experimental.pallas import tpu as pltpu

F32 = jnp.float32
BF16 = jnp.bfloat16

D_MODEL = 1024
HG_HEADS = 4
HG_DIM = 128
HG_WIDTH = HG_HEADS * HG_DIM
HG_CHUNK = 64
DIL_PAIRS = ((128, 1), (512, 4), (2048, 16))
ATT_SLOTS = 4
ATT_HEAD_DIM = 64
ATT_WIDTH = len(DIL_PAIRS) * ATT_SLOTS * ATT_HEAD_DIM
ATT_OUT = ATT_SLOTS * ATT_HEAD_DIM
ROT_DIM = ATT_HEAD_DIM // 4
ROPE_THETA = 500000.0
COL_SIZES = (HG_WIDTH,) * 5 + (ATT_WIDTH,) * 3 + (D_MODEL,) * 2
COL_STARTS = tuple(int(v) for v in np.cumsum((0,) + COL_SIZES)[:-1])
IN_COLS = sum(COL_SIZES)
N_EXPERTS = 256
TOP_K = 8
N_GROUPS = 8
TOPK_GROUPS = 4
PER_GROUP = N_EXPERTS // N_GROUPS
EXPERT_HIDDEN = 256
SHARED_HIDDEN = 256
ROUTED_SCALE = 2.5
SLOT_BLOCK = 512
PLE_DIM = 256
LN_EPS = 1e-5
NEG_INF = -1e30

LANES = 128
SUBLANES = 8
VMEM_LIMIT = 56 * 1024 * 1024

NT_DIMS = (((1,), (1,)), ((), ()))
TN_DIMS = (((0,), (0,)), ((), ()))


def _sigmoid(v):
    return jax.nn.sigmoid(v)


def _silu(v):
    return v * jax.nn.sigmoid(v)


def _layer_norm(v, g, b):
    mu = jnp.mean(v, axis=-1, keepdims=True)
    vc = v - mu
    var = jnp.mean(vc * vc, axis=-1, keepdims=True)
    return vc * lax.rsqrt(var + LN_EPS) * g + b


def _const_spec(shape):
    return pl.BlockSpec(shape, lambda *_: (0,) * len(shape))


def _in_proj_kernel(x_ref, w_ref, lbf_ref, lbb_ref, rc_ref, rsa_ref, rsb_ref,
                    hq_ref, hv_ref, lff_ref, lfb_ref, hg_ref,
                    aq_ref, ak_ref, av_ref, gh_ref, ga_ref):
    xb = x_ref[...].astype(BF16)

    def proj(seg):
        c0, width = COL_STARTS[seg], COL_SIZES[seg]
        return jnp.dot(xb, w_ref[:, c0:c0 + width], preferred_element_type=F32)

    hq_ref[...] = _silu(proj(0)).astype(BF16)
    hv_ref[...] = proj(1).astype(BF16)
    lb = lbf_ref[...]
    lff_ref[...] = jnp.log(lb + (1.0 - lb) * _sigmoid(proj(2)))
    lb = lbb_ref[...]
    lfb_ref[...] = jnp.log(lb + (1.0 - lb) * _sigmoid(proj(3)))
    hg_ref[...] = _silu(proj(4)).astype(BF16)

    rc, rsa, rsb = rc_ref[...], rsa_ref[...], rsb_ref[...]

    def rotary(z, out_ref, scale):
        for s in range(ATT_WIDTH // LANES):
            t = z[:, s * LANES:(s + 1) * LANES]
            up = pltpu.roll(t, LANES - ROT_DIM // 2, 1)
            dn = pltpu.roll(t, ROT_DIM // 2, 1)
            r = t * rc + up * rsa + dn * rsb
            out_ref[:, s * LANES:(s + 1) * LANES] = (r * scale).astype(BF16)

    rotary(proj(5), aq_ref, ATT_HEAD_DIM ** -0.5)
    rotary(proj(6), ak_ref, 1.0)
    av_ref[...] = proj(7).astype(BF16)
    gh_ref[...] = _sigmoid(proj(8)).astype(BF16)
    ga_ref[...] = _sigmoid(proj(9)).astype(BF16)


def _in_proj(x2d, w_b, lbf, lbb, rc, rsa, rsb, seq_len):
    T = x2d.shape[0]
    tm = 512
    n_pos_blocks = seq_len // tm
    row = lambda w: pl.BlockSpec((tm, w), lambda i: (i, 0))
    tab = pl.BlockSpec((tm, LANES), lambda i: (i % n_pos_blocks, 0))
    out_w = (HG_WIDTH, HG_WIDTH, HG_WIDTH, HG_WIDTH, HG_WIDTH, ATT_WIDTH, ATT_WIDTH, ATT_WIDTH, D_MODEL, D_MODEL)
    out_dt = (BF16, BF16, F32, F32, BF16, BF16, BF16, BF16, BF16, BF16)
    return pl.pallas_call(
        _in_proj_kernel,
        grid=(T // tm,),
        in_specs=[row(D_MODEL),
                  pl.BlockSpec((D_MODEL, IN_COLS), lambda i: (0, 0), pipeline_mode=pl.Buffered(1)),
                  _const_spec((1, HG_WIDTH)), _const_spec((1, HG_WIDTH)), tab, tab, tab],
        out_specs=[row(w) for w in out_w],
        out_shape=[jax.ShapeDtypeStruct((T, w), dt) for w, dt in zip(out_w, out_dt)],
        compiler_params=pltpu.CompilerParams(dimension_semantics=("parallel",), vmem_limit_bytes=VMEM_LIMIT),
        name="in_proj",
    )(x2d, w_b, lbf, lbb, rc, rsa, rsb)


def _split3(a):
    hi = a.astype(BF16)
    r1 = a - hi.astype(F32)
    mid = r1.astype(BF16)
    lo = (r1 - mid.astype(F32)).astype(BF16)
    return jnp.concatenate([hi, mid, lo], axis=1)


HG_SUPER = 4 * HG_CHUNK


def _hgrn_kernel(q_ref, v_ref, lff_ref, lfb_ref, sg_ref, ng_ref, y_ref,
                 of_scr, ob_scr, qf_scr, qb_scr, df_scr, db_scr, mf_scr, mb_scr):
    L = q_ref.shape[1]
    C, SC = HG_CHUNK, HG_SUPER
    n = L // C
    r_i = lax.broadcasted_iota(jnp.int32, (SC, SC), 0)
    c_i = lax.broadcasted_iota(jnp.int32, (SC, SC), 1)
    same = (r_i // C) == (c_i // C)
    r_l, c_l = r_i % C, c_i % C
    mask_f = jnp.logical_and(same, c_l <= r_l)
    mask_b = jnp.logical_and(same, c_l >= r_l)
    tri_f = mask_f.astype(BF16)
    tri_b = mask_b.astype(BF16)

    def chunk_rows(b, row):
        return jnp.concatenate([jnp.broadcast_to(b[c * C + row:c * C + row + 1, :], (C, HG_DIM))
                                for c in range(SC // C)], axis=0)

    def local_part(i0, lf_ref, tri_m, mask, ref_row, last_row, o_scr, qin_scr, dec_scr, m_scr):
        rows = pl.ds(i0, SC)
        q = q_ref[0, rows, :].astype(F32)
        v = v_ref[0, rows, :]
        lf = lf_ref[0, rows, :]
        p = jnp.dot(tri_m, _split3(lf), preferred_element_type=F32)
        b = p[:, :HG_DIM] + p[:, HG_DIM:2 * HG_DIM] + p[:, 2 * HG_DIM:]
        b_ref, b_last = chunk_rows(b, ref_row), chunk_rows(b, last_row)
        k = 1.0 - jnp.exp(lf)
        a_q = (q * jnp.exp(b - b_ref)).astype(BF16)
        a_k = (k * jnp.exp(b_ref - b)).astype(BF16)
        s = lax.dot_general(a_q, a_k, NT_DIMS, preferred_element_type=F32)
        s = jnp.where(mask, s, 0.0).astype(BF16)
        o_scr[rows, :] = jnp.dot(s, v, preferred_element_type=F32)
        qin_scr[rows, :] = (q * jnp.exp(b)).astype(BF16)
        dec_scr[rows, :] = b_last
        k_out = (k * jnp.exp(b_last - b)).astype(BF16)
        for c in range(SC // C):
            m_scr[i0 // C + c] = lax.dot_general(v[c * C:(c + 1) * C], k_out[c * C:(c + 1) * C], TN_DIMS,
                                                 preferred_element_type=F32)

    def local_body(i, carry):
        i0 = pl.multiple_of(i * SC, SC)
        local_part(i0, lff_ref, tri_f, mask_f, C // 2 - 1, C - 1, of_scr, qf_scr, df_scr, mf_scr)
        local_part(i0, lfb_ref, tri_b, mask_b, C // 2, 0, ob_scr, qb_scr, db_scr, mb_scr)
        return carry

    lax.fori_loop(0, L // SC, local_body, 0)

    def carry_part(c, st, o_scr, qin_scr, dec_scr, m_scr):
        rows = pl.ds(pl.multiple_of(c * C, C), C)
        o_scr[rows, :] = o_scr[rows, :] + lax.dot_general(qin_scr[rows, :], st.astype(BF16), NT_DIMS,
                                                          preferred_element_type=F32)
        return st * jnp.exp(dec_scr[pl.ds(pl.multiple_of(c * C, C), 1), :]) + m_scr[c]

    def carry_body(c, carry):
        st_f, st_b = carry
        st_f = carry_part(c, st_f, of_scr, qf_scr, df_scr, mf_scr)
        st_b = carry_part(n - 1 - c, st_b, ob_scr, qb_scr, db_scr, mb_scr)
        return st_f, st_b

    z = jnp.zeros((HG_DIM, HG_DIM), F32)
    lax.fori_loop(0, n, carry_body, (z, z), unroll=4)

    o = of_scr[...] + ob_scr[...]
    o = o * lax.rsqrt(jnp.mean(o * o, axis=-1, keepdims=True) + LN_EPS) * ng_ref[...]
    y_ref[0] = (o * sg_ref[0].astype(F32)).astype(BF16)


def _hgrn(hq, hv, lff, lfb, hg, norm_g, batch, seq_len):
    blk = pl.BlockSpec((1, seq_len, HG_DIM), lambda b, h: (b, 0, h))
    r3 = lambda a: a.reshape(batch, seq_len, HG_WIDTH)
    per_dir = [pltpu.VMEM((seq_len, HG_DIM), F32), pltpu.VMEM((seq_len, HG_DIM), BF16),
               pltpu.VMEM((seq_len, HG_DIM), F32), pltpu.VMEM((seq_len // HG_CHUNK, HG_DIM, HG_DIM), F32)]
    scratch = [s for pair in zip(per_dir, per_dir) for s in pair]
    return pl.pallas_call(
        _hgrn_kernel,
        grid=(batch, HG_HEADS),
        in_specs=[blk, blk, blk, blk, blk, pl.BlockSpec((1, HG_DIM), lambda b, h: (0, h))],
        out_specs=blk,
        out_shape=jax.ShapeDtypeStruct((batch, seq_len, HG_WIDTH), BF16),
        scratch_shapes=scratch,
        compiler_params=pltpu.CompilerParams(dimension_semantics=("parallel", "parallel"),
                                             vmem_limit_bytes=VMEM_LIMIT),
        name="hgrn",
    )(r3(hq), r3(hv), r3(lff), r3(lfb), r3(hg), norm_g.reshape(1, HG_WIDTH))


def _attn_kernel(q_ref, k_ref, v_ref, o_ref, lse_ref, *scratch, m_len, dil, qb, kw, span):
    lane = lax.broadcasted_iota(jnp.int32, (1, LANES), 1)
    head0 = lane < ATT_HEAD_DIM
    qi_l = lax.broadcasted_iota(jnp.int32, (qb, kw), 0)
    ki_l = lax.broadcasted_iota(jnp.int32, (qb, kw), 1)

    def run_class(load_q, load_k, load_v, store_o, store_l):
        def block(i, carry):
            q0 = pl.multiple_of(i * qb, qb)
            ks = pl.multiple_of(jnp.clip(q0 - span, 0, m_len - kw), span)
            mask = jnp.abs((q0 + qi_l) - (ks + ki_l)) <= span
            for slab in range(ATT_OUT // LANES):
                cols = slice(slab * LANES, (slab + 1) * LANES)
                q2, k2, v2 = load_q(q0, qb, cols), load_k(ks, kw, cols), load_v(ks, kw, cols)
                outs, lses = [], []
                for hm in (head0, jnp.logical_not(head0)):
                    qm = jnp.where(hm, q2, jnp.zeros_like(q2))
                    s = lax.dot_general(qm, k2, NT_DIMS, preferred_element_type=F32)
                    s = jnp.where(mask, s, NEG_INF)
                    mx = jnp.max(s, axis=-1, keepdims=True)
                    e = jnp.exp(s - mx)
                    den = jnp.sum(e, axis=-1, keepdims=True)
                    outs.append(jnp.dot(e.astype(BF16), v2, preferred_element_type=F32) / den)
                    lses.append(mx + jnp.log(den))
                store_o(q0, qb, cols, jnp.where(head0, outs[0], outs[1]))
                store_l(q0, qb, cols, jnp.where(head0, lses[0], lses[1]))
            return carry

        lax.fori_loop(0, m_len // qb, block, 0)

    def loader(ref):
        return lambda r0, n, cols: ref[0, pl.ds(r0, n), cols]

    if dil == 1:
        def store_o(r0, n, cols, val):
            o_ref[0, pl.ds(r0, n), cols] = val.astype(BF16)

        def store_l(r0, n, cols, val):
            lse_ref[0, pl.ds(r0, n), cols] = val

        run_class(loader(q_ref), loader(k_ref), loader(v_ref), store_o, store_l)
        return

    q32, k32, v32, o32, l32, qc, kc, vc, oc, lc = scratch
    n_slab = ATT_OUT // LANES
    for slab in range(n_slab):
        cols = slice(slab * LANES, (slab + 1) * LANES)
        q32[slab] = q_ref[0, :, cols].astype(F32)
        k32[slab] = k_ref[0, :, cols].astype(F32)
        v32[slab] = v_ref[0, :, cols].astype(F32)

    def cls_loader(ref):
        return lambda r0, n, cols: ref[pl.ds(r0, n), cols]

    def store_oc(r0, n, cols, val):
        oc[pl.ds(r0, n), cols] = val

    def store_lc(r0, n, cols, val):
        lc[pl.ds(r0, n), cols] = val

    for r in range(dil):
        rows = pl.ds(r, m_len, stride=dil)
        for slab in range(n_slab):
            cols = slice(slab * LANES, (slab + 1) * LANES)
            qc[:, cols] = q32[slab, rows, :].astype(BF16)
            kc[:, cols] = k32[slab, rows, :].astype(BF16)
            vc[:, cols] = v32[slab, rows, :].astype(BF16)
        run_class(cls_loader(qc), cls_loader(kc), cls_loader(vc), store_oc, store_lc)
        for slab in range(n_slab):
            cols = slice(slab * LANES, (slab + 1) * LANES)
            o32[slab, rows, :] = oc[:, cols]
            l32[slab, rows, :] = lc[:, cols]
    for slab in range(n_slab):
        cols = slice(slab * LANES, (slab + 1) * LANES)
        o_ref[0, :, cols] = o32[slab].astype(BF16)
        lse_ref[0, :, cols] = l32[slab]


def _attn_group(aq, ak, av, g, batch, seq_len):
    window, dil = DIL_PAIRS[g]
    span = window // (2 * dil)
    m_len = seq_len // dil
    qb = min(2 * span, m_len)
    kw = min(qb + 2 * span, m_len)
    view = lambda a: a.reshape(batch, seq_len, ATT_WIDTH)
    in_blk = pl.BlockSpec((1, seq_len, ATT_OUT), lambda b: (b, 0, g))
    out_blk = pl.BlockSpec((1, seq_len, ATT_OUT), lambda b: (b, 0, 0))
    scratch = []
    if dil > 1:
        scratch = ([pltpu.VMEM((ATT_OUT // LANES, seq_len, LANES), F32)] * 5 + [pltpu.VMEM((m_len, ATT_OUT), BF16)] * 3
                   + [pltpu.VMEM((m_len, ATT_OUT), F32)] * 2)
    o, lse = pl.pallas_call(
        functools.partial(_attn_kernel, m_len=m_len, dil=dil, qb=qb, kw=kw, span=span),
        grid=(batch,),
        in_specs=[in_blk, in_blk, in_blk],
        out_specs=[out_blk, out_blk],
        out_shape=[jax.ShapeDtypeStruct((batch, seq_len, ATT_OUT), BF16),
                   jax.ShapeDtypeStruct((batch, seq_len, ATT_OUT), F32)],
        scratch_shapes=scratch,
        compiler_params=pltpu.CompilerParams(dimension_semantics=("parallel",), vmem_limit_bytes=VMEM_LIMIT),
        name=f"attn_g{g}",
    )(view(aq), view(ak), view(av))
    T = batch * seq_len
    return o.reshape(T, ATT_OUT), lse.reshape(T, ATT_OUT)


def _mix_kernel(x_ref, yh_ref, o0_ref, o1_ref, o2_ref, l0_ref, l1_ref, l2_ref, gh_ref, ga_ref,
                wbh_ref, wba_ref, wo_ref, g1_ref, b1_ref, wrt_ref, rb_ref,
                x1_ref, tope_ref, gate_ref, rank_ref, cnt_ref, carry_scr, *, alpha):
    tm = x_ref.shape[0]

    @pl.when(pl.program_id(0) == 0)
    def _():
        carry_scr[...] = jnp.zeros_like(carry_scr)

    l0, l1, l2 = l0_ref[...], l1_ref[...], l2_ref[...]
    lm = jnp.maximum(jnp.maximum(l0, l1), l2)
    e0, e1, e2 = jnp.exp(l0 - lm), jnp.exp(l1 - lm), jnp.exp(l2 - lm)
    es = e0 + e1 + e2
    y_a = ((e0 / es) * o0_ref[...].astype(F32) + (e1 / es) * o1_ref[...].astype(F32)
           + (e2 / es) * o2_ref[...].astype(F32)).astype(BF16)
    merged = (gh_ref[...].astype(F32) * jnp.dot(yh_ref[...], wbh_ref[...], preferred_element_type=F32)
              + ga_ref[...].astype(F32) * jnp.dot(y_a, wba_ref[...], preferred_element_type=F32))
    mixed = jnp.dot(merged.astype(BF16), wo_ref[...], preferred_element_type=F32)
    x1 = _layer_norm(alpha * x_ref[...] + mixed, g1_ref[...], b1_ref[...])
    x1_ref[...] = x1

    logit = lax.dot_general(wrt_ref[...], x1.astype(BF16), NT_DIMS, preferred_element_type=F32)
    score = _sigmoid(logit)
    biased = score + rb_ref[...]
    b3 = biased.reshape(N_GROUPS, PER_GROUP, tm)
    i3 = lax.broadcasted_iota(jnp.int32, (N_GROUPS, PER_GROUP, tm), 1)
    m1 = jnp.max(b3, axis=1, keepdims=True)
    idx1 = jnp.min(jnp.where(b3 == m1, i3, PER_GROUP), axis=1, keepdims=True)
    m2 = jnp.max(jnp.where(i3 == idx1, -jnp.inf, b3), axis=1, keepdims=True)
    gs = (m1 + m2).reshape(N_GROUPS, tm)
    gi = lax.broadcasted_iota(jnp.int32, (N_GROUPS, tm), 0)
    sel = jnp.zeros((N_GROUPS, tm), F32)
    cur = gs
    for _ in range(TOPK_GROUPS):
        m = jnp.max(cur, axis=0, keepdims=True)
        ix = jnp.min(jnp.where(cur == m, gi, N_GROUPS), axis=0, keepdims=True)
        hit = gi == ix
        sel = jnp.where(hit, 1.0, sel)
        cur = jnp.where(hit, -jnp.inf, cur)
    cur = jnp.where(sel.reshape(N_GROUPS, 1, tm) > 0.0, b3, -jnp.inf).reshape(N_EXPERTS, tm)
    ei = lax.broadcasted_iota(jnp.int32, (N_EXPERTS, tm), 0)
    chosen = jnp.zeros((N_EXPERTS, tm), F32)
    top_e, top_s = [], []
    for _ in range(TOP_K):
        m = jnp.max(cur, axis=0, keepdims=True)
        ix = jnp.min(jnp.where(cur == m, ei, N_EXPERTS), axis=0, keepdims=True)
        hit = ei == ix
        top_e.append(ix)
        top_s.append(jnp.sum(jnp.where(hit, score, 0.0), axis=0, keepdims=True))
        chosen = jnp.where(hit, 1.0, chosen)
        cur = jnp.where(hit, -jnp.inf, cur)
    s_sum = top_s[0]
    for s in top_s[1:]:
        s_sum = s_sum + s
    tope_ref[...] = jnp.concatenate(top_e, axis=0)
    gate_ref[...] = jnp.concatenate([s / s_sum * ROUTED_SCALE for s in top_s], axis=0)

    su = lax.broadcasted_iota(jnp.int32, (tm, tm), 0)
    tu = lax.broadcasted_iota(jnp.int32, (tm, tm), 1)
    before = (su < tu).astype(BF16)
    prior = carry_scr[:, 0:1] + jnp.dot(chosen.astype(BF16), before, preferred_element_type=F32)
    ranks = [jnp.sum(jnp.where(ei == ix, prior, 0.0), axis=0, keepdims=True) for ix in top_e]
    rank_ref[...] = jnp.concatenate(ranks, axis=0).astype(jnp.int32)
    carry_scr[...] = carry_scr[...] + jnp.sum(chosen, axis=1, keepdims=True)
    cnt_ref[...] = carry_scr[...]


def _mix(x2d, y_h, att_o, att_l, gh, ga, wbh, wba, wo, g1, b1, wrt, rb, alpha):
    T = x2d.shape[0]
    tm = 256
    row = lambda w: pl.BlockSpec((tm, w), lambda i: (i, 0))
    col = pl.BlockSpec((TOP_K, tm), lambda i: (0, i))
    return pl.pallas_call(
        functools.partial(_mix_kernel, alpha=alpha),
        grid=(T // tm,),
        in_specs=[row(D_MODEL), row(HG_WIDTH)] + [row(ATT_OUT)] * 6 + [row(D_MODEL), row(D_MODEL),
                  _const_spec((HG_WIDTH, D_MODEL)), _const_spec((ATT_OUT, D_MODEL)),
                  _const_spec((D_MODEL, D_MODEL)), _const_spec((1, D_MODEL)), _const_spec((1, D_MODEL)),
                  _const_spec((N_EXPERTS, D_MODEL)), _const_spec((N_EXPERTS, 1))],
        out_specs=[row(D_MODEL), col, col, col, _const_spec((N_EXPERTS, LANES))],
        out_shape=[jax.ShapeDtypeStruct((T, D_MODEL), F32),
                   jax.ShapeDtypeStruct((TOP_K, T), jnp.int32),
                   jax.ShapeDtypeStruct((TOP_K, T), F32),
                   jax.ShapeDtypeStruct((TOP_K, T), jnp.int32),
                   jax.ShapeDtypeStruct((N_EXPERTS, LANES), F32)],
        scratch_shapes=[pltpu.VMEM((N_EXPERTS, LANES), F32)],
        compiler_params=pltpu.CompilerParams(dimension_semantics=("arbitrary",), vmem_limit_bytes=VMEM_LIMIT),
        name="mix",
    )(x2d, y_h, *att_o, *att_l, gh, ga, wbh, wba, wo, g1, b1, wrt, rb)


def _dest_kernel(tope_ref, rank_ref, ps_ref, dest_ref):
    tb = tope_ref.shape[1]
    ei = lax.broadcasted_iota(jnp.int32, (N_EXPERTS, tb), 0)
    ps = ps_ref[...]
    starts = [jnp.sum(jnp.where(ei == tope_ref[k:k + 1, :], ps, 0.0), axis=0, keepdims=True)
              for k in range(TOP_K)]
    dest_ref[...] = jnp.concatenate(starts, axis=0).astype(jnp.int32) + rank_ref[...]


def _dest(top_e, rank, padded_start):
    T = top_e.shape[1]
    tb = 512
    col = pl.BlockSpec((TOP_K, tb), lambda i: (0, i))
    return pl.pallas_call(
        _dest_kernel,
        grid=(T // tb,),
        in_specs=[col, col, _const_spec((N_EXPERTS, 1))],
        out_specs=col,
        out_shape=jax.ShapeDtypeStruct((TOP_K, T), jnp.int32),
        compiler_params=pltpu.CompilerParams(dimension_semantics=("parallel",)),
        name="dest",
    )(top_e, rank, padded_start.astype(F32).reshape(N_EXPERTS, 1))


def _dispatch_kernel(dest_ref, x_ref, xs_hbm, sem):
    tb = x_ref.shape[0]

    def issue(g, c):
        t0 = pl.multiple_of(g * SUBLANES, SUBLANES)
        for j in range(SUBLANES):
            for k in range(TOP_K):
                pltpu.make_async_copy(x_ref.at[pl.ds(t0 + j, 1), :], xs_hbm.at[dest_ref[k, t0 + j]], sem).start(priority=k % 2)
        return c

    lax.fori_loop(0, tb // SUBLANES, issue, 0)
    for _ in range(TOP_K):
        pltpu.make_async_copy(x_ref, xs_hbm.at[pl.ds(0, tb), 0, :], sem).wait()


def _dispatch(dest, x1, n_slots):
    T = x1.shape[0]
    tb = 512
    return pl.pallas_call(
        _dispatch_kernel,
        grid=(T // tb,),
        in_specs=[pl.BlockSpec((TOP_K, tb), lambda i: (0, i), memory_space=pltpu.SMEM),
                  pl.BlockSpec((tb, D_MODEL), lambda i: (i, 0))],
        out_specs=pl.BlockSpec(memory_space=pl.ANY),
        out_shape=jax.ShapeDtypeStruct((n_slots, 1, D_MODEL), F32),
        scratch_shapes=[pltpu.SemaphoreType.DMA],
        compiler_params=pltpu.CompilerParams(dimension_semantics=("arbitrary",)),
        name="dispatch",
    )(dest, x1)


N_IN_BUF = 3


def _expert_kernel(be_ref, bv_ref, nu_ref, xs_hbm, wg_ref, wu_ref, wd_ref, ys_hbm,
                   xbuf, ybuf, wg_b, wu_b, wd_b, in_sem, out_sem):
    i = pl.program_id(0)
    n_used = nu_ref[0]
    slot = i % 2
    in_slot = i % N_IN_BUF

    def in_copy(blk, s):
        return pltpu.make_async_copy(xs_hbm.at[pl.ds(blk * SLOT_BLOCK, SLOT_BLOCK), 0, :], xbuf.at[s], in_sem.at[s])

    def out_copy(blk, s):
        return pltpu.make_async_copy(ybuf.at[s], ys_hbm.at[pl.ds(blk * SLOT_BLOCK, SLOT_BLOCK), 0, :], out_sem.at[s])

    @pl.when(i == 0)
    def _():
        for b in range(N_IN_BUF - 1):
            @pl.when(b < n_used)
            def _():
                in_copy(b, b).start()

    @pl.when(i < n_used)
    def _():
        ahead = i + N_IN_BUF - 1

        @pl.when(ahead < n_used)
        def _():
            in_copy(ahead, ahead % N_IN_BUF).start()

        in_copy(i, in_slot).wait()

        prev = be_ref[jnp.maximum(i - 1, 0)]

        @pl.when(jnp.logical_or(i == 0, be_ref[i] != prev))
        def _():
            wg_b[...] = wg_ref[0].astype(BF16)
            wu_b[...] = wu_ref[0].astype(BF16)
            wd_b[...] = wd_ref[0].astype(BF16)

        rows = lax.broadcasted_iota(jnp.int32, (SLOT_BLOCK, 1), 0)
        xb = jnp.where(rows < bv_ref[i], xbuf[in_slot], 0.0).astype(BF16)
        h = _silu(jnp.dot(xb, wg_b[...], preferred_element_type=F32)) * jnp.dot(xb, wu_b[...], preferred_element_type=F32)
        y = jnp.dot(h.astype(BF16), wd_b[...], preferred_element_type=F32)

        @pl.when(i >= 2)
        def _():
            out_copy(i - 2, slot).wait()

        ybuf[slot] = y
        out_copy(i, slot).start(priority=1)

        @pl.when(i == n_used - 1)
        def _():
            @pl.when(i >= 1)
            def _():
                out_copy(i - 1, 1 - slot).wait()

            out_copy(i, slot).wait()


def _experts(block_expert, block_valid, n_used, xs, w_gate, w_up, w_down):
    n_blocks = xs.shape[0] // SLOT_BLOCK
    return pl.pallas_call(
        _expert_kernel,
        grid_spec=pltpu.PrefetchScalarGridSpec(
            num_scalar_prefetch=3,
            grid=(n_blocks,),
            in_specs=[pl.BlockSpec(memory_space=pl.ANY),
                      pl.BlockSpec((1, D_MODEL, EXPERT_HIDDEN), lambda i, be, bv, nu: (be[i], 0, 0)),
                      pl.BlockSpec((1, D_MODEL, EXPERT_HIDDEN), lambda i, be, bv, nu: (be[i], 0, 0)),
                      pl.BlockSpec((1, EXPERT_HIDDEN, D_MODEL), lambda i, be, bv, nu: (be[i], 0, 0))],
            out_specs=pl.BlockSpec(memory_space=pl.ANY),
            scratch_shapes=[pltpu.VMEM((N_IN_BUF, SLOT_BLOCK, D_MODEL), F32),
                            pltpu.VMEM((2, SLOT_BLOCK, D_MODEL), F32),
                            pltpu.VMEM((D_MODEL, EXPERT_HIDDEN), BF16),
                            pltpu.VMEM((D_MODEL, EXPERT_HIDDEN), BF16),
                            pltpu.VMEM((EXPERT_HIDDEN, D_MODEL), BF16),
                            pltpu.SemaphoreType.DMA((N_IN_BUF,)), pltpu.SemaphoreType.DMA((2,))],
        ),
        out_shape=jax.ShapeDtypeStruct(xs.shape, F32),
        compiler_params=pltpu.CompilerParams(dimension_semantics=("arbitrary",), vmem_limit_bytes=VMEM_LIMIT),
        name="experts",
    )(block_expert, block_valid, n_used, xs, w_gate, w_up, w_down)


def _combine_kernel(dest_ref, dnext_ref, x1_ref, gate_ref, p_ref, ys_hbm,
                    wsg_ref, wsu_ref, wsd_ref, g2_ref, b2_ref, wpg_ref, wpp_ref, g3_ref, b3_ref,
                    out_ref, yg_a, yg_b, sem, *, alpha):
    tm = yg_a.shape[1]
    i = pl.program_id(0)
    bufs = (yg_a, yg_b)

    def issue(d_ref, t, k, col0, half):
        pltpu.make_async_copy(ys_hbm.at[d_ref[k, col0 + t]], bufs[half].at[k, pl.ds(t, 1), :],
                              sem.at[half]).start(priority=k % 2)

    def wait_rows(half):
        for k in range(TOP_K):
            pltpu.make_async_copy(ys_hbm.at[pl.ds(0, tm), 0, :], bufs[half].at[k], sem.at[half]).wait()

    def compute(half):
        rows = pl.ds(half * tm, tm)
        x1 = x1_ref[rows, :]
        xb = x1.astype(BF16)
        hs = (_silu(jnp.dot(xb, wsg_ref[...], preferred_element_type=F32))
              * jnp.dot(xb, wsu_ref[...], preferred_element_type=F32))
        ffn = jnp.dot(hs.astype(BF16), wsd_ref[...], preferred_element_type=F32)
        gate = gate_ref[rows, :]
        for k in range(TOP_K):
            ffn = ffn + gate[:, k:k + 1] * bufs[half][k]
        x2 = _layer_norm(alpha * x1 + ffn, g2_ref[...], b2_ref[...])
        ple = (_sigmoid(jnp.dot(x2.astype(BF16), wpg_ref[...], preferred_element_type=F32))
               * jnp.dot(p_ref[rows, :].astype(BF16), wpp_ref[...], preferred_element_type=F32))
        out_ref[rows, :] = _layer_norm(alpha * x2 + ple, g3_ref[...], b3_ref[...])

    @pl.when(i == 0)
    def _():
        def first(g, c):
            t0 = pl.multiple_of(g * SUBLANES, SUBLANES)
            for j in range(SUBLANES):
                for k in range(TOP_K):
                    issue(dest_ref, t0 + j, k, 0, 0)
            return c

        lax.fori_loop(0, tm // SUBLANES, first, 0)

    wait_rows(0)
    for t in range(tm):
        for k in range(TOP_K):
            issue(dest_ref, t, k, tm, 1)
    compute(0)
    wait_rows(1)
    for t in range(tm):
        for k in range(TOP_K):
            issue(dnext_ref, t, k, 0, 0)
    compute(1)

    @pl.when(i == pl.num_programs(0) - 1)
    def _():
        wait_rows(0)


def _combine(dest, x1, gate_t, p2d, ys, wsg, wsu, wsd, g2, b2, wpg, wpp, g3, b3, alpha):
    T = x1.shape[0]
    tm = 256
    n = T // (2 * tm)
    row = lambda w: pl.BlockSpec((2 * tm, w), lambda i: (i, 0))
    return pl.pallas_call(
        functools.partial(_combine_kernel, alpha=alpha),
        grid=(n,),
        in_specs=[pl.BlockSpec((TOP_K, 2 * tm), lambda i: (0, i), memory_space=pltpu.SMEM),
                  pl.BlockSpec((TOP_K, tm), lambda i: (0, jnp.minimum(2 * i + 2, 2 * n - 1)), memory_space=pltpu.SMEM),
                  row(D_MODEL), row(TOP_K), row(PLE_DIM),
                  pl.BlockSpec(memory_space=pl.ANY),
                  _const_spec((D_MODEL, SHARED_HIDDEN)), _const_spec((D_MODEL, SHARED_HIDDEN)),
                  _const_spec((SHARED_HIDDEN, D_MODEL)), _const_spec((1, D_MODEL)), _const_spec((1, D_MODEL)),
                  _const_spec((D_MODEL, D_MODEL)), _const_spec((PLE_DIM, D_MODEL)),
                  _const_spec((1, D_MODEL)), _const_spec((1, D_MODEL))],
        out_specs=row(D_MODEL),
        out_shape=jax.ShapeDtypeStruct((T, D_MODEL), F32),
        scratch_shapes=[pltpu.VMEM((TOP_K, tm, D_MODEL), F32), pltpu.VMEM((TOP_K, tm, D_MODEL), F32),
                        pltpu.SemaphoreType.DMA((2,))],
        compiler_params=pltpu.CompilerParams(dimension_semantics=("arbitrary",), vmem_limit_bytes=VMEM_LIMIT),
        name="combine",
    )(dest, dest, x1, gate_t, p2d, ys, wsg, wsu, wsd, g2, b2, wpg, wpp, g3, b3)


def _rotary_tables(seq_len):
    half = ROT_DIM // 2
    inv_freq = ROPE_THETA ** (-jnp.arange(half, dtype=F32) / half)
    ang = jnp.arange(seq_len, dtype=F32)[:, None] * inv_freq[None, :]
    cos, sin = jnp.cos(ang), jnp.sin(ang)
    pad = ATT_HEAD_DIM - ROT_DIM
    one = jnp.ones((seq_len, pad), F32)
    zero = jnp.zeros((seq_len, pad + half), F32)
    rc = jnp.concatenate([cos, cos, one], axis=1)
    rsa = jnp.concatenate([-sin, zero], axis=1)
    rsb = jnp.concatenate([jnp.zeros((seq_len, half), F32), sin, jnp.zeros((seq_len, pad), F32)], axis=1)
    rep = lambda a: jnp.tile(a, (1, LANES // ATT_HEAD_DIM))
    return rep(rc), rep(rsa), rep(rsb)


def _slot_layout(counts, n_blocks):
    padded = (counts + SLOT_BLOCK - 1) // SLOT_BLOCK * SLOT_BLOCK
    padded_end = jnp.cumsum(padded)
    padded_start = padded_end - padded
    blk0 = jnp.arange(n_blocks, dtype=jnp.int32) * SLOT_BLOCK
    block_expert = jnp.minimum(jnp.sum(blk0[:, None] >= padded_end[None, :], axis=1), N_EXPERTS - 1).astype(jnp.int32)
    n_used = (padded_end[-1] // SLOT_BLOCK).astype(jnp.int32)
    own = block_expert[:, None] == jnp.arange(N_EXPERTS, dtype=jnp.int32)[None, :]
    pick = lambda tab: jnp.sum(jnp.where(own, tab[None, :], 0), axis=1)
    valid = jnp.clip(pick(counts) - (blk0 - pick(padded_start)), 0, SLOT_BLOCK)
    block_valid = jnp.where(jnp.arange(n_blocks) < n_used, valid, 0).astype(jnp.int32)
    return padded_start.astype(jnp.int32), block_expert, block_valid, n_used.reshape(1)


def kernel(x, p, w_in, hg_lb_fwd, hg_lb_bwd, hg_norm_g, w_branch_hg, w_branch_att, w_out, ln1_g, ln1_b, w_router, router_bias, w_exp_gate, w_exp_up, w_exp_down, w_sh_gate, w_sh_up, w_sh_down, ln2_g, ln2_b, w_ple_gate, w_ple_proj, ln3_g, ln3_b):
    B, L, D = x.shape
    depth = w_in.shape[0]
    T = B * L
    alpha = (2 * depth) ** 0.25
    n_assign = T * TOP_K
    n_blocks = (n_assign + N_EXPERTS * (SLOT_BLOCK - 1) + SLOT_BLOCK - 1) // SLOT_BLOCK
    lb_fwd_all = jnp.cumsum(jax.nn.softmax(hg_lb_fwd.astype(F32), axis=0), axis=0)
    lb_bwd_all = jnp.cumsum(jax.nn.softmax(hg_lb_bwd.astype(F32), axis=0), axis=0)
    rc, rsa, rsb = _rotary_tables(L)
    row = lambda a: a.reshape(1, -1)

    x2d = x.reshape(T, D)
    for i in range(depth):
        (hq, hv, lff, lfb, hg, aq, ak, av, gh, ga) = _in_proj(
            x2d, w_in[i].astype(BF16), row(lb_fwd_all[i]), row(lb_bwd_all[i]), rc, rsa, rsb, L)
        y_h = _hgrn(hq, hv, lff, lfb, hg, hg_norm_g[i], B, L).reshape(T, HG_WIDTH)
        att = [_attn_group(aq, ak, av, g, B, L) for g in range(len(DIL_PAIRS))]
        x1, top_e, gate, rank, cnt = _mix(
            x2d, y_h, [a[0] for a in att], [a[1] for a in att], gh, ga,
            w_branch_hg[i].astype(BF16), w_branch_att[i].astype(BF16), w_out[i].astype(BF16),
            row(ln1_g[i]), row(ln1_b[i]), w_router[i].T.astype(BF16), router_bias[i].reshape(N_EXPERTS, 1), alpha)
        counts = cnt[:, 0].astype(jnp.int32)
        padded_start, block_expert, block_valid, n_used = _slot_layout(counts, n_blocks)
        dest = _dest(top_e, rank, padded_start)
        xs = _dispatch(dest, x1, n_blocks * SLOT_BLOCK)
        ys = _experts(block_expert, block_valid, n_used, xs, w_exp_gate[i], w_exp_up[i], w_exp_down[i])
        x2d = _combine(dest, x1, gate.T, p[i].reshape(T, PLE_DIM), ys,
                       w_sh_gate[i].astype(BF16), w_sh_up[i].astype(BF16), w_sh_down[i].astype(BF16),
                       row(ln2_g[i]), row(ln2_b[i]), w_ple_gate[i].astype(BF16), w_ple_proj[i].astype(BF16),
                       row(ln3_g[i]), row(ln3_b[i]), alpha)
    return x2d.reshape(B, L, D)
```

```python
import functools

import jax
import jax.numpy as jnp
import numpy as np
from jax import lax
from jax.experimental import pallas as pl
from jax.experimental.pallas import tpu as pltpu

F32 = jnp.float32
BF16 = jnp.bfloat16

D_MODEL = 1024
HG_HEADS = 4
HG_DIM = 128
HG_WIDTH = HG_HEADS * HG_DIM
HG_CHUNK = 64
DIL_PAIRS = ((128, 1), (512, 4), (2048, 16))
ATT_SLOTS = 4
ATT_HEAD_DIM = 64
ATT_WIDTH = len(DIL_PAIRS) * ATT_SLOTS * ATT_HEAD_DIM
ATT_OUT = ATT_SLOTS * ATT_HEAD_DIM
ROT_DIM = ATT_HEAD_DIM // 4
ROPE_THETA = 500000.0
COL_SIZES = (HG_WIDTH,) * 5 + (ATT_WIDTH,) * 3 + (D_MODEL,) * 2
COL_STARTS = tuple(int(v) for v in np.cumsum((0,) + COL_SIZES)[:-1])
IN_COLS = sum(COL_SIZES)
N_EXPERTS = 256
TOP_K = 8
N_GROUPS = 8
TOPK_GROUPS = 4
PER_GROUP = N_EXPERTS // N_GROUPS
EXPERT_HIDDEN = 256
SHARED_HIDDEN = 256
ROUTED_SCALE = 2.5
SLOT_BLOCK = 512
PLE_DIM = 256
LN_EPS = 1e-5
NEG_INF = -1e30

LANES = 128
SUBLANES = 8
VMEM_LIMIT = 56 * 1024 * 1024

NT_DIMS = (((1,), (1,)), ((), ()))
TN_DIMS = (((0,), (0,)), ((), ()))


def _sigmoid(v):
    return jax.nn.sigmoid(v)


def _silu(v):
    return v * jax.nn.sigmoid(v)


def _layer_norm(v, g, b):
    mu = jnp.mean(v, axis=-1, keepdims=True)
    vc = v - mu
    var = jnp.mean(vc * vc, axis=-1, keepdims=True)
    return vc * lax.rsqrt(var + LN_EPS) * g + b


def _const_spec(shape):
    return pl.BlockSpec(shape, lambda *_: (0,) * len(shape))


def _in_proj_kernel(x_ref, w_ref, lbf_ref, lbb_ref, rc_ref, rsa_ref, rsb_ref,
                    hq_ref, hv_ref, lff_ref, lfb_ref, hg_ref,
                    aq_ref, ak_ref, av_ref, gh_ref, ga_ref):
    xb = x_ref[...].astype(BF16)

    def proj(seg):
        c0, width = COL_STARTS[seg], COL_SIZES[seg]
        return jnp.dot(xb, w_ref[:, c0:c0 + width], preferred_element_type=F32)

    hq_ref[...] = _silu(proj(0)).astype(BF16)
    hv_ref[...] = proj(1).astype(BF16)
    lb = lbf_ref[...]
    lff_ref[...] = jnp.log(lb + (1.0 - lb) * _sigmoid(proj(2)))
    lb = lbb_ref[...]
    lfb_ref[...] = jnp.log(lb + (1.0 - lb) * _sigmoid(proj(3)))
    hg_ref[...] = _silu(proj(4)).astype(BF16)

    rc, rsa, rsb = rc_ref[...], rsa_ref[...], rsb_ref[...]

    def rotary(z, out_ref, scale):
        for s in range(ATT_WIDTH // LANES):
            t = z[:, s * LANES:(s + 1) * LANES]
            up = pltpu.roll(t, LANES - ROT_DIM // 2, 1)
            dn = pltpu.roll(t, ROT_DIM // 2, 1)
            r = t * rc + up * rsa + dn * rsb
            out_ref[:, s * LANES:(s + 1) * LANES] = (r * scale).astype(BF16)

    rotary(proj(5), aq_ref, ATT_HEAD_DIM ** -0.5)
    rotary(proj(6), ak_ref, 1.0)
    av_ref[...] = proj(7).astype(BF16)
    gh_ref[...] = _sigmoid(proj(8)).astype(BF16)
    ga_ref[...] = _sigmoid(proj(9)).astype(BF16)


def _in_proj(x2d, w_b, lbf, lbb, rc, rsa, rsb, seq_len):
    T = x2d.shape[0]
    tm = 512
    n_pos_blocks = seq_len // tm
    row = lambda w: pl.BlockSpec((tm, w), lambda i: (i, 0))
    tab = pl.BlockSpec((tm, LANES), lambda i: (i % n_pos_blocks, 0))
    out_w = (HG_WIDTH, HG_WIDTH, HG_WIDTH, HG_WIDTH, HG_WIDTH, ATT_WIDTH, ATT_WIDTH, ATT_WIDTH, D_MODEL, D_MODEL)
    out_dt = (BF16, BF16, F32, F32, BF16, BF16, BF16, BF16, BF16, BF16)
    return pl.pallas_call(
        _in_proj_kernel,
        grid=(T // tm,),
        in_specs=[row(D_MODEL),
                  pl.BlockSpec((D_MODEL, IN_COLS), lambda i: (0, 0), pipeline_mode=pl.Buffered(1)),
                  _const_spec((1, HG_WIDTH)), _const_spec((1, HG_WIDTH)), tab, tab, tab],
        out_specs=[row(w) for w in out_w],
        out_shape=[jax.ShapeDtypeStruct((T, w), dt) for w, dt in zip(out_w, out_dt)],
        compiler_params=pltpu.CompilerParams(dimension_semantics=("parallel",), vmem_limit_bytes=VMEM_LIMIT),
        name="in_proj",
    )(x2d, w_b, lbf, lbb, rc, rsa, rsb)


def _split3(a):
    hi = a.astype(BF16)
    r1 = a - hi.astype(F32)
    mid = r1.astype(BF16)
    lo = (r1 - mid.astype(F32)).astype(BF16)
    return jnp.concatenate([hi, mid, lo], axis=1)


HG_SUPER = 4 * HG_CHUNK


def _hgrn_kernel(q_ref, v_ref, lff_ref, lfb_ref, sg_ref, ng_ref, y_ref,
                 of_scr, ob_scr, qf_scr, qb_scr, df_scr, db_scr, mf_scr, mb_scr):
    L = q_ref.shape[1]
    C, SC = HG_CHUNK, HG_SUPER
    n = L // C
    r_i = lax.broadcasted_iota(jnp.int32, (SC, SC), 0)
    c_i = lax.broadcasted_iota(jnp.int32, (SC, SC), 1)
    same = (r_i // C) == (c_i // C)
    r_l, c_l = r_i % C, c_i % C
    mask_f = jnp.logical_and(same, c_l <= r_l)
    mask_b = jnp.logical_and(same, c_l >= r_l)
    tri_f = mask_f.astype(BF16)
    tri_b = mask_b.astype(BF16)

    def chunk_rows(b, row):
        return jnp.concatenate([jnp.broadcast_to(b[c * C + row:c * C + row + 1, :], (C, HG_DIM))
                                for c in range(SC // C)], axis=0)

    dirs = ((lff_ref, tri_f, mask_f, C // 2 - 1, C - 1, of_scr, qf_scr, df_scr, mf_scr),
            (lfb_ref, tri_b, mask_b, C // 2, 0, ob_scr, qb_scr, db_scr, mb_scr))

    def local_body(i, carry):
        i0 = pl.multiple_of(i * SC, SC)
        rows = pl.ds(i0, SC)
        q = q_ref[0, rows, :].astype(F32)
        v = v_ref[0, rows, :]
        lfs = [d[0][0, rows, :] for d in dirs]
        sums = [jnp.dot(d[1], _split3(lf), preferred_element_type=F32) for d, lf in zip(dirs, lfs)]
        a_qs, a_ks, k_outs = [], [], []
        for d, lf, p in zip(dirs, lfs, sums):
            b = p[:, :HG_DIM] + p[:, HG_DIM:2 * HG_DIM] + p[:, 2 * HG_DIM:]
            b_ref, b_last = chunk_rows(b, d[3]), chunk_rows(b, d[4])
            k = 1.0 - jnp.exp(lf)
            a_qs.append((q * jnp.exp(b - b_ref)).astype(BF16))
            a_ks.append((k * jnp.exp(b_ref - b)).astype(BF16))
            k_outs.append((k * jnp.exp(b_last - b)).astype(BF16))
            d[6][rows, :] = (q * jnp.exp(b)).astype(BF16)
            d[7][rows, :] = b_last
        scores = [lax.dot_general(a_q, a_k, NT_DIMS, preferred_element_type=F32) for a_q, a_k in zip(a_qs, a_ks)]
        states = [[lax.dot_general(v[c * C:(c + 1) * C], k_out[c * C:(c + 1) * C], TN_DIMS,
                                   preferred_element_type=F32) for c in range(SC // C)] for k_out in k_outs]
        masked = [jnp.where(d[2], s, 0.0).astype(BF16) for d, s in zip(dirs, scores)]
        intra = [jnp.dot(m, v, preferred_element_type=F32) for m in masked]
        for d, o, st in zip(dirs, intra, states):
            d[5][rows, :] = o
            for c in range(SC // C):
                d[8][i0 // C + c] = st[c]
        return carry

    lax.fori_loop(0, L // SC, local_body, 0)

    def carry_part(c, st, o_scr, qin_scr, dec_scr, m_scr):
        rows = pl.ds(pl.multiple_of(c * C, C), C)
        o_scr[rows, :] = o_scr[rows, :] + lax.dot_general(qin_scr[rows, :], st.astype(BF16), NT_DIMS,
                                                          preferred_element_type=F32)
        return st * jnp.exp(dec_scr[pl.ds(pl.multiple_of(c * C, C), 1), :]) + m_scr[c]

    def carry_body(c, carry):
        st_f, st_b = carry
        st_f = carry_part(c, st_f, of_scr, qf_scr, df_scr, mf_scr)
        st_b = carry_part(n - 1 - c, st_b, ob_scr, qb_scr, db_scr, mb_scr)
        return st_f, st_b

    z = jnp.zeros((HG_DIM, HG_DIM), F32)
    lax.fori_loop(0, n, carry_body, (z, z), unroll=4)

    o = of_scr[...] + ob_scr[...]
    o = o * lax.rsqrt(jnp.mean(o * o, axis=-1, keepdims=True) + LN_EPS) * ng_ref[...]
    y_ref[0] = (o * sg_ref[0].astype(F32)).astype(BF16)


def _hgrn(hq, hv, lff, lfb, hg, norm_g, batch, seq_len):
    blk = pl.BlockSpec((1, seq_len, HG_DIM), lambda b, h: (b, 0, h))
    r3 = lambda a: a.reshape(batch, seq_len, HG_WIDTH)
    per_dir = [pltpu.VMEM((seq_len, HG_DIM), F32), pltpu.VMEM((seq_len, HG_DIM), BF16),
               pltpu.VMEM((seq_len, HG_DIM), F32), pltpu.VMEM((seq_len // HG_CHUNK, HG_DIM, HG_DIM), F32)]
    scratch = [s for pair in zip(per_dir, per_dir) for s in pair]
    return pl.pallas_call(
        _hgrn_kernel,
        grid=(batch, HG_HEADS),
        in_specs=[blk, blk, blk, blk, blk, pl.BlockSpec((1, HG_DIM), lambda b, h: (0, h))],
        out_specs=blk,
        out_shape=jax.ShapeDtypeStruct((batch, seq_len, HG_WIDTH), BF16),
        scratch_shapes=scratch,
        compiler_params=pltpu.CompilerParams(dimension_semantics=("parallel", "parallel"),
                                             vmem_limit_bytes=VMEM_LIMIT),
        name="hgrn",
    )(r3(hq), r3(hv), r3(lff), r3(lfb), r3(hg), norm_g.reshape(1, HG_WIDTH))


def _attn_kernel(q_ref, k_ref, v_ref, o_ref, lse_ref, *scratch, m_len, dil, qb, kw, span):
    lane = lax.broadcasted_iota(jnp.int32, (1, LANES), 1)
    head0 = lane < ATT_HEAD_DIM
    qi_l = lax.broadcasted_iota(jnp.int32, (qb, kw), 0)
    ki_l = lax.broadcasted_iota(jnp.int32, (qb, kw), 1)

    def run_class(load_q, load_k, load_v, store_o, store_l):
        def block(i, carry):
            q0 = pl.multiple_of(i * qb, qb)
            ks = pl.multiple_of(jnp.clip(q0 - span, 0, m_len - kw), span)
            mask = jnp.abs((q0 + qi_l) - (ks + ki_l)) <= span
            slabs = [slice(sl * LANES, (sl + 1) * LANES) for sl in range(ATT_OUT // LANES)]
            heads = (head0, jnp.logical_not(head0))
            qkv = [(load_q(q0, qb, cols), load_k(ks, kw, cols), load_v(ks, kw, cols)) for cols in slabs]
            scores = [lax.dot_general(jnp.where(hm, q2, jnp.zeros_like(q2)), k2, NT_DIMS, preferred_element_type=F32)
                      for (q2, k2, _) in qkv for hm in heads]
            probs, dens, lses = [], [], []
            for s in scores:
                s = jnp.where(mask, s, NEG_INF)
                mx = jnp.max(s, axis=-1, keepdims=True)
                e = jnp.exp(s - mx)
                den = jnp.sum(e, axis=-1, keepdims=True)
                probs.append(e.astype(BF16))
                dens.append(den)
                lses.append(mx + jnp.log(den))
            outs = [jnp.dot(pr, qkv[c // 2][2], preferred_element_type=F32) / dens[c] for c, pr in enumerate(probs)]
            for sl, cols in enumerate(slabs):
                store_o(q0, qb, cols, jnp.where(head0, outs[2 * sl], outs[2 * sl + 1]))
                store_l(q0, qb, cols, jnp.where(head0, lses[2 * sl], lses[2 * sl + 1]))
            return carry

        lax.fori_loop(0, m_len // qb, block, 0)

    def loader(ref):
        return lambda r0, n, cols: ref[0, pl.ds(r0, n), cols]

    if dil == 1:
        def store_o(r0, n, cols, val):
            o_ref[0, pl.ds(r0, n), cols] = val.astype(BF16)

        def store_l(r0, n, cols, val):
            lse_ref[0, pl.ds(r0, n), cols] = val

        run_class(loader(q_ref), loader(k_ref), loader(v_ref), store_o, store_l)
        return

    q32, k32, v32, o32, l32, qc, kc, vc, oc, lc = scratch
    n_slab = ATT_OUT // LANES
    for slab in range(n_slab):
        cols = slice(slab * LANES, (slab + 1) * LANES)
        q32[slab] = q_ref[0, :, cols].astype(F32)
        k32[slab] = k_ref[0, :, cols].astype(F32)
        v32[slab] = v_ref[0, :, cols].astype(F32)

    def cls_loader(ref):
        return lambda r0, n, cols: ref[pl.ds(r0, n), cols]

    def store_oc(r0, n, cols, val):
        oc[pl.ds(r0, n), cols] = val

    def store_lc(r0, n, cols, val):
        lc[pl.ds(r0, n), cols] = val

    for r in range(dil):
        rows = pl.ds(r, m_len, stride=dil)
        for slab in range(n_slab):
            cols = slice(slab * LANES, (slab + 1) * LANES)
            qc[:, cols] = q32[slab, rows, :].astype(BF16)
            kc[:, cols] = k32[slab, rows, :].astype(BF16)
            vc[:, cols] = v32[slab, rows, :].astype(BF16)
        run_class(cls_loader(qc), cls_loader(kc), cls_loader(vc), store_oc, store_lc)
        for slab in range(n_slab):
            cols = slice(slab * LANES, (slab + 1) * LANES)
            o32[slab, rows, :] = oc[:, cols]
            l32[slab, rows, :] = lc[:, cols]
    for slab in range(n_slab):
        cols = slice(slab * LANES, (slab + 1) * LANES)
        o_ref[0, :, cols] = o32[slab].astype(BF16)
        lse_ref[0, :, cols] = l32[slab]


def _attn_group(aq, ak, av, g, batch, seq_len):
    window, dil = DIL_PAIRS[g]
    span = window // (2 * dil)
    m_len = seq_len // dil
    qb = min(2 * span, m_len)
    kw = min(qb + 2 * span, m_len)
    view = lambda a: a.reshape(batch, seq_len, ATT_WIDTH)
    in_blk = pl.BlockSpec((1, seq_len, ATT_OUT), lambda b: (b, 0, g))
    out_blk = pl.BlockSpec((1, seq_len, ATT_OUT), lambda b: (b, 0, 0))
    scratch = []
    if dil > 1:
        scratch = ([pltpu.VMEM((ATT_OUT // LANES, seq_len, LANES), F32)] * 5 + [pltpu.VMEM((m_len, ATT_OUT), BF16)] * 3
                   + [pltpu.VMEM((m_len, ATT_OUT), F32)] * 2)
    o, lse = pl.pallas_call(
        functools.partial(_attn_kernel, m_len=m_len, dil=dil, qb=qb, kw=kw, span=span),
        grid=(batch,),
        in_specs=[in_blk, in_blk, in_blk],
        out_specs=[out_blk, out_blk],
        out_shape=[jax.ShapeDtypeStruct((batch, seq_len, ATT_OUT), BF16),
                   jax.ShapeDtypeStruct((batch, seq_len, ATT_OUT), F32)],
        scratch_shapes=scratch,
        compiler_params=pltpu.CompilerParams(dimension_semantics=("parallel",), vmem_limit_bytes=VMEM_LIMIT),
        name=f"attn_g{g}",
    )(view(aq), view(ak), view(av))
    T = batch * seq_len
    return o.reshape(T, ATT_OUT), lse.reshape(T, ATT_OUT)


def _mix_kernel(x_ref, yh_ref, o0_ref, o1_ref, o2_ref, l0_ref, l1_ref, l2_ref, gh_ref, ga_ref,
                wbh_ref, wba_ref, wo_ref, g1_ref, b1_ref, wrt_ref, rb_ref,
                x1_ref, tope_ref, gate_ref, rank_ref, cnt_ref, carry_scr, *, alpha):
    tm = x_ref.shape[0]

    @pl.when(pl.program_id(0) == 0)
    def _():
        carry_scr[...] = jnp.zeros_like(carry_scr)

    l0, l1, l2 = l0_ref[...], l1_ref[...], l2_ref[...]
    lm = jnp.maximum(jnp.maximum(l0, l1), l2)
    e0, e1, e2 = jnp.exp(l0 - lm), jnp.exp(l1 - lm), jnp.exp(l2 - lm)
    es = e0 + e1 + e2
    y_a = ((e0 / es) * o0_ref[...].astype(F32) + (e1 / es) * o1_ref[...].astype(F32)
           + (e2 / es) * o2_ref[...].astype(F32)).astype(BF16)
    merged = (gh_ref[...].astype(F32) * jnp.dot(yh_ref[...], wbh_ref[...], preferred_element_type=F32)
              + ga_ref[...].astype(F32) * jnp.dot(y_a, wba_ref[...], preferred_element_type=F32))
    mixed = jnp.dot(merged.astype(BF16), wo_ref[...], preferred_element_type=F32)
    x1 = _layer_norm(alpha * x_ref[...] + mixed, g1_ref[...], b1_ref[...])
    x1_ref[...] = x1

    logit = lax.dot_general(wrt_ref[...], x1.astype(BF16), NT_DIMS, preferred_element_type=F32)
    score = _sigmoid(logit)
    biased = score + rb_ref[...]
    b3 = biased.reshape(N_GROUPS, PER_GROUP, tm)
    i3 = lax.broadcasted_iota(jnp.int32, (N_GROUPS, PER_GROUP, tm), 1)
    m1 = jnp.max(b3, axis=1, keepdims=True)
    idx1 = jnp.min(jnp.where(b3 == m1, i3, PER_GROUP), axis=1, keepdims=True)
    m2 = jnp.max(jnp.where(i3 == idx1, -jnp.inf, b3), axis=1, keepdims=True)
    gs = (m1 + m2).reshape(N_GROUPS, tm)
    gi = lax.broadcasted_iota(jnp.int32, (N_GROUPS, tm), 0)
    sel = jnp.zeros((N_GROUPS, tm), F32)
    cur = gs
    for _ in range(TOPK_GROUPS):
        m = jnp.max(cur, axis=0, keepdims=True)
        ix = jnp.min(jnp.where(cur == m, gi, N_GROUPS), axis=0, keepdims=True)
        hit = gi == ix
        sel = jnp.where(hit, 1.0, sel)
        cur = jnp.where(hit, -jnp.inf, cur)
    cur = jnp.where(sel.reshape(N_GROUPS, 1, tm) > 0.0, b3, -jnp.inf).reshape(N_EXPERTS, tm)
    ei = lax.broadcasted_iota(jnp.int32, (N_EXPERTS, tm), 0)
    chosen = jnp.zeros((N_EXPERTS, tm), F32)
    top_e, top_s = [], []
    for _ in range(TOP_K):
        m = jnp.max(cur, axis=0, keepdims=True)
        ix = jnp.min(jnp.where(cur == m, ei, N_EXPERTS), axis=0, keepdims=True)
        hit = ei == ix
        top_e.append(ix)
        top_s.append(jnp.sum(jnp.where(hit, score, 0.0), axis=0, keepdims=True))
        chosen = jnp.where(hit, 1.0, chosen)
        cur = jnp.where(hit, -jnp.inf, cur)
    s_sum = top_s[0]
    for s in top_s[1:]:
        s_sum = s_sum + s
    tope_ref[...] = jnp.concatenate(top_e, axis=0)
    gate_ref[...] = jnp.concatenate([s / s_sum * ROUTED_SCALE for s in top_s], axis=0)

    su = lax.broadcasted_iota(jnp.int32, (tm, tm), 0)
    tu = lax.broadcasted_iota(jnp.int32, (tm, tm), 1)
    before = (su < tu).astype(BF16)
    prior = carry_scr[:, 0:1] + jnp.dot(chosen.astype(BF16), before, preferred_element_type=F32)
    ranks = [jnp.sum(jnp.where(ei == ix, prior, 0.0), axis=0, keepdims=True) for ix in top_e]
    rank_ref[...] = jnp.concatenate(ranks, axis=0).astype(jnp.int32)
    carry_scr[...] = carry_scr[...] + jnp.sum(chosen, axis=1, keepdims=True)
    cnt_ref[...] = carry_scr[...]


def _mix(x2d, y_h, att_o, att_l, gh, ga, wbh, wba, wo, g1, b1, wrt, rb, alpha):
    T = x2d.shape[0]
    tm = 256
    row = lambda w: pl.BlockSpec((tm, w), lambda i: (i, 0))
    col = pl.BlockSpec((TOP_K, tm), lambda i: (0, i))
    return pl.pallas_call(
        functools.partial(_mix_kernel, alpha=alpha),
        grid=(T // tm,),
        in_specs=[row(D_MODEL), row(HG_WIDTH)] + [row(ATT_OUT)] * 6 + [row(D_MODEL), row(D_MODEL),
                  _const_spec((HG_WIDTH, D_MODEL)), _const_spec((ATT_OUT, D_MODEL)),
                  _const_spec((D_MODEL, D_MODEL)), _const_spec((1, D_MODEL)), _const_spec((1, D_MODEL)),
                  _const_spec((N_EXPERTS, D_MODEL)), _const_spec((N_EXPERTS, 1))],
        out_specs=[row(D_MODEL), col, col, col, _const_spec((N_EXPERTS, LANES))],
        out_shape=[jax.ShapeDtypeStruct((T, D_MODEL), F32),
                   jax.ShapeDtypeStruct((TOP_K, T), jnp.int32),
                   jax.ShapeDtypeStruct((TOP_K, T), F32),
                   jax.ShapeDtypeStruct((TOP_K, T), jnp.int32),
                   jax.ShapeDtypeStruct((N_EXPERTS, LANES), F32)],
        scratch_shapes=[pltpu.VMEM((N_EXPERTS, LANES), F32)],
        compiler_params=pltpu.CompilerParams(dimension_semantics=("arbitrary",), vmem_limit_bytes=VMEM_LIMIT),
        name="mix",
    )(x2d, y_h, *att_o, *att_l, gh, ga, wbh, wba, wo, g1, b1, wrt, rb)


def _dest_kernel(tope_ref, rank_ref, ps_ref, dest_ref):
    tb = tope_ref.shape[1]
    ei = lax.broadcasted_iota(jnp.int32, (N_EXPERTS, tb), 0)
    ps = ps_ref[...]
    starts = [jnp.sum(jnp.where(ei == tope_ref[k:k + 1, :], ps, 0.0), axis=0, keepdims=True)
              for k in range(TOP_K)]
    dest_ref[...] = jnp.concatenate(starts, axis=0).astype(jnp.int32) + rank_ref[...]


def _dest(top_e, rank, padded_start):
    T = top_e.shape[1]
    tb = 512
    col = pl.BlockSpec((TOP_K, tb), lambda i: (0, i))
    return pl.pallas_call(
        _dest_kernel,
        grid=(T // tb,),
        in_specs=[col, col, _const_spec((N_EXPERTS, 1))],
        out_specs=col,
        out_shape=jax.ShapeDtypeStruct((TOP_K, T), jnp.int32),
        compiler_params=pltpu.CompilerParams(dimension_semantics=("parallel",)),
        name="dest",
    )(top_e, rank, padded_start.astype(F32).reshape(N_EXPERTS, 1))


def _dispatch_kernel(dest_ref, x_ref, xs_hbm, sem):
    tb = x_ref.shape[0]

    def issue(g, c):
        t0 = pl.multiple_of(g * SUBLANES, SUBLANES)
        for j in range(SUBLANES):
            for k in range(TOP_K):
                pltpu.make_async_copy(x_ref.at[pl.ds(t0 + j, 1), :], xs_hbm.at[dest_ref[k, t0 + j]], sem).start(priority=k % 2)
        return c

    lax.fori_loop(0, tb // SUBLANES, issue, 0)
    for _ in range(TOP_K):
        pltpu.make_async_copy(x_ref, xs_hbm.at[pl.ds(0, tb), 0, :], sem).wait()


def _dispatch(dest, x1, n_slots):
    T = x1.shape[0]
    tb = 512
    return pl.pallas_call(
        _dispatch_kernel,
        grid=(T // tb,),
        in_specs=[pl.BlockSpec((TOP_K, tb), lambda i: (0, i), memory_space=pltpu.SMEM),
                  pl.BlockSpec((tb, D_MODEL), lambda i: (i, 0))],
        out_specs=pl.BlockSpec(memory_space=pl.ANY),
        out_shape=jax.ShapeDtypeStruct((n_slots, 1, D_MODEL), F32),
        scratch_shapes=[pltpu.SemaphoreType.DMA],
        compiler_params=pltpu.CompilerParams(dimension_semantics=("arbitrary",)),
        name="dispatch",
    )(dest, x1)


N_IN_BUF = 3


def _expert_kernel(be_ref, bv_ref, nu_ref, xs_hbm, wg_ref, wu_ref, wd_ref, ys_hbm,
                   xbuf, ybuf, wg_b, wu_b, wd_b, in_sem, out_sem):
    i = pl.program_id(0)
    n_used = nu_ref[0]
    slot = i % 2
    in_slot = i % N_IN_BUF

    def in_copy(blk, s):
        return pltpu.make_async_copy(xs_hbm.at[pl.ds(blk * SLOT_BLOCK, SLOT_BLOCK), 0, :], xbuf.at[s], in_sem.at[s])

    def out_copy(blk, s):
        return pltpu.make_async_copy(ybuf.at[s], ys_hbm.at[pl.ds(blk * SLOT_BLOCK, SLOT_BLOCK), 0, :], out_sem.at[s])

    @pl.when(i == 0)
    def _():
        for b in range(N_IN_BUF - 1):
            @pl.when(b < n_used)
            def _():
                in_copy(b, b).start()

    @pl.when(i < n_used)
    def _():
        ahead = i + N_IN_BUF - 1

        @pl.when(ahead < n_used)
        def _():
            in_copy(ahead, ahead % N_IN_BUF).start()

        in_copy(i, in_slot).wait()

        prev = be_ref[jnp.maximum(i - 1, 0)]

        @pl.when(jnp.logical_or(i == 0, be_ref[i] != prev))
        def _():
            wg_b[...] = wg_ref[0].astype(BF16)
            wu_b[...] = wu_ref[0].astype(BF16)
            wd_b[...] = wd_ref[0].astype(BF16)

        rows = lax.broadcasted_iota(jnp.int32, (SLOT_BLOCK, 1), 0)
        xb = jnp.where(rows < bv_ref[i], xbuf[in_slot], 0.0).astype(BF16)
        h = _silu(jnp.dot(xb, wg_b[...], preferred_element_type=F32)) * jnp.dot(xb, wu_b[...], preferred_element_type=F32)
        y = jnp.dot(h.astype(BF16), wd_b[...], preferred_element_type=F32)

        @pl.when(i >= 2)
        def _():
            out_copy(i - 2, slot).wait()

        ybuf[slot] = y
        out_copy(i, slot).start(priority=1)

        @pl.when(i == n_used - 1)
        def _():
            @pl.when(i >= 1)
            def _():
                out_copy(i - 1, 1 - slot).wait()

            out_copy(i, slot).wait()


def _experts(block_expert, block_valid, n_used, xs, w_gate, w_up, w_down):
    n_blocks = xs.shape[0] // SLOT_BLOCK
    return pl.pallas_call(
        _expert_kernel,
        grid_spec=pltpu.PrefetchScalarGridSpec(
            num_scalar_prefetch=3,
            grid=(n_blocks,),
            in_specs=[pl.BlockSpec(memory_space=pl.ANY),
                      pl.BlockSpec((1, D_MODEL, EXPERT_HIDDEN), lambda i, be, bv, nu: (be[i], 0, 0)),
                      pl.BlockSpec((1, D_MODEL, EXPERT_HIDDEN), lambda i, be, bv, nu: (be[i], 0, 0)),
                      pl.BlockSpec((1, EXPERT_HIDDEN, D_MODEL), lambda i, be, bv, nu: (be[i], 0, 0))],
            out_specs=pl.BlockSpec(memory_space=pl.ANY),
            scratch_shapes=[pltpu.VMEM((N_IN_BUF, SLOT_BLOCK, D_MODEL), F32),
                            pltpu.VMEM((2, SLOT_BLOCK, D_MODEL), F32),
                            pltpu.VMEM((D_MODEL, EXPERT_HIDDEN), BF16),
                            pltpu.VMEM((D_MODEL, EXPERT_HIDDEN), BF16),
                            pltpu.VMEM((EXPERT_HIDDEN, D_MODEL), BF16),
                            pltpu.SemaphoreType.DMA((N_IN_BUF,)), pltpu.SemaphoreType.DMA((2,))],
        ),
        out_shape=jax.ShapeDtypeStruct(xs.shape, F32),
        compiler_params=pltpu.CompilerParams(dimension_semantics=("arbitrary",), vmem_limit_bytes=VMEM_LIMIT),
        name="experts",
    )(block_expert, block_valid, n_used, xs, w_gate, w_up, w_down)


def _combine_kernel(dest_ref, dnext_ref, x1_ref, gate_ref, p_ref, ys_hbm,
                    wsg_ref, wsu_ref, wsd_ref, g2_ref, b2_ref, wpg_ref, wpp_ref, g3_ref, b3_ref,
                    out_ref, yg_a, yg_b, sem, *, alpha):
    tm = yg_a.shape[1]
    i = pl.program_id(0)
    bufs = (yg_a, yg_b)

    def issue(d_ref, t, k, col0, half):
        pltpu.make_async_copy(ys_hbm.at[d_ref[k, col0 + t]], bufs[half].at[k, pl.ds(t, 1), :],
                              sem.at[half]).start(priority=k % 2)

    def wait_rows(half):
        for k in range(TOP_K):
            pltpu.make_async_copy(ys_hbm.at[pl.ds(0, tm), 0, :], bufs[half].at[k], sem.at[half]).wait()

    def compute(half):
        rows = pl.ds(half * tm, tm)
        x1 = x1_ref[rows, :]
        xb = x1.astype(BF16)
        hs = (_silu(jnp.dot(xb, wsg_ref[...], preferred_element_type=F32))
              * jnp.dot(xb, wsu_ref[...], preferred_element_type=F32))
        ffn = jnp.dot(hs.astype(BF16), wsd_ref[...], preferred_element_type=F32)
        gate = gate_ref[rows, :]
        for k in range(TOP_K):
            ffn = ffn + gate[:, k:k + 1] * bufs[half][k]
        x2 = _layer_norm(alpha * x1 + ffn, g2_ref[...], b2_ref[...])
        ple = (_sigmoid(jnp.dot(x2.astype(BF16), wpg_ref[...], preferred_element_type=F32))
               * jnp.dot(p_ref[rows, :].astype(BF16), wpp_ref[...], preferred_element_type=F32))
        out_ref[rows, :] = _layer_norm(alpha * x2 + ple, g3_ref[...], b3_ref[...])

    @pl.when(i == 0)
    def _():
        def first(g, c):
            t0 = pl.multiple_of(g * SUBLANES, SUBLANES)
            for j in range(SUBLANES):
                for k in range(TOP_K):
                    issue(dest_ref, t0 + j, k, 0, 0)
            return c

        lax.fori_loop(0, tm // SUBLANES, first, 0)

    wait_rows(0)
    for t in range(tm):
        for k in range(TOP_K):
            issue(dest_ref, t, k, tm, 1)
    compute(0)
    wait_rows(1)
    for t in range(tm):
        for k in range(TOP_K):
            issue(dnext_ref, t, k, 0, 0)
    compute(1)

    @pl.when(i == pl.num_programs(0) - 1)
    def _():
        wait_rows(0)


def _combine(dest, x1, gate_t, p2d, ys, wsg, wsu, wsd, g2, b2, wpg, wpp, g3, b3, alpha):
    T = x1.shape[0]
    tm = 256
    n = T // (2 * tm)
    row = lambda w: pl.BlockSpec((2 * tm, w), lambda i: (i, 0))
    return pl.pallas_call(
        functools.partial(_combine_kernel, alpha=alpha),
        grid=(n,),
        in_specs=[pl.BlockSpec((TOP_K, 2 * tm), lambda i: (0, i), memory_space=pltpu.SMEM),
                  pl.BlockSpec((TOP_K, tm), lambda i: (0, jnp.minimum(2 * i + 2, 2 * n - 1)), memory_space=pltpu.SMEM),
                  row(D_MODEL), row(TOP_K), row(PLE_DIM),
                  pl.BlockSpec(memory_space=pl.ANY),
                  _const_spec((D_MODEL, SHARED_HIDDEN)), _const_spec((D_MODEL, SHARED_HIDDEN)),
                  _const_spec((SHARED_HIDDEN, D_MODEL)), _const_spec((1, D_MODEL)), _const_spec((1, D_MODEL)),
                  _const_spec((D_MODEL, D_MODEL)), _const_spec((PLE_DIM, D_MODEL)),
                  _const_spec((1, D_MODEL)), _const_spec((1, D_MODEL))],
        out_specs=row(D_MODEL),
        out_shape=jax.ShapeDtypeStruct((T, D_MODEL), F32),
        scratch_shapes=[pltpu.VMEM((TOP_K, tm, D_MODEL), F32), pltpu.VMEM((TOP_K, tm, D_MODEL), F32),
                        pltpu.SemaphoreType.DMA((2,))],
        compiler_params=pltpu.CompilerParams(dimension_semantics=("arbitrary",), vmem_limit_bytes=VMEM_LIMIT),
        name="combine",
    )(dest, dest, x1, gate_t, p2d, ys, wsg, wsu, wsd, g2, b2, wpg, wpp, g3, b3)


def _rotary_tables(seq_len):
    half = ROT_DIM // 2
    inv_freq = ROPE_THETA ** (-jnp.arange(half, dtype=F32) / half)
    ang = jnp.arange(seq_len, dtype=F32)[:, None] * inv_freq[None, :]
    cos, sin = jnp.cos(ang), jnp.sin(ang)
    pad = ATT_HEAD_DIM - ROT_DIM
    one = jnp.ones((seq_len, pad), F32)
    zero = jnp.zeros((seq_len, pad + half), F32)
    rc = jnp.concatenate([cos, cos, one], axis=1)
    rsa = jnp.concatenate([-sin, zero], axis=1)
    rsb = jnp.concatenate([jnp.zeros((seq_len, half), F32), sin, jnp.zeros((seq_len, pad), F32)], axis=1)
    rep = lambda a: jnp.tile(a, (1, LANES // ATT_HEAD_DIM))
    return rep(rc), rep(rsa), rep(rsb)


def _slot_layout(counts, n_blocks):
    padded = (counts + SLOT_BLOCK - 1) // SLOT_BLOCK * SLOT_BLOCK
    padded_end = jnp.cumsum(padded)
    padded_start = padded_end - padded
    blk0 = jnp.arange(n_blocks, dtype=jnp.int32) * SLOT_BLOCK
    block_expert = jnp.minimum(jnp.sum(blk0[:, None] >= padded_end[None, :], axis=1), N_EXPERTS - 1).astype(jnp.int32)
    n_used = (padded_end[-1] // SLOT_BLOCK).astype(jnp.int32)
    own = block_expert[:, None] == jnp.arange(N_EXPERTS, dtype=jnp.int32)[None, :]
    pick = lambda tab: jnp.sum(jnp.where(own, tab[None, :], 0), axis=1)
    valid = jnp.clip(pick(counts) - (blk0 - pick(padded_start)), 0, SLOT_BLOCK)
    block_valid = jnp.where(jnp.arange(n_blocks) < n_used, valid, 0).astype(jnp.int32)
    return padded_start.astype(jnp.int32), block_expert, block_valid, n_used.reshape(1)


def kernel(x, p, w_in, hg_lb_fwd, hg_lb_bwd, hg_norm_g, w_branch_hg, w_branch_att, w_out, ln1_g, ln1_b, w_router, router_bias, w_exp_gate, w_exp_up, w_exp_down, w_sh_gate, w_sh_up, w_sh_down, ln2_g, ln2_b, w_ple_gate, w_ple_proj, ln3_g, ln3_b):
    B, L, D = x.shape
    depth = w_in.shape[0]
    T = B * L
    alpha = (2 * depth) ** 0.25
    n_assign = T * TOP_K
    n_blocks = (n_assign + N_EXPERTS * (SLOT_BLOCK - 1) + SLOT_BLOCK - 1) // SLOT_BLOCK
    lb_fwd_all = jnp.cumsum(jax.nn.softmax(hg_lb_fwd.astype(F32), axis=0), axis=0)
    lb_bwd_all = jnp.cumsum(jax.nn.softmax(hg_lb_bwd.astype(F32), axis=0), axis=0)
    rc, rsa, rsb = _rotary_tables(L)
    row = lambda a: a.reshape(1, -1)

    x2d = x.reshape(T, D)
    for i in range(depth):
        (hq, hv, lff, lfb, hg, aq, ak, av, gh, ga) = _in_proj(
            x2d, w_in[i].astype(BF16), row(lb_fwd_all[i]), row(lb_bwd_all[i]), rc, rsa, rsb, L)
        y_h = _hgrn(hq, hv, lff, lfb, hg, hg_norm_g[i], B, L).reshape(T, HG_WIDTH)
        att = [_attn_group(aq, ak, av, g, B, L) for g in range(len(DIL_PAIRS))]
        x1, top_e, gate, rank, cnt = _mix(
            x2d, y_h, [a[0] for a in att], [a[1] for a in att], gh, ga,
            w_branch_hg[i].astype(BF16), w_branch_att[i].astype(BF16), w_out[i].astype(BF16),
            row(ln1_g[i]), row(ln1_b[i]), w_router[i].T.astype(BF16), router_bias[i].reshape(N_EXPERTS, 1), alpha)
        counts = cnt[:, 0].astype(jnp.int32)
        padded_start, block_expert, block_valid, n_used = _slot_layout(counts, n_blocks)
        dest = _dest(top_e, rank, padded_start)
        xs = _dispatch(dest, x1, n_blocks * SLOT_BLOCK)
        ys = _experts(block_expert, block_valid, n_used, xs, w_exp_gate[i], w_exp_up[i], w_exp_down[i])
        x2d = _combine(dest, x1, gate.T, p[i].reshape(T, PLE_DIM), ys,
                       w_sh_gate[i].astype(BF16), w_sh_up[i].astype(BF16), w_sh_down[i].astype(BF16),
                       row(ln2_g[i]), row(ln2_b[i]), w_ple_gate[i].astype(BF16), w_ple_proj[i].astype(BF16),
                       row(ln3_g[i]), row(ln3_b[i]), alpha)
    return x2d.reshape(B, L, D)
```

```python
import functools

import jax
import jax.numpy as jnp
import numpy as np
from jax import lax
from jax.experimental import pallas as pl
from jax.experimental.pallas import tpu as pltpu

F32 = jnp.float32
BF16 = jnp.bfloat16

D_MODEL = 1024
HG_HEADS = 4
HG_DIM = 128
HG_WIDTH = HG_HEADS * HG_DIM
HG_CHUNK = 64
DIL_PAIRS = ((128, 1), (512, 4), (2048, 16))
ATT_SLOTS = 4
ATT_HEAD_DIM = 64
ATT_WIDTH = len(DIL_PAIRS) * ATT_SLOTS * ATT_HEAD_DIM
ATT_OUT = ATT_SLOTS * ATT_HEAD_DIM
ROT_DIM = ATT_HEAD_DIM // 4
ROPE_THETA = 500000.0
COL_SIZES = (HG_WIDTH,) * 5 + (ATT_WIDTH,) * 3 + (D_MODEL,) * 2
COL_STARTS = tuple(int(v) for v in np.cumsum((0,) + COL_SIZES)[:-1])
IN_COLS = sum(COL_SIZES)
N_EXPERTS = 256
TOP_K = 8
N_GROUPS = 8
TOPK_GROUPS = 4
PER_GROUP = N_EXPERTS // N_GROUPS
EXPERT_HIDDEN = 256
SHARED_HIDDEN = 256
ROUTED_SCALE = 2.5
SLOT_BLOCK = 512
PLE_DIM = 256
LN_EPS = 1e-5
NEG_INF = -1e30

LANES = 128
SUBLANES = 8
VMEM_LIMIT = 56 * 1024 * 1024

NT_DIMS = (((1,), (1,)), ((), ()))
TN_DIMS = (((0,), (0,)), ((), ()))


def _sigmoid(v):
    return jax.nn.sigmoid(v)


def _silu(v):
    return v * jax.nn.sigmoid(v)


def _layer_norm(v, g, b):
    mu = jnp.mean(v, axis=-1, keepdims=True)
    vc = v - mu
    var = jnp.mean(vc * vc, axis=-1, keepdims=True)
    return vc * lax.rsqrt(var + LN_EPS) * g + b


def _const_spec(shape):
    return pl.BlockSpec(shape, lambda *_: (0,) * len(shape))


def _in_proj_kernel(x_ref, w_ref, lbf_ref, lbb_ref, rc_ref, rsa_ref, rsb_ref,
                    hq_ref, hv_ref, lff_ref, lfb_ref, hg_ref,
                    aq_ref, ak_ref, av_ref, gh_ref, ga_ref):
    xb = x_ref[...].astype(BF16)

    def proj(seg):
        c0, width = COL_STARTS[seg], COL_SIZES[seg]
        return jnp.dot(xb, w_ref[:, c0:c0 + width], preferred_element_type=F32)

    hq_ref[...] = _silu(proj(0)).astype(BF16)
    hv_ref[...] = proj(1).astype(BF16)
    lb = lbf_ref[...]
    lff_ref[...] = jnp.log(lb + (1.0 - lb) * _sigmoid(proj(2)))
    lb = lbb_ref[...]
    lfb_ref[...] = jnp.log(lb + (1.0 - lb) * _sigmoid(proj(3)))
    hg_ref[...] = _silu(proj(4)).astype(BF16)

    rc, rsa, rsb = rc_ref[...], rsa_ref[...], rsb_ref[...]

    def rotary(z, out_ref, scale):
        for s in range(ATT_WIDTH // LANES):
            t = z[:, s * LANES:(s + 1) * LANES]
            up = pltpu.roll(t, LANES - ROT_DIM // 2, 1)
            dn = pltpu.roll(t, ROT_DIM // 2, 1)
            r = t * rc + up * rsa + dn * rsb
            out_ref[:, s * LANES:(s + 1) * LANES] = (r * scale).astype(BF16)

    rotary(proj(5), aq_ref, ATT_HEAD_DIM ** -0.5)
    rotary(proj(6), ak_ref, 1.0)
    av_ref[...] = proj(7).astype(BF16)
    gh_ref[...] = _sigmoid(proj(8)).astype(BF16)
    ga_ref[...] = _sigmoid(proj(9)).astype(BF16)


def _in_proj(x2d, w_b, lbf, lbb, rc, rsa, rsb, seq_len):
    T = x2d.shape[0]
    tm = 512
    n_pos_blocks = seq_len // tm
    row = lambda w: pl.BlockSpec((tm, w), lambda i: (i, 0))
    tab = pl.BlockSpec((tm, LANES), lambda i: (i % n_pos_blocks, 0))
    out_w = (HG_WIDTH, HG_WIDTH, HG_WIDTH, HG_WIDTH, HG_WIDTH, ATT_WIDTH, ATT_WIDTH, ATT_WIDTH, D_MODEL, D_MODEL)
    out_dt = (BF16, BF16, F32, F32, BF16, BF16, BF16, BF16, BF16, BF16)
    return pl.pallas_call(
        _in_proj_kernel,
        grid=(T // tm,),
        in_specs=[row(D_MODEL),
                  pl.BlockSpec((D_MODEL, IN_COLS), lambda i: (0, 0), pipeline_mode=pl.Buffered(1)),
                  _const_spec((1, HG_WIDTH)), _const_spec((1, HG_WIDTH)), tab, tab, tab],
        out_specs=[row(w) for w in out_w],
        out_shape=[jax.ShapeDtypeStruct((T, w), dt) for w, dt in zip(out_w, out_dt)],
        compiler_params=pltpu.CompilerParams(dimension_semantics=("parallel",), vmem_limit_bytes=VMEM_LIMIT),
        name="in_proj",
    )(x2d, w_b, lbf, lbb, rc, rsa, rsb)


def _split3(a):
    hi = a.astype(BF16)
    r1 = a - hi.astype(F32)
    mid = r1.astype(BF16)
    lo = (r1 - mid.astype(F32)).astype(BF16)
    return jnp.concatenate([hi, mid, lo], axis=1)


HG_SUPER = 4 * HG_CHUNK


def _hgrn_kernel(q_ref, v_ref, lff_ref, lfb_ref, sg_ref, ng_ref, y_ref,
                 of_scr, ob_scr, qf_scr, qb_scr, df_scr, db_scr, mf_scr, mb_scr):
    L = q_ref.shape[1]
    C, SC = HG_CHUNK, HG_SUPER
    n = L // C
    r_i = lax.broadcasted_iota(jnp.int32, (SC, SC), 0)
    c_i = lax.broadcasted_iota(jnp.int32, (SC, SC), 1)
    same = (r_i // C) == (c_i // C)
    r_l, c_l = r_i % C, c_i % C
    mask_f = jnp.logical_and(same, c_l <= r_l)
    mask_b = jnp.logical_and(same, c_l >= r_l)
    tri_f = mask_f.astype(BF16)
    tri_b = mask_b.astype(BF16)

    def chunk_rows(b, row):
        return jnp.concatenate([jnp.broadcast_to(b[c * C + row:c * C + row + 1, :], (C, HG_DIM))
                                for c in range(SC // C)], axis=0)

    dirs = ((lff_ref, tri_f, mask_f, C // 2 - 1, C - 1, of_scr, qf_scr, df_scr, mf_scr),
            (lfb_ref, tri_b, mask_b, C // 2, 0, ob_scr, qb_scr, db_scr, mb_scr))

    def local_body(i, carry):
        i0 = pl.multiple_of(i * SC, SC)
        rows = pl.ds(i0, SC)
        q = q_ref[0, rows, :].astype(F32)
        v = v_ref[0, rows, :]
        lfs = [d[0][0, rows, :] for d in dirs]
        sums = [jnp.dot(d[1], _split3(lf), preferred_element_type=F32) for d, lf in zip(dirs, lfs)]
        a_qs, a_ks, k_outs = [], [], []
        for d, lf, p in zip(dirs, lfs, sums):
            b = p[:, :HG_DIM] + p[:, HG_DIM:2 * HG_DIM] + p[:, 2 * HG_DIM:]
            b_ref, b_last = chunk_rows(b, d[3]), chunk_rows(b, d[4])
            k = 1.0 - jnp.exp(lf)
            a_qs.append((q * jnp.exp(b - b_ref)).astype(BF16))
            a_ks.append((k * jnp.exp(b_ref - b)).astype(BF16))
            k_outs.append((k * jnp.exp(b_last - b)).astype(BF16))
            d[6][rows, :] = (q * jnp.exp(b)).astype(BF16)
            d[7][rows, :] = b_last
        scores = [lax.dot_general(a_q, a_k, NT_DIMS, preferred_element_type=F32) for a_q, a_k in zip(a_qs, a_ks)]
        states = [[lax.dot_general(v[c * C:(c + 1) * C], k_out[c * C:(c + 1) * C], TN_DIMS,
                                   preferred_element_type=F32) for c in range(SC // C)] for k_out in k_outs]
        masked = [jnp.where(d[2], s, 0.0).astype(BF16) for d, s in zip(dirs, scores)]
        intra = [jnp.dot(m, v, preferred_element_type=F32) for m in masked]
        for d, o, st in zip(dirs, intra, states):
            d[5][rows, :] = o
            for c in range(SC // C):
                d[8][i0 // C + c] = st[c]
        return carry

    lax.fori_loop(0, L // SC, local_body, 0)

    def carry_part(c, st, o_scr, qin_scr, dec_scr, m_scr):
        rows = pl.ds(pl.multiple_of(c * C, C), C)
        o_scr[rows, :] = o_scr[rows, :] + lax.dot_general(qin_scr[rows, :], st.astype(BF16), NT_DIMS,
                                                          preferred_element_type=F32)
        return st * jnp.exp(dec_scr[pl.ds(pl.multiple_of(c * C, C), 1), :]) + m_scr[c]

    def carry_body(c, carry):
        st_f, st_b = carry
        st_f = carry_part(c, st_f, of_scr, qf_scr, df_scr, mf_scr)
        st_b = carry_part(n - 1 - c, st_b, ob_scr, qb_scr, db_scr, mb_scr)
        return st_f, st_b

    z = jnp.zeros((HG_DIM, HG_DIM), F32)
    lax.fori_loop(0, n, carry_body, (z, z), unroll=4)

    o = of_scr[...] + ob_scr[...]
    o = o * lax.rsqrt(jnp.mean(o * o, axis=-1, keepdims=True) + LN_EPS) * ng_ref[...]
    y_ref[0] = (o * sg_ref[0].astype(F32)).astype(BF16)


def _hgrn(hq, hv, lff, lfb, hg, norm_g, batch, seq_len):
    blk = pl.BlockSpec((1, seq_len, HG_DIM), lambda b, h: (b, 0, h))
    r3 = lambda a: a.reshape(batch, seq_len, HG_WIDTH)
    per_dir = [pltpu.VMEM((seq_len, HG_DIM), F32), pltpu.VMEM((seq_len, HG_DIM), BF16),
               pltpu.VMEM((seq_len, HG_DIM), F32), pltpu.VMEM((seq_len // HG_CHUNK, HG_DIM, HG_DIM), F32)]
    scratch = [s for pair in zip(per_dir, per_dir) for s in pair]
    return pl.pallas_call(
        _hgrn_kernel,
        grid=(batch, HG_HEADS),
        in_specs=[blk, blk, blk, blk, blk, pl.BlockSpec((1, HG_DIM), lambda b, h: (0, h))],
        out_specs=blk,
        out_shape=jax.ShapeDtypeStruct((batch, seq_len, HG_WIDTH), BF16),
        scratch_shapes=scratch,
        compiler_params=pltpu.CompilerParams(dimension_semantics=("parallel", "parallel"),
                                             vmem_limit_bytes=VMEM_LIMIT),
        name="hgrn",
    )(r3(hq), r3(hv), r3(lff), r3(lfb), r3(hg), norm_g.reshape(1, HG_WIDTH))


def _attn_kernel(q_ref, k_ref, v_ref, o_ref, lse_ref, *scratch, m_len, dil, qb, kw, span):
    lane = lax.broadcasted_iota(jnp.int32, (1, LANES), 1)
    head0 = lane < ATT_HEAD_DIM
    qi_l = lax.broadcasted_iota(jnp.int32, (qb, kw), 0)
    ki_l = lax.broadcasted_iota(jnp.int32, (qb, kw), 1)

    def run_class(load_q, load_k, load_v, store_o, store_l):
        def block(i, carry):
            q0 = pl.multiple_of(i * qb, qb)
            ks = pl.multiple_of(jnp.clip(q0 - span, 0, m_len - kw), span)
            mask = jnp.abs((q0 + qi_l) - (ks + ki_l)) <= span
            slabs = [slice(sl * LANES, (sl + 1) * LANES) for sl in range(ATT_OUT // LANES)]
            heads = (head0, jnp.logical_not(head0))
            qkv = [(load_q(q0, qb, cols), load_k(ks, kw, cols), load_v(ks, kw, cols)) for cols in slabs]
            scores = [lax.dot_general(jnp.where(hm, q2, jnp.zeros_like(q2)), k2, NT_DIMS, preferred_element_type=F32)
                      for (q2, k2, _) in qkv for hm in heads]
            probs, dens, lses = [], [], []
            for s in scores:
                s = jnp.where(mask, s, NEG_INF)
                mx = jnp.max(s, axis=-1, keepdims=True)
                e = jnp.exp(s - mx)
                den = jnp.sum(e, axis=-1, keepdims=True)
                probs.append(e.astype(BF16))
                dens.append(den)
                lses.append(mx + jnp.log(den))
            outs = [jnp.dot(pr, qkv[c // 2][2], preferred_element_type=F32) / dens[c] for c, pr in enumerate(probs)]
            for sl, cols in enumerate(slabs):
                store_o(q0, qb, cols, jnp.where(head0, outs[2 * sl], outs[2 * sl + 1]))
                store_l(q0, qb, cols, jnp.where(head0, lses[2 * sl], lses[2 * sl + 1]))
            return carry

        lax.fori_loop(0, m_len // qb, block, 0)

    def loader(ref):
        return lambda r0, n, cols: ref[0, pl.ds(r0, n), cols]

    if dil == 1:
        def store_o(r0, n, cols, val):
            o_ref[0, pl.ds(r0, n), cols] = val.astype(BF16)

        def store_l(r0, n, cols, val):
            lse_ref[0, pl.ds(r0, n), cols] = val

        run_class(loader(q_ref), loader(k_ref), loader(v_ref), store_o, store_l)
        return

    q32, k32, v32, o32, l32, qc, kc, vc, oc, lc = scratch
    n_slab = ATT_OUT // LANES
    for slab in range(n_slab):
        cols = slice(slab * LANES, (slab + 1) * LANES)
        q32[slab] = q_ref[0, :, cols].astype(F32)
        k32[slab] = k_ref[0, :, cols].astype(F32)
        v32[slab] = v_ref[0, :, cols].astype(F32)

    def cls_loader(ref):
        return lambda r0, n, cols: ref[pl.ds(r0, n), cols]

    def store_oc(r0, n, cols, val):
        oc[pl.ds(r0, n), cols] = val

    def store_lc(r0, n, cols, val):
        lc[pl.ds(r0, n), cols] = val

    for r in range(dil):
        rows = pl.ds(r, m_len, stride=dil)
        for slab in range(n_slab):
            cols = slice(slab * LANES, (slab + 1) * LANES)
            qc[:, cols] = q32[slab, rows, :].astype(BF16)
            kc[:, cols] = k32[slab, rows, :].astype(BF16)
            vc[:, cols] = v32[slab, rows, :].astype(BF16)
        run_class(cls_loader(qc), cls_loader(kc), cls_loader(vc), store_oc, store_lc)
        for slab in range(n_slab):
            cols = slice(slab * LANES, (slab + 1) * LANES)
            o32[slab, rows, :] = oc[:, cols]
            l32[slab, rows, :] = lc[:, cols]
    for slab in range(n_slab):
        cols = slice(slab * LANES, (slab + 1) * LANES)
        o_ref[0, :, cols] = o32[slab].astype(BF16)
        lse_ref[0, :, cols] = l32[slab]


def _attn_group(aq, ak, av, g, batch, seq_len):
    window, dil = DIL_PAIRS[g]
    span = window // (2 * dil)
    m_len = seq_len // dil
    qb = min(2 * span, m_len)
    kw = min(qb + 2 * span, m_len)
    view = lambda a: a.reshape(batch, seq_len, ATT_WIDTH)
    in_blk = pl.BlockSpec((1, seq_len, ATT_OUT), lambda b: (b, 0, g))
    out_blk = pl.BlockSpec((1, seq_len, ATT_OUT), lambda b: (b, 0, 0))
    scratch = []
    if dil > 1:
        scratch = ([pltpu.VMEM((ATT_OUT // LANES, seq_len, LANES), F32)] * 5 + [pltpu.VMEM((m_len, ATT_OUT), BF16)] * 3
                   + [pltpu.VMEM((m_len, ATT_OUT), F32)] * 2)
    o, lse = pl.pallas_call(
        functools.partial(_attn_kernel, m_len=m_len, dil=dil, qb=qb, kw=kw, span=span),
        grid=(batch,),
        in_specs=[in_blk, in_blk, in_blk],
        out_specs=[out_blk, out_blk],
        out_shape=[jax.ShapeDtypeStruct((batch, seq_len, ATT_OUT), BF16),
                   jax.ShapeDtypeStruct((batch, seq_len, ATT_OUT), F32)],
        scratch_shapes=scratch,
        compiler_params=pltpu.CompilerParams(dimension_semantics=("parallel",), vmem_limit_bytes=VMEM_LIMIT),
        name=f"attn_g{g}",
    )(view(aq), view(ak), view(av))
    T = batch * seq_len
    return o.reshape(T, ATT_OUT), lse.reshape(T, ATT_OUT)


def _mix_kernel(x_ref, yh_ref, o0_ref, o1_ref, o2_ref, l0_ref, l1_ref, l2_ref, gh_ref, ga_ref,
                wbh_ref, wba_ref, wo_ref, g1_ref, b1_ref, wrt_ref, rb_ref,
                x1_ref, x1r_ref, tope_ref, gate_ref, rank_ref, cnt_ref, carry_scr, *, alpha):
    tm = x_ref.shape[0]

    @pl.when(pl.program_id(0) == 0)
    def _():
        carry_scr[...] = jnp.zeros_like(carry_scr)

    l0, l1, l2 = l0_ref[...], l1_ref[...], l2_ref[...]
    lm = jnp.maximum(jnp.maximum(l0, l1), l2)
    e0, e1, e2 = jnp.exp(l0 - lm), jnp.exp(l1 - lm), jnp.exp(l2 - lm)
    es = e0 + e1 + e2
    y_a = ((e0 / es) * o0_ref[...].astype(F32) + (e1 / es) * o1_ref[...].astype(F32)
           + (e2 / es) * o2_ref[...].astype(F32)).astype(BF16)
    merged = (gh_ref[...].astype(F32) * jnp.dot(yh_ref[...], wbh_ref[...], preferred_element_type=F32)
              + ga_ref[...].astype(F32) * jnp.dot(y_a, wba_ref[...], preferred_element_type=F32))
    mixed = jnp.dot(merged.astype(BF16), wo_ref[...], preferred_element_type=F32)
    x1 = _layer_norm(alpha * x_ref[...] + mixed, g1_ref[...], b1_ref[...])
    x1_ref[...] = x1
    x1r_ref[:, 0, :] = x1

    logit = lax.dot_general(wrt_ref[...], x1.astype(BF16), NT_DIMS, preferred_element_type=F32)
    score = _sigmoid(logit)
    biased = score + rb_ref[...]
    b3 = biased.reshape(N_GROUPS, PER_GROUP, tm)
    i3 = lax.broadcasted_iota(jnp.int32, (N_GROUPS, PER_GROUP, tm), 1)
    m1 = jnp.max(b3, axis=1, keepdims=True)
    idx1 = jnp.min(jnp.where(b3 == m1, i3, PER_GROUP), axis=1, keepdims=True)
    m2 = jnp.max(jnp.where(i3 == idx1, -jnp.inf, b3), axis=1, keepdims=True)
    gs = (m1 + m2).reshape(N_GROUPS, tm)
    gi = lax.broadcasted_iota(jnp.int32, (N_GROUPS, tm), 0)
    sel = jnp.zeros((N_GROUPS, tm), F32)
    cur = gs
    for _ in range(TOPK_GROUPS):
        m = jnp.max(cur, axis=0, keepdims=True)
        ix = jnp.min(jnp.where(cur == m, gi, N_GROUPS), axis=0, keepdims=True)
        hit = gi == ix
        sel = jnp.where(hit, 1.0, sel)
        cur = jnp.where(hit, -jnp.inf, cur)
    cur = jnp.where(sel.reshape(N_GROUPS, 1, tm) > 0.0, b3, -jnp.inf).reshape(N_EXPERTS, tm)
    ei = lax.broadcasted_iota(jnp.int32, (N_EXPERTS, tm), 0)
    chosen = jnp.zeros((N_EXPERTS, tm), F32)
    top_e, top_s = [], []
    for _ in range(TOP_K):
        m = jnp.max(cur, axis=0, keepdims=True)
        ix = jnp.min(jnp.where(cur == m, ei, N_EXPERTS), axis=0, keepdims=True)
        hit = ei == ix
        top_e.append(ix)
        top_s.append(jnp.sum(jnp.where(hit, score, 0.0), axis=0, keepdims=True))
        chosen = jnp.where(hit, 1.0, chosen)
        cur = jnp.where(hit, -jnp.inf, cur)
    s_sum = top_s[0]
    for s in top_s[1:]:
        s_sum = s_sum + s
    tope_ref[...] = jnp.concatenate(top_e, axis=0)
    gate_ref[...] = jnp.concatenate([s / s_sum * ROUTED_SCALE for s in top_s], axis=0)

    su = lax.broadcasted_iota(jnp.int32, (tm, tm), 0)
    tu = lax.broadcasted_iota(jnp.int32, (tm, tm), 1)
    before = (su < tu).astype(BF16)
    prior = carry_scr[:, 0:1] + jnp.dot(chosen.astype(BF16), before, preferred_element_type=F32)
    ranks = [jnp.sum(jnp.where(ei == ix, prior, 0.0), axis=0, keepdims=True) for ix in top_e]
    rank_ref[...] = jnp.concatenate(ranks, axis=0).astype(jnp.int32)
    carry_scr[...] = carry_scr[...] + jnp.sum(chosen, axis=1, keepdims=True)
    cnt_ref[...] = carry_scr[...]


def _mix(x2d, y_h, att_o, att_l, gh, ga, wbh, wba, wo, g1, b1, wrt, rb, alpha):
    T = x2d.shape[0]
    tm = 256
    row = lambda w: pl.BlockSpec((tm, w), lambda i: (i, 0))
    col = pl.BlockSpec((TOP_K, tm), lambda i: (0, i))
    return pl.pallas_call(
        functools.partial(_mix_kernel, alpha=alpha),
        grid=(T // tm,),
        in_specs=[row(D_MODEL), row(HG_WIDTH)] + [row(ATT_OUT)] * 6 + [row(D_MODEL), row(D_MODEL),
                  _const_spec((HG_WIDTH, D_MODEL)), _const_spec((ATT_OUT, D_MODEL)),
                  _const_spec((D_MODEL, D_MODEL)), _const_spec((1, D_MODEL)), _const_spec((1, D_MODEL)),
                  _const_spec((N_EXPERTS, D_MODEL)), _const_spec((N_EXPERTS, 1))],
        out_specs=[row(D_MODEL), pl.BlockSpec((tm, 1, D_MODEL), lambda i: (i, 0, 0)), col, col, col,
                   _const_spec((N_EXPERTS, LANES))],
        out_shape=[jax.ShapeDtypeStruct((T, D_MODEL), F32),
                   jax.ShapeDtypeStruct((T, 1, D_MODEL), F32),
                   jax.ShapeDtypeStruct((TOP_K, T), jnp.int32),
                   jax.ShapeDtypeStruct((TOP_K, T), F32),
                   jax.ShapeDtypeStruct((TOP_K, T), jnp.int32),
                   jax.ShapeDtypeStruct((N_EXPERTS, LANES), F32)],
        scratch_shapes=[pltpu.VMEM((N_EXPERTS, LANES), F32)],
        compiler_params=pltpu.CompilerParams(dimension_semantics=("arbitrary",), vmem_limit_bytes=VMEM_LIMIT),
        name="mix",
    )(x2d, y_h, *att_o, *att_l, gh, ga, wbh, wba, wo, g1, b1, wrt, rb)


def _dest_kernel(tope_ref, rank_ref, ps_ref, dest_ref):
    tb = tope_ref.shape[1]
    ei = lax.broadcasted_iota(jnp.int32, (N_EXPERTS, tb), 0)
    ps = ps_ref[...]
    starts = [jnp.sum(jnp.where(ei == tope_ref[k:k + 1, :], ps, 0.0), axis=0, keepdims=True)
              for k in range(TOP_K)]
    dest_ref[...] = jnp.concatenate(starts, axis=0).astype(jnp.int32) + rank_ref[...]


def _dest(top_e, rank, padded_start):
    T = top_e.shape[1]
    tb = 512
    col = pl.BlockSpec((TOP_K, tb), lambda i: (0, i))
    return pl.pallas_call(
        _dest_kernel,
        grid=(T // tb,),
        in_specs=[col, col, _const_spec((N_EXPERTS, 1))],
        out_specs=col,
        out_shape=jax.ShapeDtypeStruct((TOP_K, T), jnp.int32),
        compiler_params=pltpu.CompilerParams(dimension_semantics=("parallel",)),
        name="dest",
    )(top_e, rank, padded_start.astype(F32).reshape(N_EXPERTS, 1))


def _dispatch_kernel(dest_ref, x_ref, xs_hbm, sem):
    tb = x_ref.shape[0]

    def issue(g, c):
        t0 = pl.multiple_of(g * SUBLANES, SUBLANES)
        for j in range(SUBLANES):
            for k in range(TOP_K):
                pltpu.make_async_copy(x_ref.at[t0 + j], xs_hbm.at[dest_ref[k, t0 + j]], sem).start(priority=k % 2)
        return c

    lax.fori_loop(0, tb // SUBLANES, issue, 0)
    for _ in range(TOP_K):
        pltpu.make_async_copy(x_ref, xs_hbm.at[pl.ds(0, tb)], sem).wait()


def _dispatch(dest, x1, n_slots):
    T = x1.shape[0]
    tb = 512
    return pl.pallas_call(
        _dispatch_kernel,
        grid=(T // tb,),
        in_specs=[pl.BlockSpec((TOP_K, tb), lambda i: (0, i), memory_space=pltpu.SMEM),
                  pl.BlockSpec((tb, 1, D_MODEL), lambda i: (i, 0, 0))],
        out_specs=pl.BlockSpec(memory_space=pl.ANY),
        out_shape=jax.ShapeDtypeStruct((n_slots, 1, D_MODEL), F32),
        scratch_shapes=[pltpu.SemaphoreType.DMA],
        compiler_params=pltpu.CompilerParams(dimension_semantics=("arbitrary",)),
        name="dispatch",
    )(dest, x1)


N_IN_BUF = 3


def _expert_kernel(be_ref, bv_ref, nu_ref, xs_hbm, wg_ref, wu_ref, wd_ref, ys_hbm,
                   xbuf, ybuf, wg_b, wu_b, wd_b, in_sem, out_sem):
    i = pl.program_id(0)
    n_used = nu_ref[0]
    slot = i % 2
    in_slot = i % N_IN_BUF

    def in_copy(blk, s):
        return pltpu.make_async_copy(xs_hbm.at[pl.ds(blk * SLOT_BLOCK, SLOT_BLOCK), 0, :], xbuf.at[s], in_sem.at[s])

    def out_copy(blk, s):
        return pltpu.make_async_copy(ybuf.at[s], ys_hbm.at[pl.ds(blk * SLOT_BLOCK, SLOT_BLOCK), 0, :], out_sem.at[s])

    @pl.when(i == 0)
    def _():
        for b in range(N_IN_BUF - 1):
            @pl.when(b < n_used)
            def _():
                in_copy(b, b).start()

    @pl.when(i < n_used)
    def _():
        ahead = i + N_IN_BUF - 1

        @pl.when(ahead < n_used)
        def _():
            in_copy(ahead, ahead % N_IN_BUF).start()

        in_copy(i, in_slot).wait()

        prev = be_ref[jnp.maximum(i - 1, 0)]

        @pl.when(jnp.logical_or(i == 0, be_ref[i] != prev))
        def _():
            wg_b[...] = wg_ref[0].astype(BF16)
            wu_b[...] = wu_ref[0].astype(BF16)
            wd_b[...] = wd_ref[0].astype(BF16)

        rows = lax.broadcasted_iota(jnp.int32, (SLOT_BLOCK, 1), 0)
        xb = jnp.where(rows < bv_ref[i], xbuf[in_slot], 0.0).astype(BF16)
        h = _silu(jnp.dot(xb, wg_b[...], preferred_element_type=F32)) * jnp.dot(xb, wu_b[...], preferred_element_type=F32)
        y = jnp.dot(h.astype(BF16), wd_b[...], preferred_element_type=F32)

        @pl.when(i >= 2)
        def _():
            out_copy(i - 2, slot).wait()

        ybuf[slot] = y
        out_copy(i, slot).start(priority=1)

        @pl.when(i == n_used - 1)
        def _():
            @pl.when(i >= 1)
            def _():
                out_copy(i - 1, 1 - slot).wait()

            out_copy(i, slot).wait()


def _experts(block_expert, block_valid, n_used, xs, w_gate, w_up, w_down):
    n_blocks = xs.shape[0] // SLOT_BLOCK
    return pl.pallas_call(
        _expert_kernel,
        grid_spec=pltpu.PrefetchScalarGridSpec(
            num_scalar_prefetch=3,
            grid=(n_blocks,),
            in_specs=[pl.BlockSpec(memory_space=pl.ANY),
                      pl.BlockSpec((1, D_MODEL, EXPERT_HIDDEN), lambda i, be, bv, nu: (be[i], 0, 0)),
                      pl.BlockSpec((1, D_MODEL, EXPERT_HIDDEN), lambda i, be, bv, nu: (be[i], 0, 0)),
                      pl.BlockSpec((1, EXPERT_HIDDEN, D_MODEL), lambda i, be, bv, nu: (be[i], 0, 0))],
            out_specs=pl.BlockSpec(memory_space=pl.ANY),
            scratch_shapes=[pltpu.VMEM((N_IN_BUF, SLOT_BLOCK, D_MODEL), F32),
                            pltpu.VMEM((2, SLOT_BLOCK, D_MODEL), F32),
                            pltpu.VMEM((D_MODEL, EXPERT_HIDDEN), BF16),
                            pltpu.VMEM((D_MODEL, EXPERT_HIDDEN), BF16),
                            pltpu.VMEM((EXPERT_HIDDEN, D_MODEL), BF16),
                            pltpu.SemaphoreType.DMA((N_IN_BUF,)), pltpu.SemaphoreType.DMA((2,))],
        ),
        out_shape=jax.ShapeDtypeStruct(xs.shape, F32),
        compiler_params=pltpu.CompilerParams(dimension_semantics=("arbitrary",), vmem_limit_bytes=VMEM_LIMIT),
        name="experts",
    )(block_expert, block_valid, n_used, xs, w_gate, w_up, w_down)


def _combine_kernel(dest_ref, dnext_ref, x1_ref, gate_ref, p_ref, ys_hbm,
                    wsg_ref, wsu_ref, wsd_ref, g2_ref, b2_ref, wpg_ref, wpp_ref, g3_ref, b3_ref,
                    out_ref, yg_a, yg_b, sem, *, alpha):
    tm = yg_a.shape[1]
    i = pl.program_id(0)
    bufs = (yg_a, yg_b)

    def issue(d_ref, t, k, col0, half):
        pltpu.make_async_copy(ys_hbm.at[d_ref[k, col0 + t]], bufs[half].at[k, pl.ds(t, 1), :],
                              sem.at[half]).start(priority=k % 2)

    def wait_rows(half):
        for k in range(TOP_K):
            pltpu.make_async_copy(ys_hbm.at[pl.ds(0, tm), 0, :], bufs[half].at[k], sem.at[half]).wait()

    def compute(half):
        rows = pl.ds(half * tm, tm)
        x1 = x1_ref[rows, :]
        xb = x1.astype(BF16)
        hs = (_silu(jnp.dot(xb, wsg_ref[...], preferred_element_type=F32))
              * jnp.dot(xb, wsu_ref[...], preferred_element_type=F32))
        ffn = jnp.dot(hs.astype(BF16), wsd_ref[...], preferred_element_type=F32)
        gate = gate_ref[rows, :]
        for k in range(TOP_K):
            ffn = ffn + gate[:, k:k + 1] * bufs[half][k]
        x2 = _layer_norm(alpha * x1 + ffn, g2_ref[...], b2_ref[...])
        ple = (_sigmoid(jnp.dot(x2.astype(BF16), wpg_ref[...], preferred_element_type=F32))
               * jnp.dot(p_ref[rows, :].astype(BF16), wpp_ref[...], preferred_element_type=F32))
        out_ref[rows, :] = _layer_norm(alpha * x2 + ple, g3_ref[...], b3_ref[...])

    @pl.when(i == 0)
    def _():
        def first(g, c):
            t0 = pl.multiple_of(g * SUBLANES, SUBLANES)
            for j in range(SUBLANES):
                for k in range(TOP_K):
                    issue(dest_ref, t0 + j, k, 0, 0)
            return c

        lax.fori_loop(0, tm // SUBLANES, first, 0)

    wait_rows(0)
    for t in range(tm):
        for k in range(TOP_K):
            issue(dest_ref, t, k, tm, 1)
    compute(0)
    wait_rows(1)
    for t in range(tm):
        for k in range(TOP_K):
            issue(dnext_ref, t, k, 0, 0)
    compute(1)

    @pl.when(i == pl.num_programs(0) - 1)
    def _():
        wait_rows(0)


def _combine(dest, x1, gate_t, p2d, ys, wsg, wsu, wsd, g2, b2, wpg, wpp, g3, b3, alpha):
    T = x1.shape[0]
    tm = 256
    n = T // (2 * tm)
    row = lambda w: pl.BlockSpec((2 * tm, w), lambda i: (i, 0))
    return pl.pallas_call(
        functools.partial(_combine_kernel, alpha=alpha),
        grid=(n,),
        in_specs=[pl.BlockSpec((TOP_K, 2 * tm), lambda i: (0, i), memory_space=pltpu.SMEM),
                  pl.BlockSpec((TOP_K, tm), lambda i: (0, jnp.minimum(2 * i + 2, 2 * n - 1)), memory_space=pltpu.SMEM),
                  row(D_MODEL), row(TOP_K), row(PLE_DIM),
                  pl.BlockSpec(memory_space=pl.ANY),
                  _const_spec((D_MODEL, SHARED_HIDDEN)), _const_spec((D_MODEL, SHARED_HIDDEN)),
                  _const_spec((SHARED_HIDDEN, D_MODEL)), _const_spec((1, D_MODEL)), _const_spec((1, D_MODEL)),
                  _const_spec((D_MODEL, D_MODEL)), _const_spec((PLE_DIM, D_MODEL)),
                  _const_spec((1, D_MODEL)), _const_spec((1, D_MODEL))],
        out_specs=row(D_MODEL),
        out_shape=jax.ShapeDtypeStruct((T, D_MODEL), F32),
        scratch_shapes=[pltpu.VMEM((TOP_K, tm, D_MODEL), F32), pltpu.VMEM((TOP_K, tm, D_MODEL), F32),
                        pltpu.SemaphoreType.DMA((2,))],
        compiler_params=pltpu.CompilerParams(dimension_semantics=("arbitrary",), vmem_limit_bytes=VMEM_LIMIT),
        name="combine",
    )(dest, dest, x1, gate_t, p2d, ys, wsg, wsu, wsd, g2, b2, wpg, wpp, g3, b3)


def _rotary_tables(seq_len):
    half = ROT_DIM // 2
    inv_freq = ROPE_THETA ** (-jnp.arange(half, dtype=F32) / half)
    ang = jnp.arange(seq_len, dtype=F32)[:, None] * inv_freq[None, :]
    cos, sin = jnp.cos(ang), jnp.sin(ang)
    pad = ATT_HEAD_DIM - ROT_DIM
    one = jnp.ones((seq_len, pad), F32)
    zero = jnp.zeros((seq_len, pad + half), F32)
    rc = jnp.concatenate([cos, cos, one], axis=1)
    rsa = jnp.concatenate([-sin, zero], axis=1)
    rsb = jnp.concatenate([jnp.zeros((seq_len, half), F32), sin, jnp.zeros((seq_len, pad), F32)], axis=1)
    rep = lambda a: jnp.tile(a, (1, LANES // ATT_HEAD_DIM))
    return rep(rc), rep(rsa), rep(rsb)


def _slot_layout(counts, n_blocks):
    padded = (counts + SLOT_BLOCK - 1) // SLOT_BLOCK * SLOT_BLOCK
    padded_end = jnp.cumsum(padded)
    padded_start = padded_end - padded
    blk0 = jnp.arange(n_blocks, dtype=jnp.int32) * SLOT_BLOCK
    block_expert = jnp.minimum(jnp.sum(blk0[:, None] >= padded_end[None, :], axis=1), N_EXPERTS - 1).astype(jnp.int32)
    n_used = (padded_end[-1] // SLOT_BLOCK).astype(jnp.int32)
    own = block_expert[:, None] == jnp.arange(N_EXPERTS, dtype=jnp.int32)[None, :]
    pick = lambda tab: jnp.sum(jnp.where(own, tab[None, :], 0), axis=1)
    valid = jnp.clip(pick(counts) - (blk0 - pick(padded_start)), 0, SLOT_BLOCK)
    block_valid = jnp.where(jnp.arange(n_blocks) < n_used, valid, 0).astype(jnp.int32)
    return padded_start.astype(jnp.int32), block_expert, block_valid, n_used.reshape(1)


def kernel(x, p, w_in, hg_lb_fwd, hg_lb_bwd, hg_norm_g, w_branch_hg, w_branch_att, w_out, ln1_g, ln1_b, w_router, router_bias, w_exp_gate, w_exp_up, w_exp_down, w_sh_gate, w_sh_up, w_sh_down, ln2_g, ln2_b, w_ple_gate, w_ple_proj, ln3_g, ln3_b):
    B, L, D = x.shape
    depth = w_in.shape[0]
    T = B * L
    alpha = (2 * depth) ** 0.25
    n_assign = T * TOP_K
    n_blocks = (n_assign + N_EXPERTS * (SLOT_BLOCK - 1) + SLOT_BLOCK - 1) // SLOT_BLOCK
    lb_fwd_all = jnp.cumsum(jax.nn.softmax(hg_lb_fwd.astype(F32), axis=0), axis=0)
    lb_bwd_all = jnp.cumsum(jax.nn.softmax(hg_lb_bwd.astype(F32), axis=0), axis=0)
    rc, rsa, rsb = _rotary_tables(L)
    row = lambda a: a.reshape(1, -1)

    x2d = x.reshape(T, D)
    for i in range(depth):
        (hq, hv, lff, lfb, hg, aq, ak, av, gh, ga) = _in_proj(
            x2d, w_in[i].astype(BF16), row(lb_fwd_all[i]), row(lb_bwd_all[i]), rc, rsa, rsb, L)
        y_h = _hgrn(hq, hv, lff, lfb, hg, hg_norm_g[i], B, L).reshape(T, HG_WIDTH)
        att = [_attn_group(aq, ak, av, g, B, L) for g in range(len(DIL_PAIRS))]
        x1, x1r, top_e, gate, rank, cnt = _mix(
            x2d, y_h, [a[0] for a in att], [a[1] for a in att], gh, ga,
            w_branch_hg[i].astype(BF16), w_branch_att[i].astype(BF16), w_out[i].astype(BF16),
            row(ln1_g[i]), row(ln1_b[i]), w_router[i].T.astype(BF16), router_bias[i].reshape(N_EXPERTS, 1), alpha)
        counts = cnt[:, 0].astype(jnp.int32)
        padded_start, block_expert, block_valid, n_used = _slot_layout(counts, n_blocks)
        dest = _dest(top_e, rank, padded_start)
        xs = _dispatch(dest, x1r, n_blocks * SLOT_BLOCK)
        ys = _experts(block_expert, block_valid, n_used, xs, w_exp_gate[i], w_exp_up[i], w_exp_down[i])
        x2d = _combine(dest, x1, gate.T, p[i].reshape(T, PLE_DIM), ys,
                       w_sh_gate[i].astype(BF16), w_sh_up[i].astype(BF16), w_sh_down[i].astype(BF16),
                       row(ln2_g[i]), row(ln2_b[i]), w_ple_gate[i].astype(BF16), w_ple_proj[i].astype(BF16),
                       row(ln3_g[i]), row(ln3_b[i]), alpha)
    return x2d.reshape(B, L, D)
```

```python
import functools

import jax
import jax.numpy as jnp
import numpy as np
from jax import lax
from jax.experimental import pallas as pl
from jax.experimental.pallas import tpu as pltpu

F32 = jnp.float32
BF16 = jnp.bfloat16

D_MODEL = 1024
HG_HEADS = 4
HG_DIM = 128
HG_WIDTH = HG_HEADS * HG_DIM
HG_CHUNK = 64
DIL_PAIRS = ((128, 1), (512, 4), (2048, 16))
ATT_SLOTS = 4
ATT_HEAD_DIM = 64
ATT_WIDTH = len(DIL_PAIRS) * ATT_SLOTS * ATT_HEAD_DIM
ATT_OUT = ATT_SLOTS * ATT_HEAD_DIM
ROT_DIM = ATT_HEAD_DIM // 4
ROPE_THETA = 500000.0
COL_SIZES = (HG_WIDTH,) * 5 + (ATT_WIDTH,) * 3 + (D_MODEL,) * 2
COL_STARTS = tuple(int(v) for v in np.cumsum((0,) + COL_SIZES)[:-1])
IN_COLS = sum(COL_SIZES)
N_EXPERTS = 256
TOP_K = 8
N_GROUPS = 8
TOPK_GROUPS = 4
PER_GROUP = N_EXPERTS // N_GROUPS
EXPERT_HIDDEN = 256
SHARED_HIDDEN = 256
ROUTED_SCALE = 2.5
SLOT_BLOCK = 512
PLE_DIM = 256
LN_EPS = 1e-5
NEG_INF = -1e30

LANES = 128
SUBLANES = 8
VMEM_LIMIT = 56 * 1024 * 1024

NT_DIMS = (((1,), (1,)), ((), ()))
TN_DIMS = (((0,), (0,)), ((), ()))


def _sigmoid(v):
    return jax.nn.sigmoid(v)


def _silu(v):
    return v * jax.nn.sigmoid(v)


def _layer_norm(v, g, b):
    mu = jnp.mean(v, axis=-1, keepdims=True)
    vc = v - mu
    var = jnp.mean(vc * vc, axis=-1, keepdims=True)
    return vc * lax.rsqrt(var + LN_EPS) * g + b


def _const_spec(shape):
    return pl.BlockSpec(shape, lambda *_: (0,) * len(shape))


def _in_proj_kernel(x_ref, w_ref, lbf_ref, lbb_ref, rc_ref, rsa_ref, rsb_ref,
                    hq_ref, hv_ref, lff_ref, lfb_ref, hg_ref,
                    aq_ref, ak_ref, av_ref, gh_ref, ga_ref):
    xb = x_ref[...].astype(BF16)

    def proj(seg):
        c0, width = COL_STARTS[seg], COL_SIZES[seg]
        return jnp.dot(xb, w_ref[:, c0:c0 + width], preferred_element_type=F32)

    hq_ref[...] = _silu(proj(0)).astype(BF16)
    hv_ref[...] = proj(1).astype(BF16)
    lb = lbf_ref[...]
    lff_ref[...] = jnp.log(lb + (1.0 - lb) * _sigmoid(proj(2)))
    lb = lbb_ref[...]
    lfb_ref[...] = jnp.log(lb + (1.0 - lb) * _sigmoid(proj(3)))
    hg_ref[...] = _silu(proj(4)).astype(BF16)

    rc, rsa, rsb = rc_ref[...], rsa_ref[...], rsb_ref[...]

    def rotary(z, out_ref, scale):
        for s in range(ATT_WIDTH // LANES):
            t = z[:, s * LANES:(s + 1) * LANES]
            up = pltpu.roll(t, LANES - ROT_DIM // 2, 1)
            dn = pltpu.roll(t, ROT_DIM // 2, 1)
            r = t * rc + up * rsa + dn * rsb
            out_ref[:, s * LANES:(s + 1) * LANES] = (r * scale).astype(BF16)

    rotary(proj(5), aq_ref, ATT_HEAD_DIM ** -0.5)
    rotary(proj(6), ak_ref, 1.0)
    av_ref[...] = proj(7).astype(BF16)
    gh_ref[...] = _sigmoid(proj(8)).astype(BF16)
    ga_ref[...] = _sigmoid(proj(9)).astype(BF16)


def _in_proj(x2d, w_b, lbf, lbb, rc, rsa, rsb, seq_len):
    T = x2d.shape[0]
    tm = 512
    n_pos_blocks = seq_len // tm
    row = lambda w: pl.BlockSpec((tm, w), lambda i: (i, 0))
    tab = pl.BlockSpec((tm, LANES), lambda i: (i % n_pos_blocks, 0))
    out_w = (HG_WIDTH, HG_WIDTH, HG_WIDTH, HG_WIDTH, HG_WIDTH, ATT_WIDTH, ATT_WIDTH, ATT_WIDTH, D_MODEL, D_MODEL)
    out_dt = (BF16, BF16, F32, F32, BF16, BF16, BF16, BF16, BF16, BF16)
    return pl.pallas_call(
        _in_proj_kernel,
        grid=(T // tm,),
        in_specs=[row(D_MODEL),
                  pl.BlockSpec((D_MODEL, IN_COLS), lambda i: (0, 0), pipeline_mode=pl.Buffered(1)),
                  _const_spec((1, HG_WIDTH)), _const_spec((1, HG_WIDTH)), tab, tab, tab],
        out_specs=[row(w) for w in out_w],
        out_shape=[jax.ShapeDtypeStruct((T, w), dt) for w, dt in zip(out_w, out_dt)],
        compiler_params=pltpu.CompilerParams(dimension_semantics=("parallel",), vmem_limit_bytes=VMEM_LIMIT),
        name="in_proj",
    )(x2d, w_b, lbf, lbb, rc, rsa, rsb)


def _split3(a):
    hi = a.astype(BF16)
    r1 = a - hi.astype(F32)
    mid = r1.astype(BF16)
    lo = (r1 - mid.astype(F32)).astype(BF16)
    return jnp.concatenate([hi, mid, lo], axis=1)


HG_SUPER = 4 * HG_CHUNK


def _hgrn_kernel(q_ref, v_ref, lff_ref, lfb_ref, sg_ref, ng_ref, y_ref,
                 of_scr, ob_scr, qf_scr, qb_scr, df_scr, db_scr, mf_scr, mb_scr):
    L = q_ref.shape[1]
    C, SC = HG_CHUNK, HG_SUPER
    n = L // C
    r_i = lax.broadcasted_iota(jnp.int32, (SC, SC), 0)
    c_i = lax.broadcasted_iota(jnp.int32, (SC, SC), 1)
    same = (r_i // C) == (c_i // C)
    r_l, c_l = r_i % C, c_i % C
    mask_f = jnp.logical_and(same, c_l <= r_l)
    mask_b = jnp.logical_and(same, c_l >= r_l)
    tri_f = mask_f.astype(BF16)
    tri_b = mask_b.astype(BF16)

    def chunk_rows(b, row):
        return jnp.concatenate([jnp.broadcast_to(b[c * C + row:c * C + row + 1, :], (C, HG_DIM))
                                for c in range(SC // C)], axis=0)

    dirs = ((lff_ref, tri_f, mask_f, C // 2 - 1, C - 1, of_scr, qf_scr, df_scr, mf_scr),
            (lfb_ref, tri_b, mask_b, C // 2, 0, ob_scr, qb_scr, db_scr, mb_scr))

    def local_body(i, carry):
        i0 = pl.multiple_of(i * SC, SC)
        rows = pl.ds(i0, SC)
        q = q_ref[0, rows, :].astype(F32)
        v = v_ref[0, rows, :]
        lfs = [d[0][0, rows, :] for d in dirs]
        sums = [jnp.dot(d[1], _split3(lf), preferred_element_type=F32) for d, lf in zip(dirs, lfs)]
        a_qs, a_ks, k_outs = [], [], []
        for d, lf, p in zip(dirs, lfs, sums):
            b = p[:, :HG_DIM] + p[:, HG_DIM:2 * HG_DIM] + p[:, 2 * HG_DIM:]
            b_ref, b_last = chunk_rows(b, d[3]), chunk_rows(b, d[4])
            k = 1.0 - jnp.exp(lf)
            a_qs.append((q * jnp.exp(b - b_ref)).astype(BF16))
            a_ks.append((k * jnp.exp(b_ref - b)).astype(BF16))
            k_outs.append((k * jnp.exp(b_last - b)).astype(BF16))
            d[6][rows, :] = (q * jnp.exp(b)).astype(BF16)
            d[7][rows, :] = b_last
        scores = [lax.dot_general(a_q, a_k, NT_DIMS, preferred_element_type=F32) for a_q, a_k in zip(a_qs, a_ks)]
        states = [[lax.dot_general(v[c * C:(c + 1) * C], k_out[c * C:(c + 1) * C], TN_DIMS,
                                   preferred_element_type=F32) for c in range(SC // C)] for k_out in k_outs]
        masked = [jnp.where(d[2], s, 0.0).astype(BF16) for d, s in zip(dirs, scores)]
        intra = [jnp.dot(m, v, preferred_element_type=F32) for m in masked]
        for d, o, st in zip(dirs, intra, states):
            d[5][rows, :] = o
            for c in range(SC // C):
                d[8][i0 // C + c] = st[c]
        return carry

    lax.fori_loop(0, L // SC, local_body, 0)

    def carry_part(c, st, o_scr, qin_scr, dec_scr, m_scr):
        rows = pl.ds(pl.multiple_of(c * C, C), C)
        o_scr[rows, :] = o_scr[rows, :] + lax.dot_general(qin_scr[rows, :], st.astype(BF16), NT_DIMS,
                                                          preferred_element_type=F32)
        return st * jnp.exp(dec_scr[pl.ds(pl.multiple_of(c * C, C), 1), :]) + m_scr[c]

    def carry_body(c, carry):
        st_f, st_b = carry
        st_f = carry_part(c, st_f, of_scr, qf_scr, df_scr, mf_scr)
        st_b = carry_part(n - 1 - c, st_b, ob_scr, qb_scr, db_scr, mb_scr)
        return st_f, st_b

    z = jnp.zeros((HG_DIM, HG_DIM), F32)
    lax.fori_loop(0, n, carry_body, (z, z), unroll=4)

    o = of_scr[...] + ob_scr[...]
    o = o * lax.rsqrt(jnp.mean(o * o, axis=-1, keepdims=True) + LN_EPS) * ng_ref[...]
    y_ref[0] = (o * sg_ref[0].astype(F32)).astype(BF16)


def _hgrn(hq, hv, lff, lfb, hg, norm_g, batch, seq_len):
    blk = pl.BlockSpec((1, seq_len, HG_DIM), lambda b, h: (b, 0, h))
    r3 = lambda a: a.reshape(batch, seq_len, HG_WIDTH)
    per_dir = [pltpu.VMEM((seq_len, HG_DIM), F32), pltpu.VMEM((seq_len, HG_DIM), BF16),
               pltpu.VMEM((seq_len, HG_DIM), F32), pltpu.VMEM((seq_len // HG_CHUNK, HG_DIM, HG_DIM), F32)]
    scratch = [s for pair in zip(per_dir, per_dir) for s in pair]
    return pl.pallas_call(
        _hgrn_kernel,
        grid=(batch, HG_HEADS),
        in_specs=[blk, blk, blk, blk, blk, pl.BlockSpec((1, HG_DIM), lambda b, h: (0, h))],
        out_specs=blk,
        out_shape=jax.ShapeDtypeStruct((batch, seq_len, HG_WIDTH), BF16),
        scratch_shapes=scratch,
        compiler_params=pltpu.CompilerParams(dimension_semantics=("parallel", "parallel"),
                                             vmem_limit_bytes=VMEM_LIMIT),
        name="hgrn",
    )(r3(hq), r3(hv), r3(lff), r3(lfb), r3(hg), norm_g.reshape(1, HG_WIDTH))


def _attn_kernel(q_ref, k_ref, v_ref, o_ref, lse_ref, *scratch, m_len, dil, qb, kw, span):
    lane = lax.broadcasted_iota(jnp.int32, (1, LANES), 1)
    head0 = lane < ATT_HEAD_DIM
    qi_l = lax.broadcasted_iota(jnp.int32, (qb, kw), 0)
    ki_l = lax.broadcasted_iota(jnp.int32, (qb, kw), 1)

    def run_class(load_q, load_k, load_v, store_o, store_l):
        def block(i, carry):
            q0 = pl.multiple_of(i * qb, qb)
            ks = pl.multiple_of(jnp.clip(q0 - span, 0, m_len - kw), span)
            mask = jnp.abs((q0 + qi_l) - (ks + ki_l)) <= span
            slabs = [slice(sl * LANES, (sl + 1) * LANES) for sl in range(ATT_OUT // LANES)]
            heads = (head0, jnp.logical_not(head0))
            qkv = [(load_q(q0, qb, cols), load_k(ks, kw, cols), load_v(ks, kw, cols)) for cols in slabs]
            scores = [lax.dot_general(jnp.where(hm, q2, jnp.zeros_like(q2)), k2, NT_DIMS, preferred_element_type=F32)
                      for (q2, k2, _) in qkv for hm in heads]
            probs, dens, lses = [], [], []
            for s in scores:
                s = jnp.where(mask, s, NEG_INF)
                mx = jnp.max(s, axis=-1, keepdims=True)
                e = jnp.exp(s - mx)
                den = jnp.sum(e, axis=-1, keepdims=True)
                probs.append(e.astype(BF16))
                dens.append(den)
                lses.append(mx + jnp.log(den))
            outs = [jnp.dot(pr, qkv[c // 2][2], preferred_element_type=F32) / dens[c] for c, pr in enumerate(probs)]
            for sl, cols in enumerate(slabs):
                store_o(q0, qb, cols, jnp.where(head0, outs[2 * sl], outs[2 * sl + 1]))
                store_l(q0, qb, cols, jnp.where(head0, lses[2 * sl], lses[2 * sl + 1]))
            return carry

        lax.fori_loop(0, m_len // qb, block, 0)

    def loader(ref):
        return lambda r0, n, cols: ref[0, pl.ds(r0, n), cols]

    if dil == 1:
        def store_o(r0, n, cols, val):
            o_ref[0, pl.ds(r0, n), cols] = val.astype(BF16)

        def store_l(r0, n, cols, val):
            lse_ref[0, pl.ds(r0, n), cols] = val

        run_class(loader(q_ref), loader(k_ref), loader(v_ref), store_o, store_l)
        return

    q32, k32, v32, o32, l32, qc, kc, vc, oc, lc = scratch
    n_slab = ATT_OUT // LANES
    for slab in range(n_slab):
        cols = slice(slab * LANES, (slab + 1) * LANES)
        q32[slab] = q_ref[0, :, cols].astype(F32)
        k32[slab] = k_ref[0, :, cols].astype(F32)
        v32[slab] = v_ref[0, :, cols].astype(F32)

    def cls_loader(ref):
        return lambda r0, n, cols: ref[pl.ds(r0, n), cols]

    def store_oc(r0, n, cols, val):
        oc[pl.ds(r0, n), cols] = val

    def store_lc(r0, n, cols, val):
        lc[pl.ds(r0, n), cols] = val

    for r in range(dil):
        rows = pl.ds(r, m_len, stride=dil)
        for slab in range(n_slab):
            cols = slice(slab * LANES, (slab + 1) * LANES)
            qc[:, cols] = q32[slab, rows, :].astype(BF16)
            kc[:, cols] = k32[slab, rows, :].astype(BF16)
            vc[:, cols] = v32[slab, rows, :].astype(BF16)
        run_class(cls_loader(qc), cls_loader(kc), cls_loader(vc), store_oc, store_lc)
        for slab in range(n_slab):
            cols = slice(slab * LANES, (slab + 1) * LANES)
            o32[slab, rows, :] = oc[:, cols]
            l32[slab, rows, :] = lc[:, cols]
    for slab in range(n_slab):
        cols = slice(slab * LANES, (slab + 1) * LANES)
        o_ref[0, :, cols] = o32[slab].astype(BF16)
        lse_ref[0, :, cols] = l32[slab]


def _attn_group(aq, ak, av, g, batch, seq_len):
    window, dil = DIL_PAIRS[g]
    span = window // (2 * dil)
    m_len = seq_len // dil
    qb = min(2 * span, m_len)
    kw = min(qb + 2 * span, m_len)
    view = lambda a: a.reshape(batch, seq_len, ATT_WIDTH)
    in_blk = pl.BlockSpec((1, seq_len, ATT_OUT), lambda b: (b, 0, g))
    out_blk = pl.BlockSpec((1, seq_len, ATT_OUT), lambda b: (b, 0, 0))
    scratch = []
    if dil > 1:
        scratch = ([pltpu.VMEM((ATT_OUT // LANES, seq_len, LANES), F32)] * 5 + [pltpu.VMEM((m_len, ATT_OUT), BF16)] * 3
                   + [pltpu.VMEM((m_len, ATT_OUT), F32)] * 2)
    o, lse = pl.pallas_call(
        functools.partial(_attn_kernel, m_len=m_len, dil=dil, qb=qb, kw=kw, span=span),
        grid=(batch,),
        in_specs=[in_blk, in_blk, in_blk],
        out_specs=[out_blk, out_blk],
        out_shape=[jax.ShapeDtypeStruct((batch, seq_len, ATT_OUT), BF16),
                   jax.ShapeDtypeStruct((batch, seq_len, ATT_OUT), F32)],
        scratch_shapes=scratch,
        compiler_params=pltpu.CompilerParams(dimension_semantics=("parallel",), vmem_limit_bytes=VMEM_LIMIT),
        name=f"attn_g{g}",
    )(view(aq), view(ak), view(av))
    T = batch * seq_len
    return o.reshape(T, ATT_OUT), lse.reshape(T, ATT_OUT)


def _mix_kernel(x_ref, yh_ref, o0_ref, o1_ref, o2_ref, l0_ref, l1_ref, l2_ref, gh_ref, ga_ref,
                wbh_ref, wba_ref, wo_ref, g1_ref, b1_ref, wrt_ref, rb_ref,
                x1_ref, x1r_ref, tope_ref, gate_ref, rank_ref, cnt_ref, carry_scr, *, alpha):
    tm = x_ref.shape[0]

    @pl.when(pl.program_id(0) == 0)
    def _():
        carry_scr[...] = jnp.zeros_like(carry_scr)

    l0, l1, l2 = l0_ref[...], l1_ref[...], l2_ref[...]
    lm = jnp.maximum(jnp.maximum(l0, l1), l2)
    e0, e1, e2 = jnp.exp(l0 - lm), jnp.exp(l1 - lm), jnp.exp(l2 - lm)
    es = e0 + e1 + e2
    y_a = ((e0 / es) * o0_ref[...].astype(F32) + (e1 / es) * o1_ref[...].astype(F32)
           + (e2 / es) * o2_ref[...].astype(F32)).astype(BF16)
    merged = (gh_ref[...].astype(F32) * jnp.dot(yh_ref[...], wbh_ref[...], preferred_element_type=F32)
              + ga_ref[...].astype(F32) * jnp.dot(y_a, wba_ref[...], preferred_element_type=F32))
    mixed = jnp.dot(merged.astype(BF16), wo_ref[...], preferred_element_type=F32)
    x1 = _layer_norm(alpha * x_ref[...] + mixed, g1_ref[...], b1_ref[...])
    x1_ref[...] = x1
    x1r_ref[:, 0, :] = x1

    logit = lax.dot_general(wrt_ref[...], x1.astype(BF16), NT_DIMS, preferred_element_type=F32)
    score = _sigmoid(logit)
    biased = score + rb_ref[...]
    b3 = biased.reshape(N_GROUPS, PER_GROUP, tm)
    i3 = lax.broadcasted_iota(jnp.int32, (N_GROUPS, PER_GROUP, tm), 1)
    m1 = jnp.max(b3, axis=1, keepdims=True)
    idx1 = jnp.min(jnp.where(b3 == m1, i3, PER_GROUP), axis=1, keepdims=True)
    m2 = jnp.max(jnp.where(i3 == idx1, -jnp.inf, b3), axis=1, keepdims=True)
    gs = (m1 + m2).reshape(N_GROUPS, tm)
    gi = lax.broadcasted_iota(jnp.int32, (N_GROUPS, tm), 0)
    sel = jnp.zeros((N_GROUPS, tm), F32)
    cur = gs
    for _ in range(TOPK_GROUPS):
        m = jnp.max(cur, axis=0, keepdims=True)
        ix = jnp.min(jnp.where(cur == m, gi, N_GROUPS), axis=0, keepdims=True)
        hit = gi == ix
        sel = jnp.where(hit, 1.0, sel)
        cur = jnp.where(hit, -jnp.inf, cur)
    cur = jnp.where(sel.reshape(N_GROUPS, 1, tm) > 0.0, b3, -jnp.inf).reshape(N_EXPERTS, tm)
    ei = lax.broadcasted_iota(jnp.int32, (N_EXPERTS, tm), 0)
    chosen = jnp.zeros((N_EXPERTS, tm), F32)
    top_e, top_s = [], []
    for _ in range(TOP_K):
        m = jnp.max(cur, axis=0, keepdims=True)
        ix = jnp.min(jnp.where(cur == m, ei, N_EXPERTS), axis=0, keepdims=True)
        hit = ei == ix
        top_e.append(ix)
        top_s.append(jnp.sum(jnp.where(hit, score, 0.0), axis=0, keepdims=True))
        chosen = jnp.where(hit, 1.0, chosen)
        cur = jnp.where(hit, -jnp.inf, cur)
    s_sum = top_s[0]
    for s in top_s[1:]:
        s_sum = s_sum + s
    tope_ref[...] = jnp.concatenate(top_e, axis=0)
    gate_ref[...] = jnp.concatenate([s / s_sum * ROUTED_SCALE for s in top_s], axis=0)

    su = lax.broadcasted_iota(jnp.int32, (tm, tm), 0)
    tu = lax.broadcasted_iota(jnp.int32, (tm, tm), 1)
    before = (su < tu).astype(BF16)
    prior = carry_scr[:, 0:1] + jnp.dot(chosen.astype(BF16), before, preferred_element_type=F32)
    ranks = [jnp.sum(jnp.where(ei == ix, prior, 0.0), axis=0, keepdims=True) for ix in top_e]
    rank_ref[...] = jnp.concatenate(ranks, axis=0).astype(jnp.int32)
    carry_scr[...] = carry_scr[...] + jnp.sum(chosen, axis=1, keepdims=True)
    cnt_ref[...] = carry_scr[...]


def _mix(x2d, y_h, att_o, att_l, gh, ga, wbh, wba, wo, g1, b1, wrt, rb, alpha):
    T = x2d.shape[0]
    tm = 256
    row = lambda w: pl.BlockSpec((tm, w), lambda i: (i, 0))
    col = pl.BlockSpec((TOP_K, tm), lambda i: (0, i))
    return pl.pallas_call(
        functools.partial(_mix_kernel, alpha=alpha),
        grid=(T // tm,),
        in_specs=[row(D_MODEL), row(HG_WIDTH)] + [row(ATT_OUT)] * 6 + [row(D_MODEL), row(D_MODEL),
                  _const_spec((HG_WIDTH, D_MODEL)), _const_spec((ATT_OUT, D_MODEL)),
                  _const_spec((D_MODEL, D_MODEL)), _const_spec((1, D_MODEL)), _const_spec((1, D_MODEL)),
                  _const_spec((N_EXPERTS, D_MODEL)), _const_spec((N_EXPERTS, 1))],
        out_specs=[row(D_MODEL), pl.BlockSpec((tm, 1, D_MODEL), lambda i: (i, 0, 0)), col, col, col,
                   _const_spec((N_EXPERTS, LANES))],
        out_shape=[jax.ShapeDtypeStruct((T, D_MODEL), F32),
                   jax.ShapeDtypeStruct((T, 1, D_MODEL), F32),
                   jax.ShapeDtypeStruct((TOP_K, T), jnp.int32),
                   jax.ShapeDtypeStruct((TOP_K, T), F32),
                   jax.ShapeDtypeStruct((TOP_K, T), jnp.int32),
                   jax.ShapeDtypeStruct((N_EXPERTS, LANES), F32)],
        scratch_shapes=[pltpu.VMEM((N_EXPERTS, LANES), F32)],
        compiler_params=pltpu.CompilerParams(dimension_semantics=("arbitrary",), vmem_limit_bytes=VMEM_LIMIT),
        name="mix",
    )(x2d, y_h, *att_o, *att_l, gh, ga, wbh, wba, wo, g1, b1, wrt, rb)


def _dest_kernel(tope_ref, rank_ref, ps_ref, dest_ref):
    tb = tope_ref.shape[1]
    ei = lax.broadcasted_iota(jnp.int32, (N_EXPERTS, tb), 0)
    ps = ps_ref[...]
    starts = [jnp.sum(jnp.where(ei == tope_ref[k:k + 1, :], ps, 0.0), axis=0, keepdims=True)
              for k in range(TOP_K)]
    dest_ref[...] = jnp.concatenate(starts, axis=0).astype(jnp.int32) + rank_ref[...]


def _dest(top_e, rank, padded_start):
    T = top_e.shape[1]
    tb = 512
    col = pl.BlockSpec((TOP_K, tb), lambda i: (0, i))
    return pl.pallas_call(
        _dest_kernel,
        grid=(T // tb,),
        in_specs=[col, col, _const_spec((N_EXPERTS, 1))],
        out_specs=col,
        out_shape=jax.ShapeDtypeStruct((TOP_K, T), jnp.int32),
        compiler_params=pltpu.CompilerParams(dimension_semantics=("parallel",)),
        name="dest",
    )(top_e, rank, padded_start.astype(F32).reshape(N_EXPERTS, 1))


def _dispatch_kernel(dest_ref, x_ref, xs_hbm, sem):
    tb = x_ref.shape[0]

    def issue(g, c):
        t0 = pl.multiple_of(g * SUBLANES, SUBLANES)
        for j in range(SUBLANES):
            for k in range(TOP_K):
                pltpu.make_async_copy(x_ref.at[t0 + j], xs_hbm.at[dest_ref[k, t0 + j]], sem).start(priority=k % 2)
        return c

    lax.fori_loop(0, tb // SUBLANES, issue, 0)
    for _ in range(TOP_K):
        pltpu.make_async_copy(x_ref, xs_hbm.at[pl.ds(0, tb)], sem).wait()


def _dispatch(dest, x1, n_slots):
    T = x1.shape[0]
    tb = 512
    return pl.pallas_call(
        _dispatch_kernel,
        grid=(T // tb,),
        in_specs=[pl.BlockSpec((TOP_K, tb), lambda i: (0, i), memory_space=pltpu.SMEM),
                  pl.BlockSpec((tb, 1, D_MODEL), lambda i: (i, 0, 0))],
        out_specs=pl.BlockSpec(memory_space=pl.ANY),
        out_shape=jax.ShapeDtypeStruct((n_slots, 1, D_MODEL), F32),
        scratch_shapes=[pltpu.SemaphoreType.DMA],
        compiler_params=pltpu.CompilerParams(dimension_semantics=("arbitrary",)),
        name="dispatch",
    )(dest, x1)


N_IN_BUF = 3


def _expert_kernel(be_ref, bv_ref, nu_ref, xs_hbm, wg_ref, wu_ref, wd_ref, ys_hbm,
                   xbuf, ybuf, wg_b, wu_b, wd_b, in_sem, out_sem):
    i = pl.program_id(0)
    n_used = nu_ref[0]
    slot = i % 2
    in_slot = i % N_IN_BUF

    def in_copy(blk, s):
        return pltpu.make_async_copy(xs_hbm.at[pl.ds(blk * SLOT_BLOCK, SLOT_BLOCK), 0, :], xbuf.at[s], in_sem.at[s])

    def out_copy(blk, s):
        return pltpu.make_async_copy(ybuf.at[s], ys_hbm.at[pl.ds(blk * SLOT_BLOCK, SLOT_BLOCK), 0, :], out_sem.at[s])

    @pl.when(i == 0)
    def _():
        for b in range(N_IN_BUF - 1):
            @pl.when(b < n_used)
            def _():
                in_copy(b, b).start()

    @pl.when(i < n_used)
    def _():
        ahead = i + N_IN_BUF - 1

        @pl.when(ahead < n_used)
        def _():
            in_copy(ahead, ahead % N_IN_BUF).start()

        in_copy(i, in_slot).wait()

        prev = be_ref[jnp.maximum(i - 1, 0)]

        @pl.when(jnp.logical_or(i == 0, be_ref[i] != prev))
        def _():
            wg_b[...] = wg_ref[0].astype(BF16)
            wu_b[...] = wu_ref[0].astype(BF16)
            wd_b[...] = wd_ref[0].astype(BF16)

        rows = lax.broadcasted_iota(jnp.int32, (SLOT_BLOCK, 1), 0)
        xb = jnp.where(rows < bv_ref[i], xbuf[in_slot], 0.0).astype(BF16)
        h = _silu(jnp.dot(xb, wg_b[...], preferred_element_type=F32)) * jnp.dot(xb, wu_b[...], preferred_element_type=F32)
        y = jnp.dot(h.astype(BF16), wd_b[...], preferred_element_type=F32)

        @pl.when(i >= 2)
        def _():
            out_copy(i - 2, slot).wait()

        ybuf[slot] = y
        out_copy(i, slot).start(priority=1)

        @pl.when(i == n_used - 1)
        def _():
            @pl.when(i >= 1)
            def _():
                out_copy(i - 1, 1 - slot).wait()

            out_copy(i, slot).wait()


def _experts(block_expert, block_valid, n_used, xs, w_gate, w_up, w_down):
    n_blocks = xs.shape[0] // SLOT_BLOCK
    return pl.pallas_call(
        _expert_kernel,
        grid_spec=pltpu.PrefetchScalarGridSpec(
            num_scalar_prefetch=3,
            grid=(n_blocks,),
            in_specs=[pl.BlockSpec(memory_space=pl.ANY),
                      pl.BlockSpec((1, D_MODEL, EXPERT_HIDDEN), lambda i, be, bv, nu: (be[i], 0, 0)),
                      pl.BlockSpec((1, D_MODEL, EXPERT_HIDDEN), lambda i, be, bv, nu: (be[i], 0, 0)),
                      pl.BlockSpec((1, EXPERT_HIDDEN, D_MODEL), lambda i, be, bv, nu: (be[i], 0, 0))],
            out_specs=pl.BlockSpec(memory_space=pl.ANY),
            scratch_shapes=[pltpu.VMEM((N_IN_BUF, SLOT_BLOCK, D_MODEL), F32),
                            pltpu.VMEM((2, SLOT_BLOCK, D_MODEL), F32),
                            pltpu.VMEM((D_MODEL, EXPERT_HIDDEN), BF16),
                            pltpu.VMEM((D_MODEL, EXPERT_HIDDEN), BF16),
                            pltpu.VMEM((EXPERT_HIDDEN, D_MODEL), BF16),
                            pltpu.SemaphoreType.DMA((N_IN_BUF,)), pltpu.SemaphoreType.DMA((2,))],
        ),
        out_shape=jax.ShapeDtypeStruct(xs.shape, F32),
        compiler_params=pltpu.CompilerParams(dimension_semantics=("arbitrary",), vmem_limit_bytes=VMEM_LIMIT),
        name="experts",
    )(block_expert, block_valid, n_used, xs, w_gate, w_up, w_down)


def _combine_kernel(dest_ref, dnext_ref, x1_ref, gate_ref, p_ref, ys_hbm,
                    wsg_ref, wsu_ref, wsd_ref, g2_ref, b2_ref, wpg_ref, wpp_ref, g3_ref, b3_ref,
                    out_ref, *scratch, alpha):
    sem = scratch[-1]
    bufs = (scratch[:TOP_K], scratch[TOP_K:2 * TOP_K])
    tm = bufs[0][0].shape[0]
    i = pl.program_id(0)

    def issue(d_ref, t, k, col0, half):
        pltpu.make_async_copy(ys_hbm.at[d_ref[k, col0 + t]], bufs[half][k].at[t], sem.at[half]).start(priority=k % 2)

    def wait_rows(half):
        for k in range(TOP_K):
            pltpu.make_async_copy(ys_hbm.at[pl.ds(0, tm)], bufs[half][k], sem.at[half]).wait()

    def compute(half):
        rows = pl.ds(half * tm, tm)
        x1 = x1_ref[rows, :]
        xb = x1.astype(BF16)
        hs = (_silu(jnp.dot(xb, wsg_ref[...], preferred_element_type=F32))
              * jnp.dot(xb, wsu_ref[...], preferred_element_type=F32))
        ffn = jnp.dot(hs.astype(BF16), wsd_ref[...], preferred_element_type=F32)
        gate = gate_ref[rows, :]
        for k in range(TOP_K):
            ffn = ffn + gate[:, k:k + 1] * bufs[half][k][:, 0, :]
        x2 = _layer_norm(alpha * x1 + ffn, g2_ref[...], b2_ref[...])
        ple = (_sigmoid(jnp.dot(x2.astype(BF16), wpg_ref[...], preferred_element_type=F32))
               * jnp.dot(p_ref[rows, :].astype(BF16), wpp_ref[...], preferred_element_type=F32))
        out_ref[rows, :] = _layer_norm(alpha * x2 + ple, g3_ref[...], b3_ref[...])

    @pl.when(i == 0)
    def _():
        def first(g, c):
            t0 = pl.multiple_of(g * SUBLANES, SUBLANES)
            for j in range(SUBLANES):
                for k in range(TOP_K):
                    issue(dest_ref, t0 + j, k, 0, 0)
            return c

        lax.fori_loop(0, tm // SUBLANES, first, 0)

    wait_rows(0)
    for t in range(tm):
        for k in range(TOP_K):
            issue(dest_ref, t, k, tm, 1)
    compute(0)
    wait_rows(1)
    for t in range(tm):
        for k in range(TOP_K):
            issue(dnext_ref, t, k, 0, 0)
    compute(1)

    @pl.when(i == pl.num_programs(0) - 1)
    def _():
        wait_rows(0)


def _combine(dest, x1, gate_t, p2d, ys, wsg, wsu, wsd, g2, b2, wpg, wpp, g3, b3, alpha):
    T = x1.shape[0]
    tm = 256
    n = T // (2 * tm)
    row = lambda w: pl.BlockSpec((2 * tm, w), lambda i: (i, 0))
    return pl.pallas_call(
        functools.partial(_combine_kernel, alpha=alpha),
        grid=(n,),
        in_specs=[pl.BlockSpec((TOP_K, 2 * tm), lambda i: (0, i), memory_space=pltpu.SMEM),
                  pl.BlockSpec((TOP_K, tm), lambda i: (0, jnp.minimum(2 * i + 2, 2 * n - 1)), memory_space=pltpu.SMEM),
                  row(D_MODEL), row(TOP_K), row(PLE_DIM),
                  pl.BlockSpec(memory_space=pl.ANY),
                  _const_spec((D_MODEL, SHARED_HIDDEN)), _const_spec((D_MODEL, SHARED_HIDDEN)),
                  _const_spec((SHARED_HIDDEN, D_MODEL)), _const_spec((1, D_MODEL)), _const_spec((1, D_MODEL)),
                  _const_spec((D_MODEL, D_MODEL)), _const_spec((PLE_DIM, D_MODEL)),
                  _const_spec((1, D_MODEL)), _const_spec((1, D_MODEL))],
        out_specs=row(D_MODEL),
        out_shape=jax.ShapeDtypeStruct((T, D_MODEL), F32),
        scratch_shapes=[pltpu.VMEM((tm, 1, D_MODEL), F32)] * (2 * TOP_K) + [pltpu.SemaphoreType.DMA((2,))],
        compiler_params=pltpu.CompilerParams(dimension_semantics=("arbitrary",), vmem_limit_bytes=VMEM_LIMIT),
        name="combine",
    )(dest, dest, x1, gate_t, p2d, ys, wsg, wsu, wsd, g2, b2, wpg, wpp, g3, b3)


def _rotary_tables(seq_len):
    half = ROT_DIM // 2
    inv_freq = ROPE_THETA ** (-jnp.arange(half, dtype=F32) / half)
    ang = jnp.arange(seq_len, dtype=F32)[:, None] * inv_freq[None, :]
    cos, sin = jnp.cos(ang), jnp.sin(ang)
    pad = ATT_HEAD_DIM - ROT_DIM
    one = jnp.ones((seq_len, pad), F32)
    zero = jnp.zeros((seq_len, pad + half), F32)
    rc = jnp.concatenate([cos, cos, one], axis=1)
    rsa = jnp.concatenate([-sin, zero], axis=1)
    rsb = jnp.concatenate([jnp.zeros((seq_len, half), F32), sin, jnp.zeros((seq_len, pad), F32)], axis=1)
    rep = lambda a: jnp.tile(a, (1, LANES // ATT_HEAD_DIM))
    return rep(rc), rep(rsa), rep(rsb)


def _slot_layout(counts, n_blocks):
    padded = (counts + SLOT_BLOCK - 1) // SLOT_BLOCK * SLOT_BLOCK
    padded_end = jnp.cumsum(padded)
    padded_start = padded_end - padded
    blk0 = jnp.arange(n_blocks, dtype=jnp.int32) * SLOT_BLOCK
    block_expert = jnp.minimum(jnp.sum(blk0[:, None] >= padded_end[None, :], axis=1), N_EXPERTS - 1).astype(jnp.int32)
    n_used = (padded_end[-1] // SLOT_BLOCK).astype(jnp.int32)
    own = block_expert[:, None] == jnp.arange(N_EXPERTS, dtype=jnp.int32)[None, :]
    pick = lambda tab: jnp.sum(jnp.where(own, tab[None, :], 0), axis=1)
    valid = jnp.clip(pick(counts) - (blk0 - pick(padded_start)), 0, SLOT_BLOCK)
    block_valid = jnp.where(jnp.arange(n_blocks) < n_used, valid, 0).astype(jnp.int32)
    return padded_start.astype(jnp.int32), block_expert, block_valid, n_used.reshape(1)


def kernel(x, p, w_in, hg_lb_fwd, hg_lb_bwd, hg_norm_g, w_branch_hg, w_branch_att, w_out, ln1_g, ln1_b, w_router, router_bias, w_exp_gate, w_exp_up, w_exp_down, w_sh_gate, w_sh_up, w_sh_down, ln2_g, ln2_b, w_ple_gate, w_ple_proj, ln3_g, ln3_b):
    B, L, D = x.shape
    depth = w_in.shape[0]
    T = B * L
    alpha = (2 * depth) ** 0.25
    n_assign = T * TOP_K
    n_blocks = (n_assign + N_EXPERTS * (SLOT_BLOCK - 1) + SLOT_BLOCK - 1) // SLOT_BLOCK
    lb_fwd_all = jnp.cumsum(jax.nn.softmax(hg_lb_fwd.astype(F32), axis=0), axis=0)
    lb_bwd_all = jnp.cumsum(jax.nn.softmax(hg_lb_bwd.astype(F32), axis=0), axis=0)
    rc, rsa, rsb = _rotary_tables(L)
    row = lambda a: a.reshape(1, -1)

    x2d = x.reshape(T, D)
    for i in range(depth):
        (hq, hv, lff, lfb, hg, aq, ak, av, gh, ga) = _in_proj(
            x2d, w_in[i].astype(BF16), row(lb_fwd_all[i]), row(lb_bwd_all[i]), rc, rsa, rsb, L)
        y_h = _hgrn(hq, hv, lff, lfb, hg, hg_norm_g[i], B, L).reshape(T, HG_WIDTH)
        att = [_attn_group(aq, ak, av, g, B, L) for g in range(len(DIL_PAIRS))]
        x1, x1r, top_e, gate, rank, cnt = _mix(
            x2d, y_h, [a[0] for a in att], [a[1] for a in att], gh, ga,
            w_branch_hg[i].astype(BF16), w_branch_att[i].astype(BF16), w_out[i].astype(BF16),
            row(ln1_g[i]), row(ln1_b[i]), w_router[i].T.astype(BF16), router_bias[i].reshape(N_EXPERTS, 1), alpha)
        counts = cnt[:, 0].astype(jnp.int32)
        padded_start, block_expert, block_valid, n_used = _slot_layout(counts, n_blocks)
        dest = _dest(top_e, rank, padded_start)
        xs = _dispatch(dest, x1r, n_blocks * SLOT_BLOCK)
        ys = _experts(block_expert, block_valid, n_used, xs, w_exp_gate[i], w_exp_up[i], w_exp_down[i])
        x2d = _combine(dest, x1, gate.T, p[i].reshape(T, PLE_DIM), ys,
                       w_sh_gate[i].astype(BF16), w_sh_up[i].astype(BF16), w_sh_down[i].astype(BF16),
                       row(ln2_g[i]), row(ln2_b[i]), w_ple_gate[i].astype(BF16), w_ple_proj[i].astype(BF16),
                       row(ln3_g[i]), row(ln3_b[i]), alpha)
    return x2d.reshape(B, L, D)
```

```python
import functools

import jax
import jax.numpy as jnp
import numpy as np
from jax import lax
from jax.experimental import pallas as pl
from jax.experimental.pallas import tpu as pltpu

F32 = jnp.float32
BF16 = jnp.bfloat16

D_MODEL = 1024
HG_HEADS = 4
HG_DIM = 128
HG_WIDTH = HG_HEADS * HG_DIM
HG_CHUNK = 64
DIL_PAIRS = ((128, 1), (512, 4), (2048, 16))
ATT_SLOTS = 4
ATT_HEAD_DIM = 64
ATT_WIDTH = len(DIL_PAIRS) * ATT_SLOTS * ATT_HEAD_DIM
ATT_OUT = ATT_SLOTS * ATT_HEAD_DIM
ROT_DIM = ATT_HEAD_DIM // 4
ROPE_THETA = 500000.0
COL_SIZES = (HG_WIDTH,) * 5 + (ATT_WIDTH,) * 3 + (D_MODEL,) * 2
COL_STARTS = tuple(int(v) for v in np.cumsum((0,) + COL_SIZES)[:-1])
IN_COLS = sum(COL_SIZES)
N_EXPERTS = 256
TOP_K = 8
N_GROUPS = 8
TOPK_GROUPS = 4
PER_GROUP = N_EXPERTS // N_GROUPS
EXPERT_HIDDEN = 256
SHARED_HIDDEN = 256
ROUTED_SCALE = 2.5
SLOT_BLOCK = 512
PLE_DIM = 256
LN_EPS = 1e-5
NEG_INF = -1e30

LANES = 128
SUBLANES = 8
VMEM_LIMIT = 56 * 1024 * 1024

NT_DIMS = (((1,), (1,)), ((), ()))
TN_DIMS = (((0,), (0,)), ((), ()))


def _sigmoid(v):
    return jax.nn.sigmoid(v)


def _silu(v):
    return v * jax.nn.sigmoid(v)


def _layer_norm(v, g, b):
    mu = jnp.mean(v, axis=-1, keepdims=True)
    vc = v - mu
    var = jnp.mean(vc * vc, axis=-1, keepdims=True)
    return vc * lax.rsqrt(var + LN_EPS) * g + b


def _const_spec(shape):
    return pl.BlockSpec(shape, lambda *_: (0,) * len(shape))


def _in_proj_kernel(x_ref, w_ref, lbf_ref, lbb_ref, rc_ref, rsa_ref, rsb_ref,
                    hq_ref, hv_ref, lff_ref, lfb_ref, hg_ref,
                    aq_ref, ak_ref, av_ref, gh_ref, ga_ref):
    xb = x_ref[...].astype(BF16)

    def proj(seg):
        c0, width = COL_STARTS[seg], COL_SIZES[seg]
        return jnp.dot(xb, w_ref[:, c0:c0 + width], preferred_element_type=F32)

    hq_ref[...] = _silu(proj(0)).astype(BF16)
    hv_ref[...] = proj(1).astype(BF16)
    lb = lbf_ref[...]
    lff_ref[...] = jnp.log(lb + (1.0 - lb) * _sigmoid(proj(2)))
    lb = lbb_ref[...]
    lfb_ref[...] = jnp.log(lb + (1.0 - lb) * _sigmoid(proj(3)))
    hg_ref[...] = _silu(proj(4)).astype(BF16)

    rc, rsa, rsb = rc_ref[...], rsa_ref[...], rsb_ref[...]

    def rotary(z, out_ref, scale):
        for s in range(ATT_WIDTH // LANES):
            t = z[:, s * LANES:(s + 1) * LANES]
            up = pltpu.roll(t, LANES - ROT_DIM // 2, 1)
            dn = pltpu.roll(t, ROT_DIM // 2, 1)
            r = t * rc + up * rsa + dn * rsb
            out_ref[:, s * LANES:(s + 1) * LANES] = (r * scale).astype(BF16)

    rotary(proj(5), aq_ref, ATT_HEAD_DIM ** -0.5)
    rotary(proj(6), ak_ref, 1.0)
    av_ref[...] = proj(7).astype(BF16)
    gh_ref[...] = _sigmoid(proj(8)).astype(BF16)
    ga_ref[...] = _sigmoid(proj(9)).astype(BF16)


def _in_proj(x2d, w_b, lbf, lbb, rc, rsa, rsb, seq_len):
    T = x2d.shape[0]
    tm = 512
    n_pos_blocks = seq_len // tm
    row = lambda w: pl.BlockSpec((tm, w), lambda i: (i, 0))
    tab = pl.BlockSpec((tm, LANES), lambda i: (i % n_pos_blocks, 0))
    out_w = (HG_WIDTH, HG_WIDTH, HG_WIDTH, HG_WIDTH, HG_WIDTH, ATT_WIDTH, ATT_WIDTH, ATT_WIDTH, D_MODEL, D_MODEL)
    out_dt = (BF16, BF16, F32, F32, BF16, BF16, BF16, BF16, BF16, BF16)
    return pl.pallas_call(
        _in_proj_kernel,
        grid=(T // tm,),
        in_specs=[row(D_MODEL),
                  pl.BlockSpec((D_MODEL, IN_COLS), lambda i: (0, 0), pipeline_mode=pl.Buffered(1)),
                  _const_spec((1, HG_WIDTH)), _const_spec((1, HG_WIDTH)), tab, tab, tab],
        out_specs=[row(w) for w in out_w],
        out_shape=[jax.ShapeDtypeStruct((T, w), dt) for w, dt in zip(out_w, out_dt)],
        compiler_params=pltpu.CompilerParams(dimension_semantics=("parallel",), vmem_limit_bytes=VMEM_LIMIT),
        name="in_proj",
    )(x2d, w_b, lbf, lbb, rc, rsa, rsb)


def _split3(a):
    hi = a.astype(BF16)
    r1 = a - hi.astype(F32)
    mid = r1.astype(BF16)
    lo = (r1 - mid.astype(F32)).astype(BF16)
    return jnp.concatenate([hi, mid, lo], axis=1)


HG_SUPER = 4 * HG_CHUNK


def _hgrn_kernel(q_ref, v_ref, lff_ref, lfb_ref, sg_ref, ng_ref, y_ref,
                 of_scr, ob_scr, qf_scr, qb_scr, df_scr, db_scr, mf_scr, mb_scr):
    L = q_ref.shape[1]
    C, SC = HG_CHUNK, HG_SUPER
    n = L // C
    r_i = lax.broadcasted_iota(jnp.int32, (SC, SC), 0)
    c_i = lax.broadcasted_iota(jnp.int32, (SC, SC), 1)
    same = (r_i // C) == (c_i // C)
    r_l, c_l = r_i % C, c_i % C
    mask_f = jnp.logical_and(same, c_l <= r_l)
    mask_b = jnp.logical_and(same, c_l >= r_l)
    tri_f = mask_f.astype(BF16)
    tri_b = mask_b.astype(BF16)

    def chunk_rows(b, row):
        return jnp.concatenate([jnp.broadcast_to(b[c * C + row:c * C + row + 1, :], (C, HG_DIM))
                                for c in range(SC // C)], axis=0)

    dirs = ((lff_ref, tri_f, mask_f, C // 2 - 1, C - 1, of_scr, qf_scr, df_scr, mf_scr),
            (lfb_ref, tri_b, mask_b, C // 2, 0, ob_scr, qb_scr, db_scr, mb_scr))

    def local_body(i, carry):
        i0 = pl.multiple_of(i * SC, SC)
        rows = pl.ds(i0, SC)
        q = q_ref[0, rows, :].astype(F32)
        v = v_ref[0, rows, :]
        lfs = [d[0][0, rows, :] for d in dirs]
        sums = [jnp.dot(d[1], _split3(lf), preferred_element_type=F32) for d, lf in zip(dirs, lfs)]
        a_qs, a_ks, k_outs = [], [], []
        for d, lf, p in zip(dirs, lfs, sums):
            b = p[:, :HG_DIM] + p[:, HG_DIM:2 * HG_DIM] + p[:, 2 * HG_DIM:]
            b_ref, b_last = chunk_rows(b, d[3]), chunk_rows(b, d[4])
            k = 1.0 - jnp.exp(lf)
            a_qs.append((q * jnp.exp(b - b_ref)).astype(BF16))
            a_ks.append((k * jnp.exp(b_ref - b)).astype(BF16))
            k_outs.append((k * jnp.exp(b_last - b)).astype(BF16))
            d[6][rows, :] = (q * jnp.exp(b)).astype(BF16)
            d[7][rows, :] = b_last
        scores = [lax.dot_general(a_q, a_k, NT_DIMS, preferred_element_type=F32) for a_q, a_k in zip(a_qs, a_ks)]
        states = [[lax.dot_general(v[c * C:(c + 1) * C], k_out[c * C:(c + 1) * C], TN_DIMS,
                                   preferred_element_type=F32) for c in range(SC // C)] for k_out in k_outs]
        masked = [jnp.where(d[2], s, 0.0).astype(BF16) for d, s in zip(dirs, scores)]
        intra = [jnp.dot(m, v, preferred_element_type=F32) for m in masked]
        for d, o, st in zip(dirs, intra, states):
            d[5][rows, :] = o
            for c in range(SC // C):
                d[8][i0 // C + c] = st[c]
        return carry

    lax.fori_loop(0, L // SC, local_body, 0)

    def carry_part(c, st, o_scr, qin_scr, dec_scr, m_scr):
        rows = pl.ds(pl.multiple_of(c * C, C), C)
        o_scr[rows, :] = o_scr[rows, :] + lax.dot_general(qin_scr[rows, :], st.astype(BF16), NT_DIMS,
                                                          preferred_element_type=F32)
        return st * jnp.exp(dec_scr[pl.ds(pl.multiple_of(c * C, C), 1), :]) + m_scr[c]

    def carry_body(c, carry):
        st_f, st_b = carry
        st_f = carry_part(c, st_f, of_scr, qf_scr, df_scr, mf_scr)
        st_b = carry_part(n - 1 - c, st_b, ob_scr, qb_scr, db_scr, mb_scr)
        return st_f, st_b

    z = jnp.zeros((HG_DIM, HG_DIM), F32)
    lax.fori_loop(0, n, carry_body, (z, z), unroll=4)

    o = of_scr[...] + ob_scr[...]
    o = o * lax.rsqrt(jnp.mean(o * o, axis=-1, keepdims=True) + LN_EPS) * ng_ref[...]
    y_ref[0] = (o * sg_ref[0].astype(F32)).astype(BF16)


def _hgrn(hq, hv, lff, lfb, hg, norm_g, batch, seq_len):
    blk = pl.BlockSpec((1, seq_len, HG_DIM), lambda b, h: (b, 0, h))
    r3 = lambda a: a.reshape(batch, seq_len, HG_WIDTH)
    per_dir = [pltpu.VMEM((seq_len, HG_DIM), F32), pltpu.VMEM((seq_len, HG_DIM), BF16),
               pltpu.VMEM((seq_len, HG_DIM), F32), pltpu.VMEM((seq_len // HG_CHUNK, HG_DIM, HG_DIM), F32)]
    scratch = [s for pair in zip(per_dir, per_dir) for s in pair]
    return pl.pallas_call(
        _hgrn_kernel,
        grid=(batch, HG_HEADS),
        in_specs=[blk, blk, blk, blk, blk, pl.BlockSpec((1, HG_DIM), lambda b, h: (0, h))],
        out_specs=blk,
        out_shape=jax.ShapeDtypeStruct((batch, seq_len, HG_WIDTH), BF16),
        scratch_shapes=scratch,
        compiler_params=pltpu.CompilerParams(dimension_semantics=("parallel", "parallel"),
                                             vmem_limit_bytes=VMEM_LIMIT),
        name="hgrn",
    )(r3(hq), r3(hv), r3(lff), r3(lfb), r3(hg), norm_g.reshape(1, HG_WIDTH))


def _attn_kernel(q_ref, k_ref, v_ref, o_ref, lse_ref, *scratch, m_len, dil, qb, kw, span):
    lane = lax.broadcasted_iota(jnp.int32, (1, LANES), 1)
    head0 = lane < ATT_HEAD_DIM
    qi_l = lax.broadcasted_iota(jnp.int32, (qb, kw), 0)
    ki_l = lax.broadcasted_iota(jnp.int32, (qb, kw), 1)

    def run_class(load_q, load_k, load_v, store_o, store_l):
        def block(i, carry):
            q0 = pl.multiple_of(i * qb, qb)
            ks = pl.multiple_of(jnp.clip(q0 - span, 0, m_len - kw), span)
            mask = jnp.abs((q0 + qi_l) - (ks + ki_l)) <= span
            slabs = [slice(sl * LANES, (sl + 1) * LANES) for sl in range(ATT_OUT // LANES)]
            heads = (head0, jnp.logical_not(head0))
            qkv = [(load_q(q0, qb, cols), load_k(ks, kw, cols), load_v(ks, kw, cols)) for cols in slabs]
            scores = [lax.dot_general(jnp.where(hm, q2, jnp.zeros_like(q2)), k2, NT_DIMS, preferred_element_type=F32)
                      for (q2, k2, _) in qkv for hm in heads]
            probs, dens, lses = [], [], []
            for s in scores:
                s = jnp.where(mask, s, NEG_INF)
                mx = jnp.max(s, axis=-1, keepdims=True)
                e = jnp.exp(s - mx)
                den = jnp.sum(e, axis=-1, keepdims=True)
                probs.append(e.astype(BF16))
                dens.append(den)
                lses.append(mx + jnp.log(den))
            outs = [jnp.dot(pr, qkv[c // 2][2], preferred_element_type=F32) / dens[c] for c, pr in enumerate(probs)]
            for sl, cols in enumerate(slabs):
                store_o(q0, qb, cols, jnp.where(head0, outs[2 * sl], outs[2 * sl + 1]))
                store_l(q0, qb, cols, jnp.where(head0, lses[2 * sl], lses[2 * sl + 1]))
            return carry

        lax.fori_loop(0, m_len // qb, block, 0, unroll=2 if m_len // qb > 1 else 1)

    def loader(ref):
        return lambda r0, n, cols: ref[0, pl.ds(r0, n), cols]

    if dil == 1:
        def store_o(r0, n, cols, val):
            o_ref[0, pl.ds(r0, n), cols] = val.astype(BF16)

        def store_l(r0, n, cols, val):
            lse_ref[0, pl.ds(r0, n), cols] = val

        run_class(loader(q_ref), loader(k_ref), loader(v_ref), store_o, store_l)
        return

    q32, k32, v32, o32, l32, qc, kc, vc, oc, lc = scratch
    n_slab = ATT_OUT // LANES
    for slab in range(n_slab):
        cols = slice(slab * LANES, (slab + 1) * LANES)
        q32[slab] = q_ref[0, :, cols].astype(F32)
        k32[slab] = k_ref[0, :, cols].astype(F32)
        v32[slab] = v_ref[0, :, cols].astype(F32)

    def cls_loader(ref):
        return lambda r0, n, cols: ref[pl.ds(r0, n), cols]

    def store_oc(r0, n, cols, val):
        oc[pl.ds(r0, n), cols] = val

    def store_lc(r0, n, cols, val):
        lc[pl.ds(r0, n), cols] = val

    for r in range(dil):
        rows = pl.ds(r, m_len, stride=dil)
        for slab in range(n_slab):
            cols = slice(slab * LANES, (slab + 1) * LANES)
            qc[:, cols] = q32[slab, rows, :].astype(BF16)
            kc[:, cols] = k32[slab, rows, :].astype(BF16)
            vc[:, cols] = v32[slab, rows, :].astype(BF16)
        run_class(cls_loader(qc), cls_loader(kc), cls_loader(vc), store_oc, store_lc)
        for slab in range(n_slab):
            cols = slice(slab * LANES, (slab + 1) * LANES)
            o32[slab, rows, :] = oc[:, cols]
            l32[slab, rows, :] = lc[:, cols]
    for slab in range(n_slab):
        cols = slice(slab * LANES, (slab + 1) * LANES)
        o_ref[0, :, cols] = o32[slab].astype(BF16)
        lse_ref[0, :, cols] = l32[slab]


def _attn_group(aq, ak, av, g, batch, seq_len):
    window, dil = DIL_PAIRS[g]
    span = window // (2 * dil)
    m_len = seq_len // dil
    qb = min(2 * span, m_len)
    kw = min(qb + 2 * span, m_len)
    view = lambda a: a.reshape(batch, seq_len, ATT_WIDTH)
    in_blk = pl.BlockSpec((1, seq_len, ATT_OUT), lambda b: (b, 0, g))
    out_blk = pl.BlockSpec((1, seq_len, ATT_OUT), lambda b: (b, 0, 0))
    scratch = []
    if dil > 1:
        scratch = ([pltpu.VMEM((ATT_OUT // LANES, seq_len, LANES), F32)] * 5 + [pltpu.VMEM((m_len, ATT_OUT), BF16)] * 3
                   + [pltpu.VMEM((m_len, ATT_OUT), F32)] * 2)
    o, lse = pl.pallas_call(
        functools.partial(_attn_kernel, m_len=m_len, dil=dil, qb=qb, kw=kw, span=span),
        grid=(batch,),
        in_specs=[in_blk, in_blk, in_blk],
        out_specs=[out_blk, out_blk],
        out_shape=[jax.ShapeDtypeStruct((batch, seq_len, ATT_OUT), BF16),
                   jax.ShapeDtypeStruct((batch, seq_len, ATT_OUT), F32)],
        scratch_shapes=scratch,
        compiler_params=pltpu.CompilerParams(dimension_semantics=("parallel",), vmem_limit_bytes=VMEM_LIMIT),
        name=f"attn_g{g}",
    )(view(aq), view(ak), view(av))
    T = batch * seq_len
    return o.reshape(T, ATT_OUT), lse.reshape(T, ATT_OUT)


def _mix_kernel(x_ref, yh_ref, o0_ref, o1_ref, o2_ref, l0_ref, l1_ref, l2_ref, gh_ref, ga_ref,
                wbh_ref, wba_ref, wo_ref, g1_ref, b1_ref, wrt_ref, rb_ref,
                x1_ref, x1r_ref, tope_ref, gate_ref, rank_ref, cnt_ref, carry_scr, *, alpha):
    tm = x_ref.shape[0]

    @pl.when(pl.program_id(0) == 0)
    def _():
        carry_scr[...] = jnp.zeros_like(carry_scr)

    l0, l1, l2 = l0_ref[...], l1_ref[...], l2_ref[...]
    lm = jnp.maximum(jnp.maximum(l0, l1), l2)
    e0, e1, e2 = jnp.exp(l0 - lm), jnp.exp(l1 - lm), jnp.exp(l2 - lm)
    es = e0 + e1 + e2
    y_a = ((e0 / es) * o0_ref[...].astype(F32) + (e1 / es) * o1_ref[...].astype(F32)
           + (e2 / es) * o2_ref[...].astype(F32)).astype(BF16)
    merged = (gh_ref[...].astype(F32) * jnp.dot(yh_ref[...], wbh_ref[...], preferred_element_type=F32)
              + ga_ref[...].astype(F32) * jnp.dot(y_a, wba_ref[...], preferred_element_type=F32))
    mixed = jnp.dot(merged.astype(BF16), wo_ref[...], preferred_element_type=F32)
    x1 = _layer_norm(alpha * x_ref[...] + mixed, g1_ref[...], b1_ref[...])
    x1_ref[...] = x1
    x1r_ref[:, 0, :] = x1

    logit = lax.dot_general(wrt_ref[...], x1.astype(BF16), NT_DIMS, preferred_element_type=F32)
    score = _sigmoid(logit)
    biased = score + rb_ref[...]
    b3 = biased.reshape(N_GROUPS, PER_GROUP, tm)
    i3 = lax.broadcasted_iota(jnp.int32, (N_GROUPS, PER_GROUP, tm), 1)
    m1 = jnp.max(b3, axis=1, keepdims=True)
    idx1 = jnp.min(jnp.where(b3 == m1, i3, PER_GROUP), axis=1, keepdims=True)
    m2 = jnp.max(jnp.where(i3 == idx1, -jnp.inf, b3), axis=1, keepdims=True)
    gs = (m1 + m2).reshape(N_GROUPS, tm)
    gi = lax.broadcasted_iota(jnp.int32, (N_GROUPS, tm), 0)
    sel = jnp.zeros((N_GROUPS, tm), F32)
    cur = gs
    for _ in range(TOPK_GROUPS):
        m = jnp.max(cur, axis=0, keepdims=True)
        ix = jnp.min(jnp.where(cur == m, gi, N_GROUPS), axis=0, keepdims=True)
        hit = gi == ix
        sel = jnp.where(hit, 1.0, sel)
        cur = jnp.where(hit, -jnp.inf, cur)
    cur = jnp.where(sel.reshape(N_GROUPS, 1, tm) > 0.0, b3, -jnp.inf).reshape(N_EXPERTS, tm)
    ei = lax.broadcasted_iota(jnp.int32, (N_EXPERTS, tm), 0)
    chosen = jnp.zeros((N_EXPERTS, tm), F32)
    top_e, top_s = [], []
    for _ in range(TOP_K):
        m = jnp.max(cur, axis=0, keepdims=True)
        ix = jnp.min(jnp.where(cur == m, ei, N_EXPERTS), axis=0, keepdims=True)
        hit = ei == ix
        top_e.append(ix)
        top_s.append(jnp.sum(jnp.where(hit, score, 0.0), axis=0, keepdims=True))
        chosen = jnp.where(hit, 1.0, chosen)
        cur = jnp.where(hit, -jnp.inf, cur)
    s_sum = top_s[0]
    for s in top_s[1:]:
        s_sum = s_sum + s
    tope_ref[...] = jnp.concatenate(top_e, axis=0)
    gate_ref[...] = jnp.concatenate([s / s_sum * ROUTED_SCALE for s in top_s], axis=0)

    su = lax.broadcasted_iota(jnp.int32, (tm, tm), 0)
    tu = lax.broadcasted_iota(jnp.int32, (tm, tm), 1)
    before = (su < tu).astype(BF16)
    prior = carry_scr[:, 0:1] + jnp.dot(chosen.astype(BF16), before, preferred_element_type=F32)
    ranks = [jnp.sum(jnp.where(ei == ix, prior, 0.0), axis=0, keepdims=True) for ix in top_e]
    rank_ref[...] = jnp.concatenate(ranks, axis=0).astype(jnp.int32)
    carry_scr[...] = carry_scr[...] + jnp.sum(chosen, axis=1, keepdims=True)
    cnt_ref[...] = carry_scr[...]


def _mix(x2d, y_h, att_o, att_l, gh, ga, wbh, wba, wo, g1, b1, wrt, rb, alpha):
    T = x2d.shape[0]
    tm = 256
    row = lambda w: pl.BlockSpec((tm, w), lambda i: (i, 0))
    col = pl.BlockSpec((TOP_K, tm), lambda i: (0, i))
    return pl.pallas_call(
        functools.partial(_mix_kernel, alpha=alpha),
        grid=(T // tm,),
        in_specs=[row(D_MODEL), row(HG_WIDTH)] + [row(ATT_OUT)] * 6 + [row(D_MODEL), row(D_MODEL),
                  _const_spec((HG_WIDTH, D_MODEL)), _const_spec((ATT_OUT, D_MODEL)),
                  _const_spec((D_MODEL, D_MODEL)), _const_spec((1, D_MODEL)), _const_spec((1, D_MODEL)),
                  _const_spec((N_EXPERTS, D_MODEL)), _const_spec((N_EXPERTS, 1))],
        out_specs=[row(D_MODEL), pl.BlockSpec((tm, 1, D_MODEL), lambda i: (i, 0, 0)), col, col, col,
                   _const_spec((N_EXPERTS, LANES))],
        out_shape=[jax.ShapeDtypeStruct((T, D_MODEL), F32),
                   jax.ShapeDtypeStruct((T, 1, D_MODEL), F32),
                   jax.ShapeDtypeStruct((TOP_K, T), jnp.int32),
                   jax.ShapeDtypeStruct((TOP_K, T), F32),
                   jax.ShapeDtypeStruct((TOP_K, T), jnp.int32),
                   jax.ShapeDtypeStruct((N_EXPERTS, LANES), F32)],
        scratch_shapes=[pltpu.VMEM((N_EXPERTS, LANES), F32)],
        compiler_params=pltpu.CompilerParams(dimension_semantics=("arbitrary",), vmem_limit_bytes=VMEM_LIMIT),
        name="mix",
    )(x2d, y_h, *att_o, *att_l, gh, ga, wbh, wba, wo, g1, b1, wrt, rb)


def _dest_kernel(tope_ref, rank_ref, ps_ref, dest_ref):
    tb = tope_ref.shape[1]
    ei = lax.broadcasted_iota(jnp.int32, (N_EXPERTS, tb), 0)
    ps = ps_ref[...]
    starts = [jnp.sum(jnp.where(ei == tope_ref[k:k + 1, :], ps, 0.0), axis=0, keepdims=True)
              for k in range(TOP_K)]
    dest_ref[...] = jnp.concatenate(starts, axis=0).astype(jnp.int32) + rank_ref[...]


def _dest(top_e, rank, padded_start):
    T = top_e.shape[1]
    tb = 512
    col = pl.BlockSpec((TOP_K, tb), lambda i: (0, i))
    return pl.pallas_call(
        _dest_kernel,
        grid=(T // tb,),
        in_specs=[col, col, _const_spec((N_EXPERTS, 1))],
        out_specs=col,
        out_shape=jax.ShapeDtypeStruct((TOP_K, T), jnp.int32),
        compiler_params=pltpu.CompilerParams(dimension_semantics=("parallel",)),
        name="dest",
    )(top_e, rank, padded_start.astype(F32).reshape(N_EXPERTS, 1))


def _dispatch_kernel(dest_ref, x_ref, xs_hbm, sem):
    tb = x_ref.shape[0]

    def issue(g, c):
        t0 = pl.multiple_of(g * SUBLANES, SUBLANES)
        for j in range(SUBLANES):
            for k in range(TOP_K):
                pltpu.make_async_copy(x_ref.at[t0 + j], xs_hbm.at[dest_ref[k, t0 + j]], sem).start(priority=k % 2)
        return c

    lax.fori_loop(0, tb // SUBLANES, issue, 0)
    for _ in range(TOP_K):
        pltpu.make_async_copy(x_ref, xs_hbm.at[pl.ds(0, tb)], sem).wait()


def _dispatch(dest, x1, n_slots):
    T = x1.shape[0]
    tb = 512
    return pl.pallas_call(
        _dispatch_kernel,
        grid=(T // tb,),
        in_specs=[pl.BlockSpec((TOP_K, tb), lambda i: (0, i), memory_space=pltpu.SMEM),
                  pl.BlockSpec((tb, 1, D_MODEL), lambda i: (i, 0, 0))],
        out_specs=pl.BlockSpec(memory_space=pl.ANY),
        out_shape=jax.ShapeDtypeStruct((n_slots, 1, D_MODEL), F32),
        scratch_shapes=[pltpu.SemaphoreType.DMA],
        compiler_params=pltpu.CompilerParams(dimension_semantics=("arbitrary",)),
        name="dispatch",
    )(dest, x1)


EXPERT_SUBBLOCKS = 2
N_IN_BUF = 3


def _expert_kernel(be_ref, bv_ref, nu_ref, xs_hbm, wg_ref, wu_ref, wd_ref, ys_hbm,
                   xbuf, ybuf, wg_b, wu_b, wd_b, in_sem, out_sem):
    i = pl.program_id(0)
    n_used = nu_ref[0]
    slot = i % 2
    in_slot = i % N_IN_BUF

    def in_copy(blk, s):
        return pltpu.make_async_copy(xs_hbm.at[pl.ds(blk * SLOT_BLOCK, SLOT_BLOCK), 0, :], xbuf.at[s], in_sem.at[s])

    def out_copy(blk, s):
        return pltpu.make_async_copy(ybuf.at[s], ys_hbm.at[pl.ds(blk * SLOT_BLOCK, SLOT_BLOCK), 0, :], out_sem.at[s])

    @pl.when(i == 0)
    def _():
        for b in range(N_IN_BUF - 1):
            @pl.when(b < n_used)
            def _():
                in_copy(b, b).start()

    @pl.when(i < n_used)
    def _():
        ahead = i + N_IN_BUF - 1

        @pl.when(ahead < n_used)
        def _():
            in_copy(ahead, ahead % N_IN_BUF).start()

        in_copy(i, in_slot).wait()

        prev = be_ref[jnp.maximum(i - 1, 0)]

        @pl.when(jnp.logical_or(i == 0, be_ref[i] != prev))
        def _():
            wg_b[...] = wg_ref[0].astype(BF16)
            wu_b[...] = wu_ref[0].astype(BF16)
            wd_b[...] = wd_ref[0].astype(BF16)

        @pl.when(i >= 2)
        def _():
            out_copy(i - 2, slot).wait()

        sub = SLOT_BLOCK // EXPERT_SUBBLOCKS
        rows = lax.broadcasted_iota(jnp.int32, (sub, 1), 0)
        xbs = [jnp.where(rows + r * sub < bv_ref[i], xbuf[in_slot, pl.ds(r * sub, sub), :], 0.0).astype(BF16)
               for r in range(EXPERT_SUBBLOCKS)]
        hs = [(_silu(jnp.dot(xb, wg_b[...], preferred_element_type=F32))
               * jnp.dot(xb, wu_b[...], preferred_element_type=F32)).astype(BF16) for xb in xbs]
        for r, h in enumerate(hs):
            ybuf[slot, pl.ds(r * sub, sub), :] = jnp.dot(h, wd_b[...], preferred_element_type=F32)
        out_copy(i, slot).start(priority=1)

        @pl.when(i == n_used - 1)
        def _():
            @pl.when(i >= 1)
            def _():
                out_copy(i - 1, 1 - slot).wait()

            out_copy(i, slot).wait()


def _experts(block_expert, block_valid, n_used, xs, w_gate, w_up, w_down):
    n_blocks = xs.shape[0] // SLOT_BLOCK
    return pl.pallas_call(
        _expert_kernel,
        grid_spec=pltpu.PrefetchScalarGridSpec(
            num_scalar_prefetch=3,
            grid=(n_blocks,),
            in_specs=[pl.BlockSpec(memory_space=pl.ANY),
                      pl.BlockSpec((1, D_MODEL, EXPERT_HIDDEN), lambda i, be, bv, nu: (be[i], 0, 0)),
                      pl.BlockSpec((1, D_MODEL, EXPERT_HIDDEN), lambda i, be, bv, nu: (be[i], 0, 0)),
                      pl.BlockSpec((1, EXPERT_HIDDEN, D_MODEL), lambda i, be, bv, nu: (be[i], 0, 0))],
            out_specs=pl.BlockSpec(memory_space=pl.ANY),
            scratch_shapes=[pltpu.VMEM((N_IN_BUF, SLOT_BLOCK, D_MODEL), F32),
                            pltpu.VMEM((2, SLOT_BLOCK, D_MODEL), F32),
                            pltpu.VMEM((D_MODEL, EXPERT_HIDDEN), BF16),
                            pltpu.VMEM((D_MODEL, EXPERT_HIDDEN), BF16),
                            pltpu.VMEM((EXPERT_HIDDEN, D_MODEL), BF16),
                            pltpu.SemaphoreType.DMA((N_IN_BUF,)), pltpu.SemaphoreType.DMA((2,))],
        ),
        out_shape=jax.ShapeDtypeStruct(xs.shape, F32),
        compiler_params=pltpu.CompilerParams(dimension_semantics=("arbitrary",), vmem_limit_bytes=VMEM_LIMIT),
        name="experts",
    )(block_expert, block_valid, n_used, xs, w_gate, w_up, w_down)


def _combine_kernel(dest_ref, dnext_ref, x1_ref, gate_ref, p_ref, ys_hbm,
                    wsg_ref, wsu_ref, wsd_ref, g2_ref, b2_ref, wpg_ref, wpp_ref, g3_ref, b3_ref,
                    out_ref, *scratch, alpha):
    sem = scratch[-1]
    bufs = (scratch[:TOP_K], scratch[TOP_K:2 * TOP_K])
    tm = bufs[0][0].shape[0]
    i = pl.program_id(0)

    def issue(d_ref, t, k, col0, half):
        pltpu.make_async_copy(ys_hbm.at[d_ref[k, col0 + t]], bufs[half][k].at[t], sem.at[half]).start(priority=k % 2)

    def wait_rows(half):
        for k in range(TOP_K):
            pltpu.make_async_copy(ys_hbm.at[pl.ds(0, tm)], bufs[half][k], sem.at[half]).wait()

    def compute(half):
        rows = pl.ds(half * tm, tm)
        x1 = x1_ref[rows, :]
        xb = x1.astype(BF16)
        hs = (_silu(jnp.dot(xb, wsg_ref[...], preferred_element_type=F32))
              * jnp.dot(xb, wsu_ref[...], preferred_element_type=F32))
        ffn = jnp.dot(hs.astype(BF16), wsd_ref[...], preferred_element_type=F32)
        gate = gate_ref[rows, :]
        for k in range(TOP_K):
            ffn = ffn + gate[:, k:k + 1] * bufs[half][k][:, 0, :]
        x2 = _layer_norm(alpha * x1 + ffn, g2_ref[...], b2_ref[...])
        ple = (_sigmoid(jnp.dot(x2.astype(BF16), wpg_ref[...], preferred_element_type=F32))
               * jnp.dot(p_ref[rows, :].astype(BF16), wpp_ref[...], preferred_element_type=F32))
        out_ref[rows, :] = _layer_norm(alpha * x2 + ple, g3_ref[...], b3_ref[...])

    @pl.when(i == 0)
    def _():
        def first(g, c):
            t0 = pl.multiple_of(g * SUBLANES, SUBLANES)
            for j in range(SUBLANES):
                for k in range(TOP_K):
                    issue(dest_ref, t0 + j, k, 0, 0)
            return c

        lax.fori_loop(0, tm // SUBLANES, first, 0)

    wait_rows(0)
    for t in range(tm):
        for k in range(TOP_K):
            issue(dest_ref, t, k, tm, 1)
    compute(0)
    wait_rows(1)
    for t in range(tm):
        for k in range(TOP_K):
            issue(dnext_ref, t, k, 0, 0)
    compute(1)

    @pl.when(i == pl.num_programs(0) - 1)
    def _():
        wait_rows(0)


def _combine(dest, x1, gate_t, p2d, ys, wsg, wsu, wsd, g2, b2, wpg, wpp, g3, b3, alpha):
    T = x1.shape[0]
    tm = 256
    n = T // (2 * tm)
    row = lambda w: pl.BlockSpec((2 * tm, w), lambda i: (i, 0))
    return pl.pallas_call(
        functools.partial(_combine_kernel, alpha=alpha),
        grid=(n,),
        in_specs=[pl.BlockSpec((TOP_K, 2 * tm), lambda i: (0, i), memory_space=pltpu.SMEM),
                  pl.BlockSpec((TOP_K, tm), lambda i: (0, jnp.minimum(2 * i + 2, 2 * n - 1)), memory_space=pltpu.SMEM),
                  row(D_MODEL), row(TOP_K), row(PLE_DIM),
                  pl.BlockSpec(memory_space=pl.ANY),
                  _const_spec((D_MODEL, SHARED_HIDDEN)), _const_spec((D_MODEL, SHARED_HIDDEN)),
                  _const_spec((SHARED_HIDDEN, D_MODEL)), _const_spec((1, D_MODEL)), _const_spec((1, D_MODEL)),
                  _const_spec((D_MODEL, D_MODEL)), _const_spec((PLE_DIM, D_MODEL)),
                  _const_spec((1, D_MODEL)), _const_spec((1, D_MODEL))],
        out_specs=row(D_MODEL),
        out_shape=jax.ShapeDtypeStruct((T, D_MODEL), F32),
        scratch_shapes=[pltpu.VMEM((tm, 1, D_MODEL), F32)] * (2 * TOP_K) + [pltpu.SemaphoreType.DMA((2,))],
        compiler_params=pltpu.CompilerParams(dimension_semantics=("arbitrary",), vmem_limit_bytes=VMEM_LIMIT),
        name="combine",
    )(dest, dest, x1, gate_t, p2d, ys, wsg, wsu, wsd, g2, b2, wpg, wpp, g3, b3)


def _rotary_tables(seq_len):
    half = ROT_DIM // 2
    inv_freq = ROPE_THETA ** (-jnp.arange(half, dtype=F32) / half)
    ang = jnp.arange(seq_len, dtype=F32)[:, None] * inv_freq[None, :]
    cos, sin = jnp.cos(ang), jnp.sin(ang)
    pad = ATT_HEAD_DIM - ROT_DIM
    one = jnp.ones((seq_len, pad), F32)
    zero = jnp.zeros((seq_len, pad + half), F32)
    rc = jnp.concatenate([cos, cos, one], axis=1)
    rsa = jnp.concatenate([-sin, zero], axis=1)
    rsb = jnp.concatenate([jnp.zeros((seq_len, half), F32), sin, jnp.zeros((seq_len, pad), F32)], axis=1)
    rep = lambda a: jnp.tile(a, (1, LANES // ATT_HEAD_DIM))
    return rep(rc), rep(rsa), rep(rsb)


def _slot_layout(counts, n_blocks):
    padded = (counts + SLOT_BLOCK - 1) // SLOT_BLOCK * SLOT_BLOCK
    padded_end = jnp.cumsum(padded)
    padded_start = padded_end - padded
    blk0 = jnp.arange(n_blocks, dtype=jnp.int32) * SLOT_BLOCK
    block_expert = jnp.minimum(jnp.sum(blk0[:, None] >= padded_end[None, :], axis=1), N_EXPERTS - 1).astype(jnp.int32)
    n_used = (padded_end[-1] // SLOT_BLOCK).astype(jnp.int32)
    own = block_expert[:, None] == jnp.arange(N_EXPERTS, dtype=jnp.int32)[None, :]
    pick = lambda tab: jnp.sum(jnp.where(own, tab[None, :], 0), axis=1)
    valid = jnp.clip(pick(counts) - (blk0 - pick(padded_start)), 0, SLOT_BLOCK)
    block_valid = jnp.where(jnp.arange(n_blocks) < n_used, valid, 0).astype(jnp.int32)
    return padded_start.astype(jnp.int32), block_expert, block_valid, n_used.reshape(1)


def kernel(x, p, w_in, hg_lb_fwd, hg_lb_bwd, hg_norm_g, w_branch_hg, w_branch_att, w_out, ln1_g, ln1_b, w_router, router_bias, w_exp_gate, w_exp_up, w_exp_down, w_sh_gate, w_sh_up, w_sh_down, ln2_g, ln2_b, w_ple_gate, w_ple_proj, ln3_g, ln3_b):
    B, L, D = x.shape
    depth = w_in.shape[0]
    T = B * L
    alpha = (2 * depth) ** 0.25
    n_assign = T * TOP_K
    n_blocks = (n_assign + N_EXPERTS * (SLOT_BLOCK - 1) + SLOT_BLOCK - 1) // SLOT_BLOCK
    lb_fwd_all = jnp.cumsum(jax.nn.softmax(hg_lb_fwd.astype(F32), axis=0), axis=0)
    lb_bwd_all = jnp.cumsum(jax.nn.softmax(hg_lb_bwd.astype(F32), axis=0), axis=0)
    rc, rsa, rsb = _rotary_tables(L)
    row = lambda a: a.reshape(1, -1)

    x2d = x.reshape(T, D)
    for i in range(depth):
        (hq, hv, lff, lfb, hg, aq, ak, av, gh, ga) = _in_proj(
            x2d, w_in[i].astype(BF16), row(lb_fwd_all[i]), row(lb_bwd_all[i]), rc, rsa, rsb, L)
        y_h = _hgrn(hq, hv, lff, lfb, hg, hg_norm_g[i], B, L).reshape(T, HG_WIDTH)
        att = [_attn_group(aq, ak, av, g, B, L) for g in range(len(DIL_PAIRS))]
        x1, x1r, top_e, gate, rank, cnt = _mix(
            x2d, y_h, [a[0] for a in att], [a[1] for a in att], gh, ga,
            w_branch_hg[i].astype(BF16), w_branch_att[i].astype(BF16), w_out[i].astype(BF16),
            row(ln1_g[i]), row(ln1_b[i]), w_router[i].T.astype(BF16), router_bias[i].reshape(N_EXPERTS, 1), alpha)
        counts = cnt[:, 0].astype(jnp.int32)
        padded_start, block_expert, block_valid, n_used = _slot_layout(counts, n_blocks)
        dest = _dest(top_e, rank, padded_start)
        xs = _dispatch(dest, x1r, n_blocks * SLOT_BLOCK)
        ys = _experts(block_expert, block_valid, n_used, xs, w_exp_gate[i], w_exp_up[i], w_exp_down[i])
        x2d = _combine(dest, x1, gate.T, p[i].reshape(T, PLE_DIM), ys,
                       w_sh_gate[i].astype(BF16), w_sh_up[i].astype(BF16), w_sh_down[i].astype(BF16),
                       row(ln2_g[i]), row(ln2_b[i]), w_ple_gate[i].astype(BF16), w_ple_proj[i].astype(BF16),
                       row(ln3_g[i]), row(ln3_b[i]), alpha)
    return x2d.reshape(B, L, D)
```

```python
import functools

import jax
import jax.numpy as jnp
import numpy as np
from jax import lax
from jax.experimental import pallas as pl
from jax.experimental.pallas import tpu as pltpu

F32 = jnp.float32
BF16 = jnp.bfloat16

D_MODEL = 1024
HG_HEADS = 4
HG_DIM = 128
HG_WIDTH = HG_HEADS * HG_DIM
HG_CHUNK = 64
DIL_PAIRS = ((128, 1), (512, 4), (2048, 16))
ATT_SLOTS = 4
ATT_HEAD_DIM = 64
ATT_WIDTH = len(DIL_PAIRS) * ATT_SLOTS * ATT_HEAD_DIM
ATT_OUT = ATT_SLOTS * ATT_HEAD_DIM
ROT_DIM = ATT_HEAD_DIM // 4
ROPE_THETA = 500000.0
COL_SIZES = (HG_WIDTH,) * 5 + (ATT_WIDTH,) * 3 + (D_MODEL,) * 2
COL_STARTS = tuple(int(v) for v in np.cumsum((0,) + COL_SIZES)[:-1])
IN_COLS = sum(COL_SIZES)
N_EXPERTS = 256
TOP_K = 8
N_GROUPS = 8
TOPK_GROUPS = 4
PER_GROUP = N_EXPERTS // N_GROUPS
EXPERT_HIDDEN = 256
SHARED_HIDDEN = 256
ROUTED_SCALE = 2.5
SLOT_BLOCK = 512
PLE_DIM = 256
LN_EPS = 1e-5
NEG_INF = -1e30

LANES = 128
SUBLANES = 8
VMEM_LIMIT = 56 * 1024 * 1024

NT_DIMS = (((1,), (1,)), ((), ()))
TN_DIMS = (((0,), (0,)), ((), ()))


def _sigmoid(v):
    return jax.nn.sigmoid(v)


def _silu(v):
    return v * jax.nn.sigmoid(v)


def _layer_norm(v, g, b):
    mu = jnp.mean(v, axis=-1, keepdims=True)
    vc = v - mu
    var = jnp.mean(vc * vc, axis=-1, keepdims=True)
    return vc * lax.rsqrt(var + LN_EPS) * g + b


def _const_spec(shape):
    return pl.BlockSpec(shape, lambda *_: (0,) * len(shape))


def _in_proj_kernel(x_ref, w_ref, lbf_ref, lbb_ref, rc_ref, rsa_ref, rsb_ref,
                    hq_ref, hv_ref, lff_ref, lfb_ref, hg_ref,
                    aq_ref, ak_ref, av_ref, gh_ref, ga_ref):
    xb = x_ref[...].astype(BF16)

    def proj(seg):
        c0, width = COL_STARTS[seg], COL_SIZES[seg]
        return jnp.dot(xb, w_ref[:, c0:c0 + width], preferred_element_type=F32)

    hq_ref[...] = _silu(proj(0)).astype(BF16)
    hv_ref[...] = proj(1).astype(BF16)
    lb = lbf_ref[...]
    lff_ref[...] = jnp.log(lb + (1.0 - lb) * _sigmoid(proj(2)))
    lb = lbb_ref[...]
    lfb_ref[...] = jnp.log(lb + (1.0 - lb) * _sigmoid(proj(3)))
    hg_ref[...] = _silu(proj(4)).astype(BF16)

    rc, rsa, rsb = rc_ref[...], rsa_ref[...], rsb_ref[...]

    def rotary(z, out_ref, scale):
        for s in range(ATT_WIDTH // LANES):
            t = z[:, s * LANES:(s + 1) * LANES]
            up = pltpu.roll(t, LANES - ROT_DIM // 2, 1)
            dn = pltpu.roll(t, ROT_DIM // 2, 1)
            r = t * rc + up * rsa + dn * rsb
            out_ref[:, s * LANES:(s + 1) * LANES] = (r * scale).astype(BF16)

    rotary(proj(5), aq_ref, ATT_HEAD_DIM ** -0.5)
    rotary(proj(6), ak_ref, 1.0)
    av_ref[...] = proj(7).astype(BF16)
    gh_ref[...] = _sigmoid(proj(8)).astype(BF16)
    ga_ref[...] = _sigmoid(proj(9)).astype(BF16)


def _in_proj(x2d, w_b, lbf, lbb, rc, rsa, rsb, seq_len):
    T = x2d.shape[0]
    tm = 512
    n_pos_blocks = seq_len // tm
    row = lambda w: pl.BlockSpec((tm, w), lambda i: (i, 0))
    tab = pl.BlockSpec((tm, LANES), lambda i: (i % n_pos_blocks, 0))
    out_w = (HG_WIDTH, HG_WIDTH, HG_WIDTH, HG_WIDTH, HG_WIDTH, ATT_WIDTH, ATT_WIDTH, ATT_WIDTH, D_MODEL, D_MODEL)
    out_dt = (BF16, BF16, F32, F32, BF16, BF16, BF16, BF16, BF16, BF16)
    return pl.pallas_call(
        _in_proj_kernel,
        grid=(T // tm,),
        in_specs=[row(D_MODEL),
                  pl.BlockSpec((D_MODEL, IN_COLS), lambda i: (0, 0), pipeline_mode=pl.Buffered(1)),
                  _const_spec((1, HG_WIDTH)), _const_spec((1, HG_WIDTH)), tab, tab, tab],
        out_specs=[row(w) for w in out_w],
        out_shape=[jax.ShapeDtypeStruct((T, w), dt) for w, dt in zip(out_w, out_dt)],
        compiler_params=pltpu.CompilerParams(dimension_semantics=("parallel",), vmem_limit_bytes=VMEM_LIMIT),
        name="in_proj",
    )(x2d, w_b, lbf, lbb, rc, rsa, rsb)


def _split3(a):
    hi = a.astype(BF16)
    r1 = a - hi.astype(F32)
    mid = r1.astype(BF16)
    lo = (r1 - mid.astype(F32)).astype(BF16)
    return jnp.concatenate([hi, mid, lo], axis=1)


HG_SUPER = 4 * HG_CHUNK


def _hgrn_kernel(q_ref, v_ref, lff_ref, lfb_ref, sg_ref, ng_ref, y_ref,
                 of_scr, ob_scr, qf_scr, qb_scr, df_scr, db_scr, mf_scr, mb_scr):
    L = q_ref.shape[1]
    C, SC = HG_CHUNK, HG_SUPER
    n = L // C
    r_i = lax.broadcasted_iota(jnp.int32, (SC, SC), 0)
    c_i = lax.broadcasted_iota(jnp.int32, (SC, SC), 1)
    same = (r_i // C) == (c_i // C)
    r_l, c_l = r_i % C, c_i % C
    mask_f = jnp.logical_and(same, c_l <= r_l)
    mask_b = jnp.logical_and(same, c_l >= r_l)
    tri_f = mask_f.astype(BF16)
    tri_b = mask_b.astype(BF16)

    def chunk_rows(b, row):
        return jnp.concatenate([jnp.broadcast_to(b[c * C + row:c * C + row + 1, :], (C, HG_DIM))
                                for c in range(SC // C)], axis=0)

    dirs = ((lff_ref, tri_f, mask_f, C // 2 - 1, C - 1, of_scr, qf_scr, df_scr, mf_scr),
            (lfb_ref, tri_b, mask_b, C // 2, 0, ob_scr, qb_scr, db_scr, mb_scr))

    def local_body(i, carry):
        i0 = pl.multiple_of(i * SC, SC)
        rows = pl.ds(i0, SC)
        q = q_ref[0, rows, :].astype(F32)
        v = v_ref[0, rows, :]
        lfs = [d[0][0, rows, :] for d in dirs]
        sums = [jnp.dot(d[1], _split3(lf), preferred_element_type=F32) for d, lf in zip(dirs, lfs)]
        a_qs, a_ks, k_outs = [], [], []
        for d, lf, p in zip(dirs, lfs, sums):
            b = p[:, :HG_DIM] + p[:, HG_DIM:2 * HG_DIM] + p[:, 2 * HG_DIM:]
            b_ref, b_last = chunk_rows(b, d[3]), chunk_rows(b, d[4])
            k = 1.0 - jnp.exp(lf)
            a_qs.append((q * jnp.exp(b - b_ref)).astype(BF16))
            a_ks.append((k * jnp.exp(b_ref - b)).astype(BF16))
            k_outs.append((k * jnp.exp(b_last - b)).astype(BF16))
            d[6][rows, :] = (q * jnp.exp(b)).astype(BF16)
            d[7][rows, :] = b_last
        scores = [lax.dot_general(a_q, a_k, NT_DIMS, preferred_element_type=F32) for a_q, a_k in zip(a_qs, a_ks)]
        states = [[lax.dot_general(v[c * C:(c + 1) * C], k_out[c * C:(c + 1) * C], TN_DIMS,
                                   preferred_element_type=F32) for c in range(SC // C)] for k_out in k_outs]
        masked = [jnp.where(d[2], s, 0.0).astype(BF16) for d, s in zip(dirs, scores)]
        intra = [jnp.dot(m, v, preferred_element_type=F32) for m in masked]
        for d, o, st in zip(dirs, intra, states):
            d[5][rows, :] = o
            for c in range(SC // C):
                d[8][i0 // C + c] = st[c]
        return carry

    lax.fori_loop(0, L // SC, local_body, 0)

    def carry_part(c, st, o_scr, qin_scr, dec_scr, m_scr):
        rows = pl.ds(pl.multiple_of(c * C, C), C)
        o_scr[rows, :] = o_scr[rows, :] + lax.dot_general(qin_scr[rows, :], st.astype(BF16), NT_DIMS,
                                                          preferred_element_type=F32)
        return st * jnp.exp(dec_scr[pl.ds(pl.multiple_of(c * C, C), 1), :]) + m_scr[c]

    def carry_body(c, carry):
        st_f, st_b = carry
        st_f = carry_part(c, st_f, of_scr, qf_scr, df_scr, mf_scr)
        st_b = carry_part(n - 1 - c, st_b, ob_scr, qb_scr, db_scr, mb_scr)
        return st_f, st_b

    z = jnp.zeros((HG_DIM, HG_DIM), F32)
    lax.fori_loop(0, n, carry_body, (z, z), unroll=4)

    o = of_scr[...] + ob_scr[...]
    o = o * lax.rsqrt(jnp.mean(o * o, axis=-1, keepdims=True) + LN_EPS) * ng_ref[...]
    y_ref[0] = (o * sg_ref[0].astype(F32)).astype(BF16)


def _hgrn(hq, hv, lff, lfb, hg, norm_g, batch, seq_len):
    blk = pl.BlockSpec((1, seq_len, HG_DIM), lambda b, h: (b, 0, h))
    r3 = lambda a: a.reshape(batch, seq_len, HG_WIDTH)
    per_dir = [pltpu.VMEM((seq_len, HG_DIM), F32), pltpu.VMEM((seq_len, HG_DIM), BF16),
               pltpu.VMEM((seq_len, HG_DIM), F32), pltpu.VMEM((seq_len // HG_CHUNK, HG_DIM, HG_DIM), F32)]
    scratch = [s for pair in zip(per_dir, per_dir) for s in pair]
    return pl.pallas_call(
        _hgrn_kernel,
        grid=(batch, HG_HEADS),
        in_specs=[blk, blk, blk, blk, blk, pl.BlockSpec((1, HG_DIM), lambda b, h: (0, h))],
        out_specs=blk,
        out_shape=jax.ShapeDtypeStruct((batch, seq_len, HG_WIDTH), BF16),
        scratch_shapes=scratch,
        compiler_params=pltpu.CompilerParams(dimension_semantics=("parallel", "parallel"),
                                             vmem_limit_bytes=VMEM_LIMIT),
        name="hgrn",
    )(r3(hq), r3(hv), r3(lff), r3(lfb), r3(hg), norm_g.reshape(1, HG_WIDTH))


def _attn_kernel(q_ref, k_ref, v_ref, o_ref, lse_ref, *scratch, m_len, dil, qb, kw, span):
    lane = lax.broadcasted_iota(jnp.int32, (1, LANES), 1)
    head0 = lane < ATT_HEAD_DIM
    qi_l = lax.broadcasted_iota(jnp.int32, (qb, kw), 0)
    ki_l = lax.broadcasted_iota(jnp.int32, (qb, kw), 1)

    def run_class(load_q, load_k, load_v, store_o, store_l):
        def block(i, carry):
            q0 = pl.multiple_of(i * qb, qb)
            ks = pl.multiple_of(jnp.clip(q0 - span, 0, m_len - kw), span)
            mask = jnp.abs((q0 + qi_l) - (ks + ki_l)) <= span
            slabs = [slice(sl * LANES, (sl + 1) * LANES) for sl in range(ATT_OUT // LANES)]
            heads = (head0, jnp.logical_not(head0))
            qkv = [(load_q(q0, qb, cols), load_k(ks, kw, cols), load_v(ks, kw, cols)) for cols in slabs]
            scores = [lax.dot_general(jnp.where(hm, q2, jnp.zeros_like(q2)), k2, NT_DIMS, preferred_element_type=F32)
                      for (q2, k2, _) in qkv for hm in heads]
            probs, dens, lses = [], [], []
            for s in scores:
                s = jnp.where(mask, s, NEG_INF)
                mx = jnp.max(s, axis=-1, keepdims=True)
                e = jnp.exp(s - mx)
                den = jnp.sum(e, axis=-1, keepdims=True)
                probs.append(e.astype(BF16))
                dens.append(den)
                lses.append(mx + jnp.log(den))
            outs = [jnp.dot(pr, qkv[c // 2][2], preferred_element_type=F32) / dens[c] for c, pr in enumerate(probs)]
            for sl, cols in enumerate(slabs):
                store_o(q0, qb, cols, jnp.where(head0, outs[2 * sl], outs[2 * sl + 1]))
                store_l(q0, qb, cols, jnp.where(head0, lses[2 * sl], lses[2 * sl + 1]))
            return carry

        lax.fori_loop(0, m_len // qb, block, 0, unroll=2 if m_len // qb > 1 else 1)

    def loader(ref):
        return lambda r0, n, cols: ref[0, pl.ds(r0, n), cols]

    if dil == 1:
        def store_o(r0, n, cols, val):
            o_ref[0, pl.ds(r0, n), cols] = val.astype(BF16)

        def store_l(r0, n, cols, val):
            lse_ref[0, pl.ds(r0, n), cols] = val

        run_class(loader(q_ref), loader(k_ref), loader(v_ref), store_o, store_l)
        return

    q32, k32, v32, o32, l32, qc, kc, vc, oc, lc = scratch
    n_slab = ATT_OUT // LANES
    for slab in range(n_slab):
        cols = slice(slab * LANES, (slab + 1) * LANES)
        q32[slab] = q_ref[0, :, cols].astype(F32)
        k32[slab] = k_ref[0, :, cols].astype(F32)
        v32[slab] = v_ref[0, :, cols].astype(F32)

    def cls_loader(ref):
        return lambda r0, n, cols: ref[pl.ds(r0, n), cols]

    def store_oc(r0, n, cols, val):
        oc[pl.ds(r0, n), cols] = val

    def store_lc(r0, n, cols, val):
        lc[pl.ds(r0, n), cols] = val

    for r in range(dil):
        rows = pl.ds(r, m_len, stride=dil)
        for slab in range(n_slab):
            cols = slice(slab * LANES, (slab + 1) * LANES)
            qc[:, cols] = q32[slab, rows, :].astype(BF16)
            kc[:, cols] = k32[slab, rows, :].astype(BF16)
            vc[:, cols] = v32[slab, rows, :].astype(BF16)
        run_class(cls_loader(qc), cls_loader(kc), cls_loader(vc), store_oc, store_lc)
        for slab in range(n_slab):
            cols = slice(slab * LANES, (slab + 1) * LANES)
            o32[slab, rows, :] = oc[:, cols]
            l32[slab, rows, :] = lc[:, cols]
    for slab in range(n_slab):
        cols = slice(slab * LANES, (slab + 1) * LANES)
        o_ref[0, :, cols] = o32[slab].astype(BF16)
        lse_ref[0, :, cols] = l32[slab]


def _attn_group(aq, ak, av, g, batch, seq_len):
    window, dil = DIL_PAIRS[g]
    span = window // (2 * dil)
    m_len = seq_len // dil
    qb = min(2 * span, m_len)
    kw = min(qb + 2 * span, m_len)
    view = lambda a: a.reshape(batch, seq_len, ATT_WIDTH)
    in_blk = pl.BlockSpec((1, seq_len, ATT_OUT), lambda b: (b, 0, g))
    out_blk = pl.BlockSpec((1, seq_len, ATT_OUT), lambda b: (b, 0, 0))
    scratch = []
    if dil > 1:
        scratch = ([pltpu.VMEM((ATT_OUT // LANES, seq_len, LANES), F32)] * 5 + [pltpu.VMEM((m_len, ATT_OUT), BF16)] * 3
                   + [pltpu.VMEM((m_len, ATT_OUT), F32)] * 2)
    o, lse = pl.pallas_call(
        functools.partial(_attn_kernel, m_len=m_len, dil=dil, qb=qb, kw=kw, span=span),
        grid=(batch,),
        in_specs=[in_blk, in_blk, in_blk],
        out_specs=[out_blk, out_blk],
        out_shape=[jax.ShapeDtypeStruct((batch, seq_len, ATT_OUT), BF16),
                   jax.ShapeDtypeStruct((batch, seq_len, ATT_OUT), F32)],
        scratch_shapes=scratch,
        compiler_params=pltpu.CompilerParams(dimension_semantics=("parallel",), vmem_limit_bytes=VMEM_LIMIT),
        name=f"attn_g{g}",
    )(view(aq), view(ak), view(av))
    T = batch * seq_len
    return o.reshape(T, ATT_OUT), lse.reshape(T, ATT_OUT)


def _mix_kernel(x_ref, yh_ref, o0_ref, o1_ref, o2_ref, l0_ref, l1_ref, l2_ref, gh_ref, ga_ref,
                wbh_ref, wba_ref, wo_ref, g1_ref, b1_ref, wrt_ref, rb_ref,
                x1_ref, x1r_ref, tope_ref, gate_ref, rank_ref, cnt_ref, carry_scr, *, alpha):
    tm = x_ref.shape[0]

    @pl.when(pl.program_id(0) == 0)
    def _():
        carry_scr[...] = jnp.zeros_like(carry_scr)

    l0, l1, l2 = l0_ref[...], l1_ref[...], l2_ref[...]
    lm = jnp.maximum(jnp.maximum(l0, l1), l2)
    e0, e1, e2 = jnp.exp(l0 - lm), jnp.exp(l1 - lm), jnp.exp(l2 - lm)
    es = e0 + e1 + e2
    y_a = ((e0 / es) * o0_ref[...].astype(F32) + (e1 / es) * o1_ref[...].astype(F32)
           + (e2 / es) * o2_ref[...].astype(F32)).astype(BF16)
    merged = (gh_ref[...].astype(F32) * jnp.dot(yh_ref[...], wbh_ref[...], preferred_element_type=F32)
              + ga_ref[...].astype(F32) * jnp.dot(y_a, wba_ref[...], preferred_element_type=F32))
    mixed = jnp.dot(merged.astype(BF16), wo_ref[...], preferred_element_type=F32)
    x1 = _layer_norm(alpha * x_ref[...] + mixed, g1_ref[...], b1_ref[...])
    x1_ref[...] = x1
    x1r_ref[:, 0, :] = x1

    logit = lax.dot_general(wrt_ref[...], x1.astype(BF16), NT_DIMS, preferred_element_type=F32)
    score = _sigmoid(logit)
    biased = score + rb_ref[...]
    b3 = biased.reshape(N_GROUPS, PER_GROUP, tm)
    i3 = lax.broadcasted_iota(jnp.int32, (N_GROUPS, PER_GROUP, tm), 1)
    m1 = jnp.max(b3, axis=1, keepdims=True)
    idx1 = jnp.min(jnp.where(b3 == m1, i3, PER_GROUP), axis=1, keepdims=True)
    m2 = jnp.max(jnp.where(i3 == idx1, -jnp.inf, b3), axis=1, keepdims=True)
    gs = (m1 + m2).reshape(N_GROUPS, tm)
    gi = lax.broadcasted_iota(jnp.int32, (N_GROUPS, tm), 0)
    sel = jnp.zeros((N_GROUPS, tm), F32)
    cur = gs
    for _ in range(TOPK_GROUPS):
        m = jnp.max(cur, axis=0, keepdims=True)
        ix = jnp.min(jnp.where(cur == m, gi, N_GROUPS), axis=0, keepdims=True)
        hit = gi == ix
        sel = jnp.where(hit, 1.0, sel)
        cur = jnp.where(hit, -jnp.inf, cur)
    cur = jnp.where(sel.reshape(N_GROUPS, 1, tm) > 0.0, b3, -jnp.inf).reshape(N_EXPERTS, tm)
    ei = lax.broadcasted_iota(jnp.int32, (N_EXPERTS, tm), 0)
    chosen = jnp.zeros((N_EXPERTS, tm), F32)
    top_e, top_s = [], []
    for _ in range(TOP_K):
        m = jnp.max(cur, axis=0, keepdims=True)
        ix = jnp.min(jnp.where(cur == m, ei, N_EXPERTS), axis=0, keepdims=True)
        hit = ei == ix
        top_e.append(ix)
        top_s.append(jnp.sum(jnp.where(hit, score, 0.0), axis=0, keepdims=True))
        chosen = jnp.where(hit, 1.0, chosen)
        cur = jnp.where(hit, -jnp.inf, cur)
    s_sum = top_s[0]
    for s in top_s[1:]:
        s_sum = s_sum + s
    tope_ref[...] = jnp.concatenate(top_e, axis=0)
    gate_ref[...] = jnp.concatenate([s / s_sum * ROUTED_SCALE for s in top_s], axis=0)

    su = lax.broadcasted_iota(jnp.int32, (tm, tm), 0)
    tu = lax.broadcasted_iota(jnp.int32, (tm, tm), 1)
    before = (su < tu).astype(BF16)
    prior = carry_scr[:, 0:1] + jnp.dot(chosen.astype(BF16), before, preferred_element_type=F32)
    ranks = [jnp.sum(jnp.where(ei == ix, prior, 0.0), axis=0, keepdims=True) for ix in top_e]
    rank_ref[...] = jnp.concatenate(ranks, axis=0).astype(jnp.int32)
    carry_scr[...] = carry_scr[...] + jnp.sum(chosen, axis=1, keepdims=True)
    cnt_ref[...] = carry_scr[...]


def _mix(x2d, y_h, att_o, att_l, gh, ga, wbh, wba, wo, g1, b1, wrt, rb, alpha):
    T = x2d.shape[0]
    tm = 256
    row = lambda w: pl.BlockSpec((tm, w), lambda i: (i, 0))
    col = pl.BlockSpec((TOP_K, tm), lambda i: (0, i))
    return pl.pallas_call(
        functools.partial(_mix_kernel, alpha=alpha),
        grid=(T // tm,),
        in_specs=[row(D_MODEL), row(HG_WIDTH)] + [row(ATT_OUT)] * 6 + [row(D_MODEL), row(D_MODEL),
                  _const_spec((HG_WIDTH, D_MODEL)), _const_spec((ATT_OUT, D_MODEL)),
                  _const_spec((D_MODEL, D_MODEL)), _const_spec((1, D_MODEL)), _const_spec((1, D_MODEL)),
                  _const_spec((N_EXPERTS, D_MODEL)), _const_spec((N_EXPERTS, 1))],
        out_specs=[row(D_MODEL), pl.BlockSpec((tm, 1, D_MODEL), lambda i: (i, 0, 0)), col, col, col,
                   _const_spec((N_EXPERTS, LANES))],
        out_shape=[jax.ShapeDtypeStruct((T, D_MODEL), F32),
                   jax.ShapeDtypeStruct((T, 1, D_MODEL), F32),
                   jax.ShapeDtypeStruct((TOP_K, T), jnp.int32),
                   jax.ShapeDtypeStruct((TOP_K, T), F32),
                   jax.ShapeDtypeStruct((TOP_K, T), jnp.int32),
                   jax.ShapeDtypeStruct((N_EXPERTS, LANES), F32)],
        scratch_shapes=[pltpu.VMEM((N_EXPERTS, LANES), F32)],
        compiler_params=pltpu.CompilerParams(dimension_semantics=("arbitrary",), vmem_limit_bytes=VMEM_LIMIT),
        name="mix",
    )(x2d, y_h, *att_o, *att_l, gh, ga, wbh, wba, wo, g1, b1, wrt, rb)


def _dest_kernel(tope_ref, rank_ref, ps_ref, dest_ref):
    tb = tope_ref.shape[1]
    ei = lax.broadcasted_iota(jnp.int32, (N_EXPERTS, tb), 0)
    ps = ps_ref[...]
    starts = [jnp.sum(jnp.where(ei == tope_ref[k:k + 1, :], ps, 0.0), axis=0, keepdims=True)
              for k in range(TOP_K)]
    dest_ref[...] = jnp.concatenate(starts, axis=0).astype(jnp.int32) + rank_ref[...]


def _dest(top_e, rank, padded_start):
    T = top_e.shape[1]
    tb = 512
    col = pl.BlockSpec((TOP_K, tb), lambda i: (0, i))
    return pl.pallas_call(
        _dest_kernel,
        grid=(T // tb,),
        in_specs=[col, col, _const_spec((N_EXPERTS, 1))],
        out_specs=col,
        out_shape=jax.ShapeDtypeStruct((TOP_K, T), jnp.int32),
        compiler_params=pltpu.CompilerParams(dimension_semantics=("parallel",)),
        name="dest",
    )(top_e, rank, padded_start.astype(F32).reshape(N_EXPERTS, 1))


def _dispatch_kernel(dest_ref, x_ref, xs_hbm, sem):
    tb = x_ref.shape[0]

    def issue(g, c):
        t0 = pl.multiple_of(g * SUBLANES, SUBLANES)
        for j in range(SUBLANES):
            for k in range(TOP_K):
                pltpu.make_async_copy(x_ref.at[t0 + j], xs_hbm.at[dest_ref[k, t0 + j]], sem).start(priority=k % 2)
        return c

    lax.fori_loop(0, tb // SUBLANES, issue, 0)
    for _ in range(TOP_K):
        pltpu.make_async_copy(x_ref, xs_hbm.at[pl.ds(0, tb)], sem).wait()


def _dispatch(dest, x1, n_slots):
    T = x1.shape[0]
    tb = 512
    return pl.pallas_call(
        _dispatch_kernel,
        grid=(T // tb,),
        in_specs=[pl.BlockSpec((TOP_K, tb), lambda i: (0, i), memory_space=pltpu.SMEM),
                  pl.BlockSpec((tb, 1, D_MODEL), lambda i: (i, 0, 0))],
        out_specs=pl.BlockSpec(memory_space=pl.ANY),
        out_shape=jax.ShapeDtypeStruct((n_slots, 1, D_MODEL), F32),
        scratch_shapes=[pltpu.SemaphoreType.DMA],
        compiler_params=pltpu.CompilerParams(dimension_semantics=("arbitrary",)),
        name="dispatch",
    )(dest, x1)


EXPERT_SUBBLOCKS = 2
N_IN_BUF = 3


def _expert_kernel(be_ref, bv_ref, nu_ref, xs_hbm, wg_ref, wu_ref, wd_ref, ys_hbm,
                   xbuf, ybuf, wg_b, wu_b, wd_b, in_sem, out_sem):
    i = pl.program_id(0)
    n_used = nu_ref[0]
    slot = i % 2
    in_slot = i % N_IN_BUF

    def live_rows(blk):
        return pl.multiple_of((bv_ref[blk] + SUBLANES - 1) // SUBLANES * SUBLANES, SUBLANES)

    def in_copy(blk, s):
        n = live_rows(blk)
        return pltpu.make_async_copy(xs_hbm.at[pl.ds(blk * SLOT_BLOCK, n), 0, :], xbuf.at[s, pl.ds(0, n), :], in_sem.at[s])

    def out_copy(blk, s):
        n = live_rows(blk)
        return pltpu.make_async_copy(ybuf.at[s, pl.ds(0, n), :], ys_hbm.at[pl.ds(blk * SLOT_BLOCK, n), 0, :], out_sem.at[s])

    @pl.when(i == 0)
    def _():
        for b in range(N_IN_BUF - 1):
            @pl.when(b < n_used)
            def _():
                in_copy(b, b).start()

    @pl.when(i < n_used)
    def _():
        ahead = i + N_IN_BUF - 1

        @pl.when(ahead < n_used)
        def _():
            in_copy(ahead, ahead % N_IN_BUF).start()

        in_copy(i, in_slot).wait()

        prev = be_ref[jnp.maximum(i - 1, 0)]

        @pl.when(jnp.logical_or(i == 0, be_ref[i] != prev))
        def _():
            wg_b[...] = wg_ref[0].astype(BF16)
            wu_b[...] = wu_ref[0].astype(BF16)
            wd_b[...] = wd_ref[0].astype(BF16)

        @pl.when(i >= 2)
        def _():
            out_copy(i - 2, slot).wait()

        sub = SLOT_BLOCK // EXPERT_SUBBLOCKS
        rows = lax.broadcasted_iota(jnp.int32, (sub, 1), 0)
        xbs = [jnp.where(rows + r * sub < bv_ref[i], xbuf[in_slot, pl.ds(r * sub, sub), :], 0.0).astype(BF16)
               for r in range(EXPERT_SUBBLOCKS)]
        hs = [(_silu(jnp.dot(xb, wg_b[...], preferred_element_type=F32))
               * jnp.dot(xb, wu_b[...], preferred_element_type=F32)).astype(BF16) for xb in xbs]
        for r, h in enumerate(hs):
            ybuf[slot, pl.ds(r * sub, sub), :] = jnp.dot(h, wd_b[...], preferred_element_type=F32)
        out_copy(i, slot).start(priority=1)

        @pl.when(i == n_used - 1)
        def _():
            @pl.when(i >= 1)
            def _():
                out_copy(i - 1, 1 - slot).wait()

            out_copy(i, slot).wait()


def _experts(block_expert, block_valid, n_used, xs, w_gate, w_up, w_down):
    n_blocks = xs.shape[0] // SLOT_BLOCK
    return pl.pallas_call(
        _expert_kernel,
        grid_spec=pltpu.PrefetchScalarGridSpec(
            num_scalar_prefetch=3,
            grid=(n_blocks,),
            in_specs=[pl.BlockSpec(memory_space=pl.ANY),
                      pl.BlockSpec((1, D_MODEL, EXPERT_HIDDEN), lambda i, be, bv, nu: (be[i], 0, 0)),
                      pl.BlockSpec((1, D_MODEL, EXPERT_HIDDEN), lambda i, be, bv, nu: (be[i], 0, 0)),
                      pl.BlockSpec((1, EXPERT_HIDDEN, D_MODEL), lambda i, be, bv, nu: (be[i], 0, 0))],
            out_specs=pl.BlockSpec(memory_space=pl.ANY),
            scratch_shapes=[pltpu.VMEM((N_IN_BUF, SLOT_BLOCK, D_MODEL), F32),
                            pltpu.VMEM((2, SLOT_BLOCK, D_MODEL), F32),
                            pltpu.VMEM((D_MODEL, EXPERT_HIDDEN), BF16),
                            pltpu.VMEM((D_MODEL, EXPERT_HIDDEN), BF16),
                            pltpu.VMEM((EXPERT_HIDDEN, D_MODEL), BF16),
                            pltpu.SemaphoreType.DMA((N_IN_BUF,)), pltpu.SemaphoreType.DMA((2,))],
        ),
        out_shape=jax.ShapeDtypeStruct(xs.shape, F32),
        compiler_params=pltpu.CompilerParams(dimension_semantics=("arbitrary",), vmem_limit_bytes=VMEM_LIMIT),
        name="experts",
    )(block_expert, block_valid, n_used, xs, w_gate, w_up, w_down)


def _combine_kernel(dest_ref, dnext_ref, x1_ref, gate_ref, p_ref, ys_hbm,
                    wsg_ref, wsu_ref, wsd_ref, g2_ref, b2_ref, wpg_ref, wpp_ref, g3_ref, b3_ref,
                    out_ref, *scratch, alpha):
    sem = scratch[-1]
    bufs = (scratch[:TOP_K], scratch[TOP_K:2 * TOP_K])
    tm = bufs[0][0].shape[0]
    i = pl.program_id(0)

    def issue(d_ref, t, k, col0, half):
        pltpu.make_async_copy(ys_hbm.at[d_ref[k, col0 + t]], bufs[half][k].at[t], sem.at[half]).start(priority=k % 2)

    def wait_rows(half):
        for k in range(TOP_K):
            pltpu.make_async_copy(ys_hbm.at[pl.ds(0, tm)], bufs[half][k], sem.at[half]).wait()

    def compute(half):
        rows = pl.ds(half * tm, tm)
        x1 = x1_ref[rows, :]
        xb = x1.astype(BF16)
        hs = (_silu(jnp.dot(xb, wsg_ref[...], preferred_element_type=F32))
              * jnp.dot(xb, wsu_ref[...], preferred_element_type=F32))
        ffn = jnp.dot(hs.astype(BF16), wsd_ref[...], preferred_element_type=F32)
        gate = gate_ref[rows, :]
        for k in range(TOP_K):
            ffn = ffn + gate[:, k:k + 1] * bufs[half][k][:, 0, :]
        x2 = _layer_norm(alpha * x1 + ffn, g2_ref[...], b2_ref[...])
        ple = (_sigmoid(jnp.dot(x2.astype(BF16), wpg_ref[...], preferred_element_type=F32))
               * jnp.dot(p_ref[rows, :].astype(BF16), wpp_ref[...], preferred_element_type=F32))
        out_ref[rows, :] = _layer_norm(alpha * x2 + ple, g3_ref[...], b3_ref[...])

    @pl.when(i == 0)
    def _():
        def first(g, c):
            t0 = pl.multiple_of(g * SUBLANES, SUBLANES)
            for j in range(SUBLANES):
                for k in range(TOP_K):
                    issue(dest_ref, t0 + j, k, 0, 0)
            return c

        lax.fori_loop(0, tm // SUBLANES, first, 0)

    wait_rows(0)
    for t in range(tm):
        for k in range(TOP_K):
            issue(dest_ref, t, k, tm, 1)
    compute(0)
    wait_rows(1)
    for t in range(tm):
        for k in range(TOP_K):
            issue(dnext_ref, t, k, 0, 0)
    compute(1)

    @pl.when(i == pl.num_programs(0) - 1)
    def _():
        wait_rows(0)


def _combine(dest, x1, gate_t, p2d, ys, wsg, wsu, wsd, g2, b2, wpg, wpp, g3, b3, alpha):
    T = x1.shape[0]
    tm = 256
    n = T // (2 * tm)
    row = lambda w: pl.BlockSpec((2 * tm, w), lambda i: (i, 0))
    return pl.pallas_call(
        functools.partial(_combine_kernel, alpha=alpha),
        grid=(n,),
        in_specs=[pl.BlockSpec((TOP_K, 2 * tm), lambda i: (0, i), memory_space=pltpu.SMEM),
                  pl.BlockSpec((TOP_K, tm), lambda i: (0, jnp.minimum(2 * i + 2, 2 * n - 1)), memory_space=pltpu.SMEM),
                  row(D_MODEL), row(TOP_K), row(PLE_DIM),
                  pl.BlockSpec(memory_space=pl.ANY),
                  _const_spec((D_MODEL, SHARED_HIDDEN)), _const_spec((D_MODEL, SHARED_HIDDEN)),
                  _const_spec((SHARED_HIDDEN, D_MODEL)), _const_spec((1, D_MODEL)), _const_spec((1, D_MODEL)),
                  _const_spec((D_MODEL, D_MODEL)), _const_spec((PLE_DIM, D_MODEL)),
                  _const_spec((1, D_MODEL)), _const_spec((1, D_MODEL))],
        out_specs=row(D_MODEL),
        out_shape=jax.ShapeDtypeStruct((T, D_MODEL), F32),
        scratch_shapes=[pltpu.VMEM((tm, 1, D_MODEL), F32)] * (2 * TOP_K) + [pltpu.SemaphoreType.DMA((2,))],
        compiler_params=pltpu.CompilerParams(dimension_semantics=("arbitrary",), vmem_limit_bytes=VMEM_LIMIT),
        name="combine",
    )(dest, dest, x1, gate_t, p2d, ys, wsg, wsu, wsd, g2, b2, wpg, wpp, g3, b3)


def _rotary_tables(seq_len):
    half = ROT_DIM // 2
    inv_freq = ROPE_THETA ** (-jnp.arange(half, dtype=F32) / half)
    ang = jnp.arange(seq_len, dtype=F32)[:, None] * inv_freq[None, :]
    cos, sin = jnp.cos(ang), jnp.sin(ang)
    pad = ATT_HEAD_DIM - ROT_DIM
    one = jnp.ones((seq_len, pad), F32)
    zero = jnp.zeros((seq_len, pad + half), F32)
    rc = jnp.concatenate([cos, cos, one], axis=1)
    rsa = jnp.concatenate([-sin, zero], axis=1)
    rsb = jnp.concatenate([jnp.zeros((seq_len, half), F32), sin, jnp.zeros((seq_len, pad), F32)], axis=1)
    rep = lambda a: jnp.tile(a, (1, LANES // ATT_HEAD_DIM))
    return rep(rc), rep(rsa), rep(rsb)


def _slot_layout(counts, n_blocks):
    padded = (counts + SLOT_BLOCK - 1) // SLOT_BLOCK * SLOT_BLOCK
    padded_end = jnp.cumsum(padded)
    padded_start = padded_end - padded
    blk0 = jnp.arange(n_blocks, dtype=jnp.int32) * SLOT_BLOCK
    block_expert = jnp.minimum(jnp.sum(blk0[:, None] >= padded_end[None, :], axis=1), N_EXPERTS - 1).astype(jnp.int32)
    n_used = (padded_end[-1] // SLOT_BLOCK).astype(jnp.int32)
    own = block_expert[:, None] == jnp.arange(N_EXPERTS, dtype=jnp.int32)[None, :]
    pick = lambda tab: jnp.sum(jnp.where(own, tab[None, :], 0), axis=1)
    valid = jnp.clip(pick(counts) - (blk0 - pick(padded_start)), 0, SLOT_BLOCK)
    block_valid = jnp.where(jnp.arange(n_blocks) < n_used, valid, 0).astype(jnp.int32)
    return padded_start.astype(jnp.int32), block_expert, block_valid, n_used.reshape(1)


def kernel(x, p, w_in, hg_lb_fwd, hg_lb_bwd, hg_norm_g, w_branch_hg, w_branch_att, w_out, ln1_g, ln1_b, w_router, router_bias, w_exp_gate, w_exp_up, w_exp_down, w_sh_gate, w_sh_up, w_sh_down, ln2_g, ln2_b, w_ple_gate, w_ple_proj, ln3_g, ln3_b):
    B, L, D = x.shape
    depth = w_in.shape[0]
    T = B * L
    alpha = (2 * depth) ** 0.25
    n_assign = T * TOP_K
    n_blocks = (n_assign + N_EXPERTS * (SLOT_BLOCK - 1) + SLOT_BLOCK - 1) // SLOT_BLOCK
    lb_fwd_all = jnp.cumsum(jax.nn.softmax(hg_lb_fwd.astype(F32), axis=0), axis=0)
    lb_bwd_all = jnp.cumsum(jax.nn.softmax(hg_lb_bwd.astype(F32), axis=0), axis=0)
    rc, rsa, rsb = _rotary_tables(L)
    row = lambda a: a.reshape(1, -1)

    x2d = x.reshape(T, D)
    for i in range(depth):
        (hq, hv, lff, lfb, hg, aq, ak, av, gh, ga) = _in_proj(
            x2d, w_in[i].astype(BF16), row(lb_fwd_all[i]), row(lb_bwd_all[i]), rc, rsa, rsb, L)
        y_h = _hgrn(hq, hv, lff, lfb, hg, hg_norm_g[i], B, L).reshape(T, HG_WIDTH)
        att = [_attn_group(aq, ak, av, g, B, L) for g in range(len(DIL_PAIRS))]
        x1, x1r, top_e, gate, rank, cnt = _mix(
            x2d, y_h, [a[0] for a in att], [a[1] for a in att], gh, ga,
            w_branch_hg[i].astype(BF16), w_branch_att[i].astype(BF16), w_out[i].astype(BF16),
            row(ln1_g[i]), row(ln1_b[i]), w_router[i].T.astype(BF16), router_bias[i].reshape(N_EXPERTS, 1), alpha)
        counts = cnt[:, 0].astype(jnp.int32)
        padded_start, block_expert, block_valid, n_used = _slot_layout(counts, n_blocks)
        dest = _dest(top_e, rank, padded_start)
        xs = _dispatch(dest, x1r, n_blocks * SLOT_BLOCK)
        ys = _experts(block_expert, block_valid, n_used, xs, w_exp_gate[i], w_exp_up[i], w_exp_down[i])
        x2d = _combine(dest, x1, gate.T, p[i].reshape(T, PLE_DIM), ys,
                       w_sh_gate[i].astype(BF16), w_sh_up[i].astype(BF16), w_sh_down[i].astype(BF16),
                       row(ln2_g[i]), row(ln2_b[i]), w_ple_gate[i].astype(BF16), w_ple_proj[i].astype(BF16),
                       row(ln3_g[i]), row(ln3_b[i]), alpha)
    return x2d.reshape(B, L, D)
```

```python
import functools

import jax
import jax.numpy as jnp
import numpy as np
from jax import lax
from jax.experimental import pallas as pl
from jax.experimental.pallas import tpu as pltpu

F32 = jnp.float32
BF16 = jnp.bfloat16

D_MODEL = 1024
HG_HEADS = 4
HG_DIM = 128
HG_WIDTH = HG_HEADS * HG_DIM
HG_CHUNK = 64
DIL_PAIRS = ((128, 1), (512, 4), (2048, 16))
ATT_SLOTS = 4
ATT_HEAD_DIM = 64
ATT_WIDTH = len(DIL_PAIRS) * ATT_SLOTS * ATT_HEAD_DIM
ATT_OUT = ATT_SLOTS * ATT_HEAD_DIM
ROT_DIM = ATT_HEAD_DIM // 4
ROPE_THETA = 500000.0
COL_SIZES = (HG_WIDTH,) * 5 + (ATT_WIDTH,) * 3 + (D_MODEL,) * 2
COL_STARTS = tuple(int(v) for v in np.cumsum((0,) + COL_SIZES)[:-1])
IN_COLS = sum(COL_SIZES)
N_EXPERTS = 256
TOP_K = 8
N_GROUPS = 8
TOPK_GROUPS = 4
PER_GROUP = N_EXPERTS // N_GROUPS
EXPERT_HIDDEN = 256
SHARED_HIDDEN = 256
ROUTED_SCALE = 2.5
SLOT_BLOCK = 1024
PLE_DIM = 256
LN_EPS = 1e-5
NEG_INF = -1e30

LANES = 128
SUBLANES = 8
VMEM_LIMIT = 56 * 1024 * 1024

NT_DIMS = (((1,), (1,)), ((), ()))
TN_DIMS = (((0,), (0,)), ((), ()))


def _sigmoid(v):
    return jax.nn.sigmoid(v)


def _silu(v):
    return v * jax.nn.sigmoid(v)


def _layer_norm(v, g, b):
    mu = jnp.mean(v, axis=-1, keepdims=True)
    vc = v - mu
    var = jnp.mean(vc * vc, axis=-1, keepdims=True)
    return vc * lax.rsqrt(var + LN_EPS) * g + b


def _const_spec(shape):
    return pl.BlockSpec(shape, lambda *_: (0,) * len(shape))


def _in_proj_kernel(x_ref, w_ref, lbf_ref, lbb_ref, rc_ref, rsa_ref, rsb_ref,
                    hq_ref, hv_ref, lff_ref, lfb_ref, hg_ref,
                    aq_ref, ak_ref, av_ref, gh_ref, ga_ref):
    xb = x_ref[...].astype(BF16)

    def proj(seg):
        c0, width = COL_STARTS[seg], COL_SIZES[seg]
        return jnp.dot(xb, w_ref[:, c0:c0 + width], preferred_element_type=F32)

    hq_ref[...] = _silu(proj(0)).astype(BF16)
    hv_ref[...] = proj(1).astype(BF16)
    lb = lbf_ref[...]
    lff_ref[...] = jnp.log(lb + (1.0 - lb) * _sigmoid(proj(2)))
    lb = lbb_ref[...]
    lfb_ref[...] = jnp.log(lb + (1.0 - lb) * _sigmoid(proj(3)))
    hg_ref[...] = _silu(proj(4)).astype(BF16)

    rc, rsa, rsb = rc_ref[...], rsa_ref[...], rsb_ref[...]

    def rotary(z, out_ref, scale):
        for s in range(ATT_WIDTH // LANES):
            t = z[:, s * LANES:(s + 1) * LANES]
            up = pltpu.roll(t, LANES - ROT_DIM // 2, 1)
            dn = pltpu.roll(t, ROT_DIM // 2, 1)
            r = t * rc + up * rsa + dn * rsb
            out_ref[:, s * LANES:(s + 1) * LANES] = (r * scale).astype(BF16)

    rotary(proj(5), aq_ref, ATT_HEAD_DIM ** -0.5)
    rotary(proj(6), ak_ref, 1.0)
    av_ref[...] = proj(7).astype(BF16)
    gh_ref[...] = _sigmoid(proj(8)).astype(BF16)
    ga_ref[...] = _sigmoid(proj(9)).astype(BF16)


def _in_proj(x2d, w_b, lbf, lbb, rc, rsa, rsb, seq_len):
    T = x2d.shape[0]
    tm = 512
    n_pos_blocks = seq_len // tm
    row = lambda w: pl.BlockSpec((tm, w), lambda i: (i, 0))
    tab = pl.BlockSpec((tm, LANES), lambda i: (i % n_pos_blocks, 0))
    out_w = (HG_WIDTH, HG_WIDTH, HG_WIDTH, HG_WIDTH, HG_WIDTH, ATT_WIDTH, ATT_WIDTH, ATT_WIDTH, D_MODEL, D_MODEL)
    out_dt = (BF16, BF16, F32, F32, BF16, BF16, BF16, BF16, BF16, BF16)
    return pl.pallas_call(
        _in_proj_kernel,
        grid=(T // tm,),
        in_specs=[row(D_MODEL),
                  pl.BlockSpec((D_MODEL, IN_COLS), lambda i: (0, 0), pipeline_mode=pl.Buffered(1)),
                  _const_spec((1, HG_WIDTH)), _const_spec((1, HG_WIDTH)), tab, tab, tab],
        out_specs=[row(w) for w in out_w],
        out_shape=[jax.ShapeDtypeStruct((T, w), dt) for w, dt in zip(out_w, out_dt)],
        compiler_params=pltpu.CompilerParams(dimension_semantics=("parallel",), vmem_limit_bytes=VMEM_LIMIT),
        name="in_proj",
    )(x2d, w_b, lbf, lbb, rc, rsa, rsb)


def _split3(a):
    hi = a.astype(BF16)
    r1 = a - hi.astype(F32)
    mid = r1.astype(BF16)
    lo = (r1 - mid.astype(F32)).astype(BF16)
    return jnp.concatenate([hi, mid, lo], axis=1)


HG_SUPER = 4 * HG_CHUNK


def _hgrn_kernel(q_ref, v_ref, lff_ref, lfb_ref, sg_ref, ng_ref, y_ref,
                 of_scr, ob_scr, qf_scr, qb_scr, df_scr, db_scr, mf_scr, mb_scr):
    L = q_ref.shape[1]
    C, SC = HG_CHUNK, HG_SUPER
    n = L // C
    r_i = lax.broadcasted_iota(jnp.int32, (SC, SC), 0)
    c_i = lax.broadcasted_iota(jnp.int32, (SC, SC), 1)
    same = (r_i // C) == (c_i // C)
    r_l, c_l = r_i % C, c_i % C
    mask_f = jnp.logical_and(same, c_l <= r_l)
    mask_b = jnp.logical_and(same, c_l >= r_l)
    tri_f = mask_f.astype(BF16)
    tri_b = mask_b.astype(BF16)

    def chunk_rows(b, row):
        return jnp.concatenate([jnp.broadcast_to(b[c * C + row:c * C + row + 1, :], (C, HG_DIM))
                                for c in range(SC // C)], axis=0)

    dirs = ((lff_ref, tri_f, mask_f, C // 2 - 1, C - 1, of_scr, qf_scr, df_scr, mf_scr),
            (lfb_ref, tri_b, mask_b, C // 2, 0, ob_scr, qb_scr, db_scr, mb_scr))

    def local_body(i, carry):
        i0 = pl.multiple_of(i * SC, SC)
        rows = pl.ds(i0, SC)
        q = q_ref[0, rows, :].astype(F32)
        v = v_ref[0, rows, :]
        lfs = [d[0][0, rows, :] for d in dirs]
        sums = [jnp.dot(d[1], _split3(lf), preferred_element_type=F32) for d, lf in zip(dirs, lfs)]
        a_qs, a_ks, k_outs = [], [], []
        for d, lf, p in zip(dirs, lfs, sums):
            b = p[:, :HG_DIM] + p[:, HG_DIM:2 * HG_DIM] + p[:, 2 * HG_DIM:]
            b_ref, b_last = chunk_rows(b, d[3]), chunk_rows(b, d[4])
            k = 1.0 - jnp.exp(lf)
            a_qs.append((q * jnp.exp(b - b_ref)).astype(BF16))
            a_ks.append((k * jnp.exp(b_ref - b)).astype(BF16))
            k_outs.append((k * jnp.exp(b_last - b)).astype(BF16))
            d[6][rows, :] = (q * jnp.exp(b)).astype(BF16)
            d[7][rows, :] = b_last
        scores = [lax.dot_general(a_q, a_k, NT_DIMS, preferred_element_type=F32) for a_q, a_k in zip(a_qs, a_ks)]
        states = [[lax.dot_general(v[c * C:(c + 1) * C], k_out[c * C:(c + 1) * C], TN_DIMS,
                                   preferred_element_type=F32) for c in range(SC // C)] for k_out in k_outs]
        masked = [jnp.where(d[2], s, 0.0).astype(BF16) for d, s in zip(dirs, scores)]
        intra = [jnp.dot(m, v, preferred_element_type=F32) for m in masked]
        for d, o, st in zip(dirs, intra, states):
            d[5][rows, :] = o
            for c in range(SC // C):
                d[8][i0 // C + c] = st[c]
        return carry

    lax.fori_loop(0, L // SC, local_body, 0)

    def carry_part(c, st, o_scr, qin_scr, dec_scr, m_scr):
        rows = pl.ds(pl.multiple_of(c * C, C), C)
        o_scr[rows, :] = o_scr[rows, :] + lax.dot_general(qin_scr[rows, :], st.astype(BF16), NT_DIMS,
                                                          preferred_element_type=F32)
        return st * jnp.exp(dec_scr[pl.ds(pl.multiple_of(c * C, C), 1), :]) + m_scr[c]

    def carry_body(c, carry):
        st_f, st_b = carry
        st_f = carry_part(c, st_f, of_scr, qf_scr, df_scr, mf_scr)
        st_b = carry_part(n - 1 - c, st_b, ob_scr, qb_scr, db_scr, mb_scr)
        return st_f, st_b

    z = jnp.zeros((HG_DIM, HG_DIM), F32)
    lax.fori_loop(0, n, carry_body, (z, z), unroll=4)

    o = of_scr[...] + ob_scr[...]
    o = o * lax.rsqrt(jnp.mean(o * o, axis=-1, keepdims=True) + LN_EPS) * ng_ref[...]
    y_ref[0] = (o * sg_ref[0].astype(F32)).astype(BF16)


def _hgrn(hq, hv, lff, lfb, hg, norm_g, batch, seq_len):
    blk = pl.BlockSpec((1, seq_len, HG_DIM), lambda b, h: (b, 0, h))
    r3 = lambda a: a.reshape(batch, seq_len, HG_WIDTH)
    per_dir = [pltpu.VMEM((seq_len, HG_DIM), F32), pltpu.VMEM((seq_len, HG_DIM), BF16),
               pltpu.VMEM((seq_len, HG_DIM), F32), pltpu.VMEM((seq_len // HG_CHUNK, HG_DIM, HG_DIM), F32)]
    scratch = [s for pair in zip(per_dir, per_dir) for s in pair]
    return pl.pallas_call(
        _hgrn_kernel,
        grid=(batch, HG_HEADS),
        in_specs=[blk, blk, blk, blk, blk, pl.BlockSpec((1, HG_DIM), lambda b, h: (0, h))],
        out_specs=blk,
        out_shape=jax.ShapeDtypeStruct((batch, seq_len, HG_WIDTH), BF16),
        scratch_shapes=scratch,
        compiler_params=pltpu.CompilerParams(dimension_semantics=("parallel", "parallel"),
                                             vmem_limit_bytes=VMEM_LIMIT),
        name="hgrn",
    )(r3(hq), r3(hv), r3(lff), r3(lfb), r3(hg), norm_g.reshape(1, HG_WIDTH))


def _attn_kernel(q_ref, k_ref, v_ref, o_ref, lse_ref, *scratch, m_len, dil, qb, kw, span):
    lane = lax.broadcasted_iota(jnp.int32, (1, LANES), 1)
    head0 = lane < ATT_HEAD_DIM
    qi_l = lax.broadcasted_iota(jnp.int32, (qb, kw), 0)
    ki_l = lax.broadcasted_iota(jnp.int32, (qb, kw), 1)

    def run_class(load_q, load_k, load_v, store_o, store_l):
        def block(i, carry):
            q0 = pl.multiple_of(i * qb, qb)
            ks = pl.multiple_of(jnp.clip(q0 - span, 0, m_len - kw), span)
            mask = jnp.abs((q0 + qi_l) - (ks + ki_l)) <= span
            slabs = [slice(sl * LANES, (sl + 1) * LANES) for sl in range(ATT_OUT // LANES)]
            heads = (head0, jnp.logical_not(head0))
            qkv = [(load_q(q0, qb, cols), load_k(ks, kw, cols), load_v(ks, kw, cols)) for cols in slabs]
            scores = [lax.dot_general(jnp.where(hm, q2, jnp.zeros_like(q2)), k2, NT_DIMS, preferred_element_type=F32)
                      for (q2, k2, _) in qkv for hm in heads]
            probs, dens, lses = [], [], []
            for s in scores:
                s = jnp.where(mask, s, NEG_INF)
                mx = jnp.max(s, axis=-1, keepdims=True)
                e = jnp.exp(s - mx)
                den = jnp.sum(e, axis=-1, keepdims=True)
                probs.append(e.astype(BF16))
                dens.append(den)
                lses.append(mx + jnp.log(den))
            outs = [jnp.dot(pr, qkv[c // 2][2], preferred_element_type=F32) / dens[c] for c, pr in enumerate(probs)]
            for sl, cols in enumerate(slabs):
                store_o(q0, qb, cols, jnp.where(head0, outs[2 * sl], outs[2 * sl + 1]))
                store_l(q0, qb, cols, jnp.where(head0, lses[2 * sl], lses[2 * sl + 1]))
            return carry

        lax.fori_loop(0, m_len // qb, block, 0, unroll=2 if m_len // qb > 1 else 1)

    def loader(ref):
        return lambda r0, n, cols: ref[0, pl.ds(r0, n), cols]

    if dil == 1:
        def store_o(r0, n, cols, val):
            o_ref[0, pl.ds(r0, n), cols] = val.astype(BF16)

        def store_l(r0, n, cols, val):
            lse_ref[0, pl.ds(r0, n), cols] = val

        run_class(loader(q_ref), loader(k_ref), loader(v_ref), store_o, store_l)
        return

    q32, k32, v32, o32, l32, qc, kc, vc, oc, lc = scratch
    n_slab = ATT_OUT // LANES
    for slab in range(n_slab):
        cols = slice(slab * LANES, (slab + 1) * LANES)
        q32[slab] = q_ref[0, :, cols].astype(F32)
        k32[slab] = k_ref[0, :, cols].astype(F32)
        v32[slab] = v_ref[0, :, cols].astype(F32)

    def cls_loader(ref):
        return lambda r0, n, cols: ref[pl.ds(r0, n), cols]

    def store_oc(r0, n, cols, val):
        oc[pl.ds(r0, n), cols] = val

    def store_lc(r0, n, cols, val):
        lc[pl.ds(r0, n), cols] = val

    for r in range(dil):
        rows = pl.ds(r, m_len, stride=dil)
        for slab in range(n_slab):
            cols = slice(slab * LANES, (slab + 1) * LANES)
            qc[:, cols] = q32[slab, rows, :].astype(BF16)
            kc[:, cols] = k32[slab, rows, :].astype(BF16)
            vc[:, cols] = v32[slab, rows, :].astype(BF16)
        run_class(cls_loader(qc), cls_loader(kc), cls_loader(vc), store_oc, store_lc)
        for slab in range(n_slab):
            cols = slice(slab * LANES, (slab + 1) * LANES)
            o32[slab, rows, :] = oc[:, cols]
            l32[slab, rows, :] = lc[:, cols]
    for slab in range(n_slab):
        cols = slice(slab * LANES, (slab + 1) * LANES)
        o_ref[0, :, cols] = o32[slab].astype(BF16)
        lse_ref[0, :, cols] = l32[slab]


def _attn_group(aq, ak, av, g, batch, seq_len):
    window, dil = DIL_PAIRS[g]
    span = window // (2 * dil)
    m_len = seq_len // dil
    qb = min(2 * span, m_len)
    kw = min(qb + 2 * span, m_len)
    view = lambda a: a.reshape(batch, seq_len, ATT_WIDTH)
    in_blk = pl.BlockSpec((1, seq_len, ATT_OUT), lambda b: (b, 0, g))
    out_blk = pl.BlockSpec((1, seq_len, ATT_OUT), lambda b: (b, 0, 0))
    scratch = []
    if dil > 1:
        scratch = ([pltpu.VMEM((ATT_OUT // LANES, seq_len, LANES), F32)] * 5 + [pltpu.VMEM((m_len, ATT_OUT), BF16)] * 3
                   + [pltpu.VMEM((m_len, ATT_OUT), F32)] * 2)
    o, lse = pl.pallas_call(
        functools.partial(_attn_kernel, m_len=m_len, dil=dil, qb=qb, kw=kw, span=span),
        grid=(batch,),
        in_specs=[in_blk, in_blk, in_blk],
        out_specs=[out_blk, out_blk],
        out_shape=[jax.ShapeDtypeStruct((batch, seq_len, ATT_OUT), BF16),
                   jax.ShapeDtypeStruct((batch, seq_len, ATT_OUT), F32)],
        scratch_shapes=scratch,
        compiler_params=pltpu.CompilerParams(dimension_semantics=("parallel",), vmem_limit_bytes=VMEM_LIMIT),
        name=f"attn_g{g}",
    )(view(aq), view(ak), view(av))
    T = batch * seq_len
    return o.reshape(T, ATT_OUT), lse.reshape(T, ATT_OUT)


def _mix_kernel(x_ref, yh_ref, o0_ref, o1_ref, o2_ref, l0_ref, l1_ref, l2_ref, gh_ref, ga_ref,
                wbh_ref, wba_ref, wo_ref, g1_ref, b1_ref, wrt_ref, rb_ref,
                x1_ref, x1r_ref, tope_ref, gate_ref, rank_ref, cnt_ref, carry_scr, *, alpha):
    tm = x_ref.shape[0]

    @pl.when(pl.program_id(0) == 0)
    def _():
        carry_scr[...] = jnp.zeros_like(carry_scr)

    l0, l1, l2 = l0_ref[...], l1_ref[...], l2_ref[...]
    lm = jnp.maximum(jnp.maximum(l0, l1), l2)
    e0, e1, e2 = jnp.exp(l0 - lm), jnp.exp(l1 - lm), jnp.exp(l2 - lm)
    es = e0 + e1 + e2
    y_a = ((e0 / es) * o0_ref[...].astype(F32) + (e1 / es) * o1_ref[...].astype(F32)
           + (e2 / es) * o2_ref[...].astype(F32)).astype(BF16)
    merged = (gh_ref[...].astype(F32) * jnp.dot(yh_ref[...], wbh_ref[...], preferred_element_type=F32)
              + ga_ref[...].astype(F32) * jnp.dot(y_a, wba_ref[...], preferred_element_type=F32))
    mixed = jnp.dot(merged.astype(BF16), wo_ref[...], preferred_element_type=F32)
    x1 = _layer_norm(alpha * x_ref[...] + mixed, g1_ref[...], b1_ref[...])
    x1_ref[...] = x1
    x1r_ref[:, 0, :] = x1

    logit = lax.dot_general(wrt_ref[...], x1.astype(BF16), NT_DIMS, preferred_element_type=F32)
    score = _sigmoid(logit)
    biased = score + rb_ref[...]
    b3 = biased.reshape(N_GROUPS, PER_GROUP, tm)
    i3 = lax.broadcasted_iota(jnp.int32, (N_GROUPS, PER_GROUP, tm), 1)
    m1 = jnp.max(b3, axis=1, keepdims=True)
    idx1 = jnp.min(jnp.where(b3 == m1, i3, PER_GROUP), axis=1, keepdims=True)
    m2 = jnp.max(jnp.where(i3 == idx1, -jnp.inf, b3), axis=1, keepdims=True)
    gs = (m1 + m2).reshape(N_GROUPS, tm)
    gi = lax.broadcasted_iota(jnp.int32, (N_GROUPS, tm), 0)
    sel = jnp.zeros((N_GROUPS, tm), F32)
    cur = gs
    for _ in range(TOPK_GROUPS):
        m = jnp.max(cur, axis=0, keepdims=True)
        ix = jnp.min(jnp.where(cur == m, gi, N_GROUPS), axis=0, keepdims=True)
        hit = gi == ix
        sel = jnp.where(hit, 1.0, sel)
        cur = jnp.where(hit, -jnp.inf, cur)
    cur = jnp.where(sel.reshape(N_GROUPS, 1, tm) > 0.0, b3, -jnp.inf).reshape(N_EXPERTS, tm)
    ei = lax.broadcasted_iota(jnp.int32, (N_EXPERTS, tm), 0)
    chosen = jnp.zeros((N_EXPERTS, tm), F32)
    top_e, top_s = [], []
    for _ in range(TOP_K):
        m = jnp.max(cur, axis=0, keepdims=True)
        ix = jnp.min(jnp.where(cur == m, ei, N_EXPERTS), axis=0, keepdims=True)
        hit = ei == ix
        top_e.append(ix)
        top_s.append(jnp.sum(jnp.where(hit, score, 0.0), axis=0, keepdims=True))
        chosen = jnp.where(hit, 1.0, chosen)
        cur = jnp.where(hit, -jnp.inf, cur)
    s_sum = top_s[0]
    for s in top_s[1:]:
        s_sum = s_sum + s
    tope_ref[...] = jnp.concatenate(top_e, axis=0)
    gate_ref[...] = jnp.concatenate([s / s_sum * ROUTED_SCALE for s in top_s], axis=0)

    su = lax.broadcasted_iota(jnp.int32, (tm, tm), 0)
    tu = lax.broadcasted_iota(jnp.int32, (tm, tm), 1)
    before = (su < tu).astype(BF16)
    prior = carry_scr[:, 0:1] + jnp.dot(chosen.astype(BF16), before, preferred_element_type=F32)
    ranks = [jnp.sum(jnp.where(ei == ix, prior, 0.0), axis=0, keepdims=True) for ix in top_e]
    rank_ref[...] = jnp.concatenate(ranks, axis=0).astype(jnp.int32)
    carry_scr[...] = carry_scr[...] + jnp.sum(chosen, axis=1, keepdims=True)
    cnt_ref[...] = carry_scr[...]


def _mix(x2d, y_h, att_o, att_l, gh, ga, wbh, wba, wo, g1, b1, wrt, rb, alpha):
    T = x2d.shape[0]
    tm = 256
    row = lambda w: pl.BlockSpec((tm, w), lambda i: (i, 0))
    col = pl.BlockSpec((TOP_K, tm), lambda i: (0, i))
    return pl.pallas_call(
        functools.partial(_mix_kernel, alpha=alpha),
        grid=(T // tm,),
        in_specs=[row(D_MODEL), row(HG_WIDTH)] + [row(ATT_OUT)] * 6 + [row(D_MODEL), row(D_MODEL),
                  _const_spec((HG_WIDTH, D_MODEL)), _const_spec((ATT_OUT, D_MODEL)),
                  _const_spec((D_MODEL, D_MODEL)), _const_spec((1, D_MODEL)), _const_spec((1, D_MODEL)),
                  _const_spec((N_EXPERTS, D_MODEL)), _const_spec((N_EXPERTS, 1))],
        out_specs=[row(D_MODEL), pl.BlockSpec((tm, 1, D_MODEL), lambda i: (i, 0, 0)), col, col, col,
                   _const_spec((N_EXPERTS, LANES))],
        out_shape=[jax.ShapeDtypeStruct((T, D_MODEL), F32),
                   jax.ShapeDtypeStruct((T, 1, D_MODEL), F32),
                   jax.ShapeDtypeStruct((TOP_K, T), jnp.int32),
                   jax.ShapeDtypeStruct((TOP_K, T), F32),
                   jax.ShapeDtypeStruct((TOP_K, T), jnp.int32),
                   jax.ShapeDtypeStruct((N_EXPERTS, LANES), F32)],
        scratch_shapes=[pltpu.VMEM((N_EXPERTS, LANES), F32)],
        compiler_params=pltpu.CompilerParams(dimension_semantics=("arbitrary",), vmem_limit_bytes=VMEM_LIMIT),
        name="mix",
    )(x2d, y_h, *att_o, *att_l, gh, ga, wbh, wba, wo, g1, b1, wrt, rb)


def _dest_kernel(tope_ref, rank_ref, ps_ref, dest_ref):
    tb = tope_ref.shape[1]
    ei = lax.broadcasted_iota(jnp.int32, (N_EXPERTS, tb), 0)
    ps = ps_ref[...]
    starts = [jnp.sum(jnp.where(ei == tope_ref[k:k + 1, :], ps, 0.0), axis=0, keepdims=True)
              for k in range(TOP_K)]
    dest_ref[...] = jnp.concatenate(starts, axis=0).astype(jnp.int32) + rank_ref[...]


def _dest(top_e, rank, padded_start):
    T = top_e.shape[1]
    tb = 512
    col = pl.BlockSpec((TOP_K, tb), lambda i: (0, i))
    return pl.pallas_call(
        _dest_kernel,
        grid=(T // tb,),
        in_specs=[col, col, _const_spec((N_EXPERTS, 1))],
        out_specs=col,
        out_shape=jax.ShapeDtypeStruct((TOP_K, T), jnp.int32),
        compiler_params=pltpu.CompilerParams(dimension_semantics=("parallel",)),
        name="dest",
    )(top_e, rank, padded_start.astype(F32).reshape(N_EXPERTS, 1))


def _dispatch_kernel(dest_ref, x_ref, xs_hbm, sem):
    tb = x_ref.shape[0]

    def issue(g, c):
        t0 = pl.multiple_of(g * SUBLANES, SUBLANES)
        for j in range(SUBLANES):
            for k in range(TOP_K):
                pltpu.make_async_copy(x_ref.at[t0 + j], xs_hbm.at[dest_ref[k, t0 + j]], sem).start(priority=k % 2)
        return c

    lax.fori_loop(0, tb // SUBLANES, issue, 0)
    for _ in range(TOP_K):
        pltpu.make_async_copy(x_ref, xs_hbm.at[pl.ds(0, tb)], sem).wait()


def _dispatch(dest, x1, n_slots):
    T = x1.shape[0]
    tb = 512
    return pl.pallas_call(
        _dispatch_kernel,
        grid=(T // tb,),
        in_specs=[pl.BlockSpec((TOP_K, tb), lambda i: (0, i), memory_space=pltpu.SMEM),
                  pl.BlockSpec((tb, 1, D_MODEL), lambda i: (i, 0, 0))],
        out_specs=pl.BlockSpec(memory_space=pl.ANY),
        out_shape=jax.ShapeDtypeStruct((n_slots, 1, D_MODEL), F32),
        scratch_shapes=[pltpu.SemaphoreType.DMA],
        compiler_params=pltpu.CompilerParams(dimension_semantics=("arbitrary",)),
        name="dispatch",
    )(dest, x1)


EXPERT_SUBBLOCKS = 2
N_IN_BUF = 3


def _expert_kernel(be_ref, bv_ref, nu_ref, xs_hbm, wg_ref, wu_ref, wd_ref, ys_hbm,
                   xbuf, ybuf, wg_b, wu_b, wd_b, in_sem, out_sem):
    i = pl.program_id(0)
    n_used = nu_ref[0]
    slot = i % 2
    in_slot = i % N_IN_BUF

    def live_rows(blk):
        return pl.multiple_of((bv_ref[blk] + SUBLANES - 1) // SUBLANES * SUBLANES, SUBLANES)

    def in_copy(blk, s):
        n = live_rows(blk)
        return pltpu.make_async_copy(xs_hbm.at[pl.ds(blk * SLOT_BLOCK, n), 0, :], xbuf.at[s, pl.ds(0, n), :], in_sem.at[s])

    def out_copy(blk, s):
        n = live_rows(blk)
        return pltpu.make_async_copy(ybuf.at[s, pl.ds(0, n), :], ys_hbm.at[pl.ds(blk * SLOT_BLOCK, n), 0, :], out_sem.at[s])

    @pl.when(i == 0)
    def _():
        for b in range(N_IN_BUF - 1):
            @pl.when(b < n_used)
            def _():
                in_copy(b, b).start()

    @pl.when(i < n_used)
    def _():
        ahead = i + N_IN_BUF - 1

        @pl.when(ahead < n_used)
        def _():
            in_copy(ahead, ahead % N_IN_BUF).start()

        in_copy(i, in_slot).wait()

        prev = be_ref[jnp.maximum(i - 1, 0)]

        @pl.when(jnp.logical_or(i == 0, be_ref[i] != prev))
        def _():
            wg_b[...] = wg_ref[0].astype(BF16)
            wu_b[...] = wu_ref[0].astype(BF16)
            wd_b[...] = wd_ref[0].astype(BF16)

        @pl.when(i >= 2)
        def _():
            out_copy(i - 2, slot).wait()

        sub = SLOT_BLOCK // EXPERT_SUBBLOCKS
        rows = lax.broadcasted_iota(jnp.int32, (sub, 1), 0)
        xbs = [jnp.where(rows + r * sub < bv_ref[i], xbuf[in_slot, pl.ds(r * sub, sub), :], 0.0).astype(BF16)
               for r in range(EXPERT_SUBBLOCKS)]
        hs = [(_silu(jnp.dot(xb, wg_b[...], preferred_element_type=F32))
               * jnp.dot(xb, wu_b[...], preferred_element_type=F32)).astype(BF16) for xb in xbs]
        for r, h in enumerate(hs):
            ybuf[slot, pl.ds(r * sub, sub), :] = jnp.dot(h, wd_b[...], preferred_element_type=F32)
        out_copy(i, slot).start(priority=1)

        @pl.when(i == n_used - 1)
        def _():
            @pl.when(i >= 1)
            def _():
                out_copy(i - 1, 1 - slot).wait()

            out_copy(i, slot).wait()


def _experts(block_expert, block_valid, n_used, xs, w_gate, w_up, w_down):
    n_blocks = xs.shape[0] // SLOT_BLOCK
    return pl.pallas_call(
        _expert_kernel,
        grid_spec=pltpu.PrefetchScalarGridSpec(
            num_scalar_prefetch=3,
            grid=(n_blocks,),
            in_specs=[pl.BlockSpec(memory_space=pl.ANY),
                      pl.BlockSpec((1, D_MODEL, EXPERT_HIDDEN), lambda i, be, bv, nu: (be[i], 0, 0)),
                      pl.BlockSpec((1, D_MODEL, EXPERT_HIDDEN), lambda i, be, bv, nu: (be[i], 0, 0)),
                      pl.BlockSpec((1, EXPERT_HIDDEN, D_MODEL), lambda i, be, bv, nu: (be[i], 0, 0))],
            out_specs=pl.BlockSpec(memory_space=pl.ANY),
            scratch_shapes=[pltpu.VMEM((N_IN_BUF, SLOT_BLOCK, D_MODEL), F32),
                            pltpu.VMEM((2, SLOT_BLOCK, D_MODEL), F32),
                            pltpu.VMEM((D_MODEL, EXPERT_HIDDEN), BF16),
                            pltpu.VMEM((D_MODEL, EXPERT_HIDDEN), BF16),
                            pltpu.VMEM((EXPERT_HIDDEN, D_MODEL), BF16),
                            pltpu.SemaphoreType.DMA((N_IN_BUF,)), pltpu.SemaphoreType.DMA((2,))],
        ),
        out_shape=jax.ShapeDtypeStruct(xs.shape, F32),
        compiler_params=pltpu.CompilerParams(dimension_semantics=("arbitrary",), vmem_limit_bytes=VMEM_LIMIT),
        name="experts",
    )(block_expert, block_valid, n_used, xs, w_gate, w_up, w_down)


def _combine_kernel(dest_ref, dnext_ref, x1_ref, gate_ref, p_ref, ys_hbm,
                    wsg_ref, wsu_ref, wsd_ref, g2_ref, b2_ref, wpg_ref, wpp_ref, g3_ref, b3_ref,
                    out_ref, *scratch, alpha):
    sem = scratch[-1]
    bufs = (scratch[:TOP_K], scratch[TOP_K:2 * TOP_K])
    tm = bufs[0][0].shape[0]
    i = pl.program_id(0)

    def issue(d_ref, t, k, col0, half):
        pltpu.make_async_copy(ys_hbm.at[d_ref[k, col0 + t]], bufs[half][k].at[t], sem.at[half]).start(priority=k % 2)

    def wait_rows(half):
        for k in range(TOP_K):
            pltpu.make_async_copy(ys_hbm.at[pl.ds(0, tm)], bufs[half][k], sem.at[half]).wait()

    def compute(half):
        rows = pl.ds(half * tm, tm)
        x1 = x1_ref[rows, :]
        xb = x1.astype(BF16)
        hs = (_silu(jnp.dot(xb, wsg_ref[...], preferred_element_type=F32))
              * jnp.dot(xb, wsu_ref[...], preferred_element_type=F32))
        ffn = jnp.dot(hs.astype(BF16), wsd_ref[...], preferred_element_type=F32)
        gate = gate_ref[rows, :]
        for k in range(TOP_K):
            ffn = ffn + gate[:, k:k + 1] * bufs[half][k][:, 0, :]
        x2 = _layer_norm(alpha * x1 + ffn, g2_ref[...], b2_ref[...])
        ple = (_sigmoid(jnp.dot(x2.astype(BF16), wpg_ref[...], preferred_element_type=F32))
               * jnp.dot(p_ref[rows, :].astype(BF16), wpp_ref[...], preferred_element_type=F32))
        out_ref[rows, :] = _layer_norm(alpha * x2 + ple, g3_ref[...], b3_ref[...])

    @pl.when(i == 0)
    def _():
        def first(g, c):
            t0 = pl.multiple_of(g * SUBLANES, SUBLANES)
            for j in range(SUBLANES):
                for k in range(TOP_K):
                    issue(dest_ref, t0 + j, k, 0, 0)
            return c

        lax.fori_loop(0, tm // SUBLANES, first, 0)

    wait_rows(0)
    for t in range(tm):
        for k in range(TOP_K):
            issue(dest_ref, t, k, tm, 1)
    compute(0)
    wait_rows(1)
    for t in range(tm):
        for k in range(TOP_K):
            issue(dnext_ref, t, k, 0, 0)
    compute(1)

    @pl.when(i == pl.num_programs(0) - 1)
    def _():
        wait_rows(0)


def _combine(dest, x1, gate_t, p2d, ys, wsg, wsu, wsd, g2, b2, wpg, wpp, g3, b3, alpha):
    T = x1.shape[0]
    tm = 256
    n = T // (2 * tm)
    row = lambda w: pl.BlockSpec((2 * tm, w), lambda i: (i, 0))
    return pl.pallas_call(
        functools.partial(_combine_kernel, alpha=alpha),
        grid=(n,),
        in_specs=[pl.BlockSpec((TOP_K, 2 * tm), lambda i: (0, i), memory_space=pltpu.SMEM),
                  pl.BlockSpec((TOP_K, tm), lambda i: (0, jnp.minimum(2 * i + 2, 2 * n - 1)), memory_space=pltpu.SMEM),
                  row(D_MODEL), row(TOP_K), row(PLE_DIM),
                  pl.BlockSpec(memory_space=pl.ANY),
                  _const_spec((D_MODEL, SHARED_HIDDEN)), _const_spec((D_MODEL, SHARED_HIDDEN)),
                  _const_spec((SHARED_HIDDEN, D_MODEL)), _const_spec((1, D_MODEL)), _const_spec((1, D_MODEL)),
                  _const_spec((D_MODEL, D_MODEL)), _const_spec((PLE_DIM, D_MODEL)),
                  _const_spec((1, D_MODEL)), _const_spec((1, D_MODEL))],
        out_specs=row(D_MODEL),
        out_shape=jax.ShapeDtypeStruct((T, D_MODEL), F32),
        scratch_shapes=[pltpu.VMEM((tm, 1, D_MODEL), F32)] * (2 * TOP_K) + [pltpu.SemaphoreType.DMA((2,))],
        compiler_params=pltpu.CompilerParams(dimension_semantics=("arbitrary",), vmem_limit_bytes=VMEM_LIMIT),
        name="combine",
    )(dest, dest, x1, gate_t, p2d, ys, wsg, wsu, wsd, g2, b2, wpg, wpp, g3, b3)


def _rotary_tables(seq_len):
    half = ROT_DIM // 2
    inv_freq = ROPE_THETA ** (-jnp.arange(half, dtype=F32) / half)
    ang = jnp.arange(seq_len, dtype=F32)[:, None] * inv_freq[None, :]
    cos, sin = jnp.cos(ang), jnp.sin(ang)
    pad = ATT_HEAD_DIM - ROT_DIM
    one = jnp.ones((seq_len, pad), F32)
    zero = jnp.zeros((seq_len, pad + half), F32)
    rc = jnp.concatenate([cos, cos, one], axis=1)
    rsa = jnp.concatenate([-sin, zero], axis=1)
    rsb = jnp.concatenate([jnp.zeros((seq_len, half), F32), sin, jnp.zeros((seq_len, pad), F32)], axis=1)
    rep = lambda a: jnp.tile(a, (1, LANES // ATT_HEAD_DIM))
    return rep(rc), rep(rsa), rep(rsb)


def _slot_layout(counts, n_blocks):
    padded = (counts + SLOT_BLOCK - 1) // SLOT_BLOCK * SLOT_BLOCK
    padded_end = jnp.cumsum(padded)
    padded_start = padded_end - padded
    blk0 = jnp.arange(n_blocks, dtype=jnp.int32) * SLOT_BLOCK
    block_expert = jnp.minimum(jnp.sum(blk0[:, None] >= padded_end[None, :], axis=1), N_EXPERTS - 1).astype(jnp.int32)
    n_used = (padded_end[-1] // SLOT_BLOCK).astype(jnp.int32)
    own = block_expert[:, None] == jnp.arange(N_EXPERTS, dtype=jnp.int32)[None, :]
    pick = lambda tab: jnp.sum(jnp.where(own, tab[None, :], 0), axis=1)
    valid = jnp.clip(pick(counts) - (blk0 - pick(padded_start)), 0, SLOT_BLOCK)
    block_valid = jnp.where(jnp.arange(n_blocks) < n_used, valid, 0).astype(jnp.int32)
    return padded_start.astype(jnp.int32), block_expert, block_valid, n_used.reshape(1)


def kernel(x, p, w_in, hg_lb_fwd, hg_lb_bwd, hg_norm_g, w_branch_hg, w_branch_att, w_out, ln1_g, ln1_b, w_router, router_bias, w_exp_gate, w_exp_up, w_exp_down, w_sh_gate, w_sh_up, w_sh_down, ln2_g, ln2_b, w_ple_gate, w_ple_proj, ln3_g, ln3_b):
    B, L, D = x.shape
    depth = w_in.shape[0]
    T = B * L
    alpha = (2 * depth) ** 0.25
    n_assign = T * TOP_K
    n_blocks = (n_assign + N_EXPERTS * (SLOT_BLOCK - 1) + SLOT_BLOCK - 1) // SLOT_BLOCK
    lb_fwd_all = jnp.cumsum(jax.nn.softmax(hg_lb_fwd.astype(F32), axis=0), axis=0)
    lb_bwd_all = jnp.cumsum(jax.nn.softmax(hg_lb_bwd.astype(F32), axis=0), axis=0)
    rc, rsa, rsb = _rotary_tables(L)
    row = lambda a: a.reshape(1, -1)

    x2d = x.reshape(T, D)
    for i in range(depth):
        (hq, hv, lff, lfb, hg, aq, ak, av, gh, ga) = _in_proj(
            x2d, w_in[i].astype(BF16), row(lb_fwd_all[i]), row(lb_bwd_all[i]), rc, rsa, rsb, L)
        y_h = _hgrn(hq, hv, lff, lfb, hg, hg_norm_g[i], B, L).reshape(T, HG_WIDTH)
        att = [_attn_group(aq, ak, av, g, B, L) for g in range(len(DIL_PAIRS))]
        x1, x1r, top_e, gate, rank, cnt = _mix(
            x2d, y_h, [a[0] for a in att], [a[1] for a in att], gh, ga,
            w_branch_hg[i].astype(BF16), w_branch_att[i].astype(BF16), w_out[i].astype(BF16),
            row(ln1_g[i]), row(ln1_b[i]), w_router[i].T.astype(BF16), router_bias[i].reshape(N_EXPERTS, 1), alpha)
        counts = cnt[:, 0].astype(jnp.int32)
        padded_start, block_expert, block_valid, n_used = _slot_layout(counts, n_blocks)
        dest = _dest(top_e, rank, padded_start)
        xs = _dispatch(dest, x1r, n_blocks * SLOT_BLOCK)
        ys = _experts(block_expert, block_valid, n_used, xs, w_exp_gate[i], w_exp_up[i], w_exp_down[i])
        x2d = _combine(dest, x1, gate.T, p[i].reshape(T, PLE_DIM), ys,
                       w_sh_gate[i].astype(BF16), w_sh_up[i].astype(BF16), w_sh_down[i].astype(BF16),
                       row(ln2_g[i]), row(ln2_b[i]), w_ple_gate[i].astype(BF16), w_ple_proj[i].astype(BF16),
                       row(ln3_g[i]), row(ln3_b[i]), alpha)
    return x2d.reshape(B, L, D)
```

```python
import functools

import jax
import jax.numpy as jnp
import numpy as np
from jax import lax
from jax.experimental import pallas as pl
from jax.experimental.pallas import tpu as pltpu

F32 = jnp.float32
BF16 = jnp.bfloat16

D_MODEL = 1024
HG_HEADS = 4
HG_DIM = 128
HG_WIDTH = HG_HEADS * HG_DIM
HG_CHUNK = 64
DIL_PAIRS = ((128, 1), (512, 4), (2048, 16))
ATT_SLOTS = 4
ATT_HEAD_DIM = 64
ATT_WIDTH = len(DIL_PAIRS) * ATT_SLOTS * ATT_HEAD_DIM
ATT_OUT = ATT_SLOTS * ATT_HEAD_DIM
ROT_DIM = ATT_HEAD_DIM // 4
ROPE_THETA = 500000.0
COL_SIZES = (HG_WIDTH,) * 5 + (ATT_WIDTH,) * 3 + (D_MODEL,) * 2
COL_STARTS = tuple(int(v) for v in np.cumsum((0,) + COL_SIZES)[:-1])
IN_COLS = sum(COL_SIZES)
N_EXPERTS = 256
TOP_K = 8
N_GROUPS = 8
TOPK_GROUPS = 4
PER_GROUP = N_EXPERTS // N_GROUPS
EXPERT_HIDDEN = 256
SHARED_HIDDEN = 256
ROUTED_SCALE = 2.5
SLOT_BLOCK = 1024
PLE_DIM = 256
LN_EPS = 1e-5
NEG_INF = -1e30

LANES = 128
SUBLANES = 8
VMEM_LIMIT = 56 * 1024 * 1024

NT_DIMS = (((1,), (1,)), ((), ()))
TN_DIMS = (((0,), (0,)), ((), ()))


def _sigmoid(v):
    return jax.nn.sigmoid(v)


def _silu(v):
    return v * jax.nn.sigmoid(v)


def _layer_norm(v, g, b):
    mu = jnp.mean(v, axis=-1, keepdims=True)
    vc = v - mu
    var = jnp.mean(vc * vc, axis=-1, keepdims=True)
    return vc * lax.rsqrt(var + LN_EPS) * g + b


def _const_spec(shape):
    return pl.BlockSpec(shape, lambda *_: (0,) * len(shape))


def _in_proj_kernel(x_ref, w_ref, lbf_ref, lbb_ref, rc_ref, rsa_ref, rsb_ref,
                    hq_ref, hv_ref, lff_ref, lfb_ref, hg_ref,
                    aq_ref, ak_ref, av_ref, gh_ref, ga_ref):
    xb = x_ref[...].astype(BF16)

    def proj(seg):
        c0, width = COL_STARTS[seg], COL_SIZES[seg]
        return jnp.dot(xb, w_ref[:, c0:c0 + width], preferred_element_type=F32)

    hq_ref[...] = _silu(proj(0)).astype(BF16)
    hv_ref[...] = proj(1).astype(BF16)
    lb = lbf_ref[...]
    lff_ref[...] = jnp.log(lb + (1.0 - lb) * _sigmoid(proj(2)))
    lb = lbb_ref[...]
    lfb_ref[...] = jnp.log(lb + (1.0 - lb) * _sigmoid(proj(3)))
    hg_ref[...] = _silu(proj(4)).astype(BF16)

    rc, rsa, rsb = rc_ref[...], rsa_ref[...], rsb_ref[...]

    def rotary(z, out_ref, scale):
        for s in range(ATT_WIDTH // LANES):
            t = z[:, s * LANES:(s + 1) * LANES]
            up = pltpu.roll(t, LANES - ROT_DIM // 2, 1)
            dn = pltpu.roll(t, ROT_DIM // 2, 1)
            r = t * rc + up * rsa + dn * rsb
            out_ref[:, s * LANES:(s + 1) * LANES] = (r * scale).astype(BF16)

    rotary(proj(5), aq_ref, ATT_HEAD_DIM ** -0.5)
    rotary(proj(6), ak_ref, 1.0)
    av_ref[...] = proj(7).astype(BF16)
    gh_ref[...] = _sigmoid(proj(8)).astype(BF16)
    ga_ref[...] = _sigmoid(proj(9)).astype(BF16)


def _in_proj(x2d, w_b, lbf, lbb, rc, rsa, rsb, seq_len):
    T = x2d.shape[0]
    tm = 512
    n_pos_blocks = seq_len // tm
    row = lambda w: pl.BlockSpec((tm, w), lambda i: (i, 0))
    tab = pl.BlockSpec((tm, LANES), lambda i: (i % n_pos_blocks, 0))
    out_w = (HG_WIDTH, HG_WIDTH, HG_WIDTH, HG_WIDTH, HG_WIDTH, ATT_WIDTH, ATT_WIDTH, ATT_WIDTH, D_MODEL, D_MODEL)
    out_dt = (BF16, BF16, F32, F32, BF16, BF16, BF16, BF16, BF16, BF16)
    return pl.pallas_call(
        _in_proj_kernel,
        grid=(T // tm,),
        in_specs=[row(D_MODEL),
                  pl.BlockSpec((D_MODEL, IN_COLS), lambda i: (0, 0), pipeline_mode=pl.Buffered(1)),
                  _const_spec((1, HG_WIDTH)), _const_spec((1, HG_WIDTH)), tab, tab, tab],
        out_specs=[row(w) for w in out_w],
        out_shape=[jax.ShapeDtypeStruct((T, w), dt) for w, dt in zip(out_w, out_dt)],
        compiler_params=pltpu.CompilerParams(dimension_semantics=("parallel",), vmem_limit_bytes=VMEM_LIMIT),
        name="in_proj",
    )(x2d, w_b, lbf, lbb, rc, rsa, rsb)


def _split3(a):
    hi = a.astype(BF16)
    r1 = a - hi.astype(F32)
    mid = r1.astype(BF16)
    lo = (r1 - mid.astype(F32)).astype(BF16)
    return jnp.concatenate([hi, mid, lo], axis=1)


HG_SUPER = 4 * HG_CHUNK


def _hgrn_kernel(q_ref, v_ref, lff_ref, lfb_ref, sg_ref, ng_ref, y_ref,
                 of_scr, ob_scr, qf_scr, qb_scr, df_scr, db_scr, mf_scr, mb_scr):
    L = q_ref.shape[1]
    C, SC = HG_CHUNK, HG_SUPER
    n = L // C
    r_i = lax.broadcasted_iota(jnp.int32, (SC, SC), 0)
    c_i = lax.broadcasted_iota(jnp.int32, (SC, SC), 1)
    same = (r_i // C) == (c_i // C)
    r_l, c_l = r_i % C, c_i % C
    mask_f = jnp.logical_and(same, c_l <= r_l)
    mask_b = jnp.logical_and(same, c_l >= r_l)
    tri_f = mask_f.astype(BF16)
    tri_b = mask_b.astype(BF16)

    def chunk_rows(b, row):
        return jnp.concatenate([jnp.broadcast_to(b[c * C + row:c * C + row + 1, :], (C, HG_DIM))
                                for c in range(SC // C)], axis=0)

    dirs = ((lff_ref, tri_f, mask_f, C // 2 - 1, C - 1, of_scr, qf_scr, df_scr, mf_scr),
            (lfb_ref, tri_b, mask_b, C // 2, 0, ob_scr, qb_scr, db_scr, mb_scr))

    def local_body(i, carry):
        i0 = pl.multiple_of(i * SC, SC)
        rows = pl.ds(i0, SC)
        q = q_ref[0, rows, :].astype(F32)
        v = v_ref[0, rows, :]
        lfs = [d[0][0, rows, :] for d in dirs]
        sums = [jnp.dot(d[1], _split3(lf), preferred_element_type=F32) for d, lf in zip(dirs, lfs)]
        a_qs, a_ks, k_outs = [], [], []
        for d, lf, p in zip(dirs, lfs, sums):
            b = p[:, :HG_DIM] + p[:, HG_DIM:2 * HG_DIM] + p[:, 2 * HG_DIM:]
            b_ref, b_last = chunk_rows(b, d[3]), chunk_rows(b, d[4])
            k = 1.0 - jnp.exp(lf)
            a_qs.append((q * jnp.exp(b - b_ref)).astype(BF16))
            a_ks.append((k * jnp.exp(b_ref - b)).astype(BF16))
            k_outs.append((k * jnp.exp(b_last - b)).astype(BF16))
            d[6][rows, :] = (q * jnp.exp(b)).astype(BF16)
            d[7][rows, :] = b_last
        scores = [lax.dot_general(a_q, a_k, NT_DIMS, preferred_element_type=F32) for a_q, a_k in zip(a_qs, a_ks)]
        states = [[lax.dot_general(v[c * C:(c + 1) * C], k_out[c * C:(c + 1) * C], TN_DIMS,
                                   preferred_element_type=F32) for c in range(SC // C)] for k_out in k_outs]
        masked = [jnp.where(d[2], s, 0.0).astype(BF16) for d, s in zip(dirs, scores)]
        intra = [jnp.dot(m, v, preferred_element_type=F32) for m in masked]
        for d, o, st in zip(dirs, intra, states):
            d[5][rows, :] = o
            for c in range(SC // C):
                d[8][i0 // C + c] = st[c]
        return carry

    lax.fori_loop(0, L // SC, local_body, 0, unroll=2)

    def carry_part(c, st, o_scr, qin_scr, dec_scr, m_scr):
        rows = pl.ds(pl.multiple_of(c * C, C), C)
        o_scr[rows, :] = o_scr[rows, :] + lax.dot_general(qin_scr[rows, :], st.astype(BF16), NT_DIMS,
                                                          preferred_element_type=F32)
        return st * jnp.exp(dec_scr[pl.ds(pl.multiple_of(c * C, C), 1), :]) + m_scr[c]

    def carry_body(c, carry):
        st_f, st_b = carry
        st_f = carry_part(c, st_f, of_scr, qf_scr, df_scr, mf_scr)
        st_b = carry_part(n - 1 - c, st_b, ob_scr, qb_scr, db_scr, mb_scr)
        return st_f, st_b

    z = jnp.zeros((HG_DIM, HG_DIM), F32)
    lax.fori_loop(0, n, carry_body, (z, z), unroll=8)

    o = of_scr[...] + ob_scr[...]
    o = o * lax.rsqrt(jnp.mean(o * o, axis=-1, keepdims=True) + LN_EPS) * ng_ref[...]
    y_ref[0] = (o * sg_ref[0].astype(F32)).astype(BF16)


def _hgrn(hq, hv, lff, lfb, hg, norm_g, batch, seq_len):
    blk = pl.BlockSpec((1, seq_len, HG_DIM), lambda b, h: (b, 0, h))
    r3 = lambda a: a.reshape(batch, seq_len, HG_WIDTH)
    per_dir = [pltpu.VMEM((seq_len, HG_DIM), F32), pltpu.VMEM((seq_len, HG_DIM), BF16),
               pltpu.VMEM((seq_len, HG_DIM), F32), pltpu.VMEM((seq_len // HG_CHUNK, HG_DIM, HG_DIM), F32)]
    scratch = [s for pair in zip(per_dir, per_dir) for s in pair]
    return pl.pallas_call(
        _hgrn_kernel,
        grid=(batch, HG_HEADS),
        in_specs=[blk, blk, blk, blk, blk, pl.BlockSpec((1, HG_DIM), lambda b, h: (0, h))],
        out_specs=blk,
        out_shape=jax.ShapeDtypeStruct((batch, seq_len, HG_WIDTH), BF16),
        scratch_shapes=scratch,
        compiler_params=pltpu.CompilerParams(dimension_semantics=("parallel", "parallel"),
                                             vmem_limit_bytes=VMEM_LIMIT),
        name="hgrn",
    )(r3(hq), r3(hv), r3(lff), r3(lfb), r3(hg), norm_g.reshape(1, HG_WIDTH))


def _attn_kernel(q_ref, k_ref, v_ref, o_ref, lse_ref, *scratch, m_len, dil, qb, kw, span):
    lane = lax.broadcasted_iota(jnp.int32, (1, LANES), 1)
    head0 = lane < ATT_HEAD_DIM
    qi_l = lax.broadcasted_iota(jnp.int32, (qb, kw), 0)
    ki_l = lax.broadcasted_iota(jnp.int32, (qb, kw), 1)

    def run_class(load_q, load_k, load_v, store_o, store_l):
        def block(i, carry):
            q0 = pl.multiple_of(i * qb, qb)
            ks = pl.multiple_of(jnp.clip(q0 - span, 0, m_len - kw), span)
            mask = jnp.abs((q0 + qi_l) - (ks + ki_l)) <= span
            slabs = [slice(sl * LANES, (sl + 1) * LANES) for sl in range(ATT_OUT // LANES)]
            heads = (head0, jnp.logical_not(head0))
            qkv = [(load_q(q0, qb, cols), load_k(ks, kw, cols), load_v(ks, kw, cols)) for cols in slabs]
            scores = [lax.dot_general(jnp.where(hm, q2, jnp.zeros_like(q2)), k2, NT_DIMS, preferred_element_type=F32)
                      for (q2, k2, _) in qkv for hm in heads]
            probs, dens, lses = [], [], []
            for s in scores:
                s = jnp.where(mask, s, NEG_INF)
                mx = jnp.max(s, axis=-1, keepdims=True)
                e = jnp.exp(s - mx)
                den = jnp.sum(e, axis=-1, keepdims=True)
                probs.append(e.astype(BF16))
                dens.append(den)
                lses.append(mx + jnp.log(den))
            outs = [jnp.dot(pr, qkv[c // 2][2], preferred_element_type=F32) / dens[c] for c, pr in enumerate(probs)]
            for sl, cols in enumerate(slabs):
                store_o(q0, qb, cols, jnp.where(head0, outs[2 * sl], outs[2 * sl + 1]))
                store_l(q0, qb, cols, jnp.where(head0, lses[2 * sl], lses[2 * sl + 1]))
            return carry

        lax.fori_loop(0, m_len // qb, block, 0, unroll=2 if m_len // qb > 1 else 1)

    def loader(ref):
        return lambda r0, n, cols: ref[0, pl.ds(r0, n), cols]

    if dil == 1:
        def store_o(r0, n, cols, val):
            o_ref[0, pl.ds(r0, n), cols] = val.astype(BF16)

        def store_l(r0, n, cols, val):
            lse_ref[0, pl.ds(r0, n), cols] = val

        run_class(loader(q_ref), loader(k_ref), loader(v_ref), store_o, store_l)
        return

    q32, k32, v32, o32, l32, qc, kc, vc, oc, lc = scratch
    n_slab = ATT_OUT // LANES
    for slab in range(n_slab):
        cols = slice(slab * LANES, (slab + 1) * LANES)
        q32[slab] = q_ref[0, :, cols].astype(F32)
        k32[slab] = k_ref[0, :, cols].astype(F32)
        v32[slab] = v_ref[0, :, cols].astype(F32)

    def cls_loader(ref):
        return lambda r0, n, cols: ref[pl.ds(r0, n), cols]

    def store_oc(r0, n, cols, val):
        oc[pl.ds(r0, n), cols] = val

    def store_lc(r0, n, cols, val):
        lc[pl.ds(r0, n), cols] = val

    for r in range(dil):
        rows = pl.ds(r, m_len, stride=dil)
        for slab in range(n_slab):
            cols = slice(slab * LANES, (slab + 1) * LANES)
            qc[:, cols] = q32[slab, rows, :].astype(BF16)
            kc[:, cols] = k32[slab, rows, :].astype(BF16)
            vc[:, cols] = v32[slab, rows, :].astype(BF16)
        run_class(cls_loader(qc), cls_loader(kc), cls_loader(vc), store_oc, store_lc)
        for slab in range(n_slab):
            cols = slice(slab * LANES, (slab + 1) * LANES)
            o32[slab, rows, :] = oc[:, cols]
            l32[slab, rows, :] = lc[:, cols]
    for slab in range(n_slab):
        cols = slice(slab * LANES, (slab + 1) * LANES)
        o_ref[0, :, cols] = o32[slab].astype(BF16)
        lse_ref[0, :, cols] = l32[slab]


def _attn_group(aq, ak, av, g, batch, seq_len):
    window, dil = DIL_PAIRS[g]
    span = window // (2 * dil)
    m_len = seq_len // dil
    qb = min(2 * span, m_len)
    kw = min(qb + 2 * span, m_len)
    view = lambda a: a.reshape(batch, seq_len, ATT_WIDTH)
    in_blk = pl.BlockSpec((1, seq_len, ATT_OUT), lambda b: (b, 0, g))
    out_blk = pl.BlockSpec((1, seq_len, ATT_OUT), lambda b: (b, 0, 0))
    scratch = []
    if dil > 1:
        scratch = ([pltpu.VMEM((ATT_OUT // LANES, seq_len, LANES), F32)] * 5 + [pltpu.VMEM((m_len, ATT_OUT), BF16)] * 3
                   + [pltpu.VMEM((m_len, ATT_OUT), F32)] * 2)
    o, lse = pl.pallas_call(
        functools.partial(_attn_kernel, m_len=m_len, dil=dil, qb=qb, kw=kw, span=span),
        grid=(batch,),
        in_specs=[in_blk, in_blk, in_blk],
        out_specs=[out_blk, out_blk],
        out_shape=[jax.ShapeDtypeStruct((batch, seq_len, ATT_OUT), BF16),
                   jax.ShapeDtypeStruct((batch, seq_len, ATT_OUT), F32)],
        scratch_shapes=scratch,
        compiler_params=pltpu.CompilerParams(dimension_semantics=("parallel",), vmem_limit_bytes=VMEM_LIMIT),
        name=f"attn_g{g}",
    )(view(aq), view(ak), view(av))
    T = batch * seq_len
    return o.reshape(T, ATT_OUT), lse.reshape(T, ATT_OUT)


def _mix_kernel(x_ref, yh_ref, o0_ref, o1_ref, o2_ref, l0_ref, l1_ref, l2_ref, gh_ref, ga_ref,
                wbh_ref, wba_ref, wo_ref, g1_ref, b1_ref, wrt_ref, rb_ref,
                x1_ref, x1r_ref, tope_ref, gate_ref, rank_ref, cnt_ref, carry_scr, *, alpha):
    tm = x_ref.shape[0]

    @pl.when(pl.program_id(0) == 0)
    def _():
        carry_scr[...] = jnp.zeros_like(carry_scr)

    l0, l1, l2 = l0_ref[...], l1_ref[...], l2_ref[...]
    lm = jnp.maximum(jnp.maximum(l0, l1), l2)
    e0, e1, e2 = jnp.exp(l0 - lm), jnp.exp(l1 - lm), jnp.exp(l2 - lm)
    es = e0 + e1 + e2
    y_a = ((e0 / es) * o0_ref[...].astype(F32) + (e1 / es) * o1_ref[...].astype(F32)
           + (e2 / es) * o2_ref[...].astype(F32)).astype(BF16)
    merged = (gh_ref[...].astype(F32) * jnp.dot(yh_ref[...], wbh_ref[...], preferred_element_type=F32)
              + ga_ref[...].astype(F32) * jnp.dot(y_a, wba_ref[...], preferred_element_type=F32))
    mixed = jnp.dot(merged.astype(BF16), wo_ref[...], preferred_element_type=F32)
    x1 = _layer_norm(alpha * x_ref[...] + mixed, g1_ref[...], b1_ref[...])
    x1_ref[...] = x1
    x1r_ref[:, 0, :] = x1

    logit = lax.dot_general(wrt_ref[...], x1.astype(BF16), NT_DIMS, preferred_element_type=F32)
    score = _sigmoid(logit)
    biased = score + rb_ref[...]
    b3 = biased.reshape(N_GROUPS, PER_GROUP, tm)
    i3 = lax.broadcasted_iota(jnp.int32, (N_GROUPS, PER_GROUP, tm), 1)
    m1 = jnp.max(b3, axis=1, keepdims=True)
    idx1 = jnp.min(jnp.where(b3 == m1, i3, PER_GROUP), axis=1, keepdims=True)
    m2 = jnp.max(jnp.where(i3 == idx1, -jnp.inf, b3), axis=1, keepdims=True)
    gs = (m1 + m2).reshape(N_GROUPS, tm)
    gi = lax.broadcasted_iota(jnp.int32, (N_GROUPS, tm), 0)
    sel = jnp.zeros((N_GROUPS, tm), F32)
    cur = gs
    for _ in range(TOPK_GROUPS):
        m = jnp.max(cur, axis=0, keepdims=True)
        ix = jnp.min(jnp.where(cur == m, gi, N_GROUPS), axis=0, keepdims=True)
        hit = gi == ix
        sel = jnp.where(hit, 1.0, sel)
        cur = jnp.where(hit, -jnp.inf, cur)
    cur = jnp.where(sel.reshape(N_GROUPS, 1, tm) > 0.0, b3, -jnp.inf).reshape(N_EXPERTS, tm)
    ei = lax.broadcasted_iota(jnp.int32, (N_EXPERTS, tm), 0)
    chosen = jnp.zeros((N_EXPERTS, tm), F32)
    top_e, top_s = [], []
    for _ in range(TOP_K):
        m = jnp.max(cur, axis=0, keepdims=True)
        ix = jnp.min(jnp.where(cur == m, ei, N_EXPERTS), axis=0, keepdims=True)
        hit = ei == ix
        top_e.append(ix)
        top_s.append(jnp.sum(jnp.where(hit, score, 0.0), axis=0, keepdims=True))
        chosen = jnp.where(hit, 1.0, chosen)
        cur = jnp.where(hit, -jnp.inf, cur)
    s_sum = top_s[0]
    for s in top_s[1:]:
        s_sum = s_sum + s
    tope_ref[...] = jnp.concatenate(top_e, axis=0)
    gate_ref[...] = jnp.concatenate([s / s_sum * ROUTED_SCALE for s in top_s], axis=0)

    su = lax.broadcasted_iota(jnp.int32, (tm, tm), 0)
    tu = lax.broadcasted_iota(jnp.int32, (tm, tm), 1)
    before = (su < tu).astype(BF16)
    prior = carry_scr[:, 0:1] + jnp.dot(chosen.astype(BF16), before, preferred_element_type=F32)
    ranks = [jnp.sum(jnp.where(ei == ix, prior, 0.0), axis=0, keepdims=True) for ix in top_e]
    rank_ref[...] = jnp.concatenate(ranks, axis=0).astype(jnp.int32)
    carry_scr[...] = carry_scr[...] + jnp.sum(chosen, axis=1, keepdims=True)
    cnt_ref[...] = carry_scr[...]


def _mix(x2d, y_h, att_o, att_l, gh, ga, wbh, wba, wo, g1, b1, wrt, rb, alpha):
    T = x2d.shape[0]
    tm = 256
    row = lambda w: pl.BlockSpec((tm, w), lambda i: (i, 0))
    col = pl.BlockSpec((TOP_K, tm), lambda i: (0, i))
    return pl.pallas_call(
        functools.partial(_mix_kernel, alpha=alpha),
        grid=(T // tm,),
        in_specs=[row(D_MODEL), row(HG_WIDTH)] + [row(ATT_OUT)] * 6 + [row(D_MODEL), row(D_MODEL),
                  _const_spec((HG_WIDTH, D_MODEL)), _const_spec((ATT_OUT, D_MODEL)),
                  _const_spec((D_MODEL, D_MODEL)), _const_spec((1, D_MODEL)), _const_spec((1, D_MODEL)),
                  _const_spec((N_EXPERTS, D_MODEL)), _const_spec((N_EXPERTS, 1))],
        out_specs=[row(D_MODEL), pl.BlockSpec((tm, 1, D_MODEL), lambda i: (i, 0, 0)), col, col, col,
                   _const_spec((N_EXPERTS, LANES))],
        out_shape=[jax.ShapeDtypeStruct((T, D_MODEL), F32),
                   jax.ShapeDtypeStruct((T, 1, D_MODEL), F32),
                   jax.ShapeDtypeStruct((TOP_K, T), jnp.int32),
                   jax.ShapeDtypeStruct((TOP_K, T), F32),
                   jax.ShapeDtypeStruct((TOP_K, T), jnp.int32),
                   jax.ShapeDtypeStruct((N_EXPERTS, LANES), F32)],
        scratch_shapes=[pltpu.VMEM((N_EXPERTS, LANES), F32)],
        compiler_params=pltpu.CompilerParams(dimension_semantics=("arbitrary",), vmem_limit_bytes=VMEM_LIMIT),
        name="mix",
    )(x2d, y_h, *att_o, *att_l, gh, ga, wbh, wba, wo, g1, b1, wrt, rb)


def _dest_kernel(tope_ref, rank_ref, ps_ref, dest_ref):
    tb = tope_ref.shape[1]
    ei = lax.broadcasted_iota(jnp.int32, (N_EXPERTS, tb), 0)
    ps = ps_ref[...]
    starts = [jnp.sum(jnp.where(ei == tope_ref[k:k + 1, :], ps, 0.0), axis=0, keepdims=True)
              for k in range(TOP_K)]
    dest_ref[...] = jnp.concatenate(starts, axis=0).astype(jnp.int32) + rank_ref[...]


def _dest(top_e, rank, padded_start):
    T = top_e.shape[1]
    tb = 512
    col = pl.BlockSpec((TOP_K, tb), lambda i: (0, i))
    return pl.pallas_call(
        _dest_kernel,
        grid=(T // tb,),
        in_specs=[col, col, _const_spec((N_EXPERTS, 1))],
        out_specs=col,
        out_shape=jax.ShapeDtypeStruct((TOP_K, T), jnp.int32),
        compiler_params=pltpu.CompilerParams(dimension_semantics=("parallel",)),
        name="dest",
    )(top_e, rank, padded_start.astype(F32).reshape(N_EXPERTS, 1))


def _dispatch_kernel(dest_ref, x_ref, xs_hbm, sem):
    tb = x_ref.shape[0]

    def issue(g, c):
        t0 = pl.multiple_of(g * SUBLANES, SUBLANES)
        for j in range(SUBLANES):
            for k in range(TOP_K):
                pltpu.make_async_copy(x_ref.at[t0 + j], xs_hbm.at[dest_ref[k, t0 + j]], sem).start(priority=k % 2)
        return c

    lax.fori_loop(0, tb // SUBLANES, issue, 0)
    for _ in range(TOP_K):
        pltpu.make_async_copy(x_ref, xs_hbm.at[pl.ds(0, tb)], sem).wait()


def _dispatch(dest, x1, n_slots):
    T = x1.shape[0]
    tb = 512
    return pl.pallas_call(
        _dispatch_kernel,
        grid=(T // tb,),
        in_specs=[pl.BlockSpec((TOP_K, tb), lambda i: (0, i), memory_space=pltpu.SMEM),
                  pl.BlockSpec((tb, 1, D_MODEL), lambda i: (i, 0, 0))],
        out_specs=pl.BlockSpec(memory_space=pl.ANY),
        out_shape=jax.ShapeDtypeStruct((n_slots, 1, D_MODEL), F32),
        scratch_shapes=[pltpu.SemaphoreType.DMA],
        compiler_params=pltpu.CompilerParams(dimension_semantics=("arbitrary",)),
        name="dispatch",
    )(dest, x1)


EXPERT_SUBBLOCKS = 2
N_IN_BUF = 3


def _expert_kernel(be_ref, bv_ref, nu_ref, xs_hbm, wg_ref, wu_ref, wd_ref, ys_hbm,
                   xbuf, ybuf, wg_b, wu_b, wd_b, in_sem, out_sem):
    i = pl.program_id(0)
    n_used = nu_ref[0]
    slot = i % 2
    in_slot = i % N_IN_BUF

    def live_rows(blk):
        return pl.multiple_of((bv_ref[blk] + SUBLANES - 1) // SUBLANES * SUBLANES, SUBLANES)

    def in_copy(blk, s):
        n = live_rows(blk)
        return pltpu.make_async_copy(xs_hbm.at[pl.ds(blk * SLOT_BLOCK, n), 0, :], xbuf.at[s, pl.ds(0, n), :], in_sem.at[s])

    def out_copy(blk, s):
        n = live_rows(blk)
        return pltpu.make_async_copy(ybuf.at[s, pl.ds(0, n), :], ys_hbm.at[pl.ds(blk * SLOT_BLOCK, n), 0, :], out_sem.at[s])

    @pl.when(i == 0)
    def _():
        for b in range(N_IN_BUF - 1):
            @pl.when(b < n_used)
            def _():
                in_copy(b, b).start()

    @pl.when(i < n_used)
    def _():
        ahead = i + N_IN_BUF - 1

        @pl.when(ahead < n_used)
        def _():
            in_copy(ahead, ahead % N_IN_BUF).start()

        in_copy(i, in_slot).wait()

        prev = be_ref[jnp.maximum(i - 1, 0)]

        @pl.when(jnp.logical_or(i == 0, be_ref[i] != prev))
        def _():
            wg_b[...] = wg_ref[0].astype(BF16)
            wu_b[...] = wu_ref[0].astype(BF16)
            wd_b[...] = wd_ref[0].astype(BF16)

        @pl.when(i >= 2)
        def _():
            out_copy(i - 2, slot).wait()

        sub = SLOT_BLOCK // EXPERT_SUBBLOCKS
        rows = lax.broadcasted_iota(jnp.int32, (sub, 1), 0)
        xbs = [jnp.where(rows + r * sub < bv_ref[i], xbuf[in_slot, pl.ds(r * sub, sub), :], 0.0).astype(BF16)
               for r in range(EXPERT_SUBBLOCKS)]
        hs = [(_silu(jnp.dot(xb, wg_b[...], preferred_element_type=F32))
               * jnp.dot(xb, wu_b[...], preferred_element_type=F32)).astype(BF16) for xb in xbs]
        for r, h in enumerate(hs):
            ybuf[slot, pl.ds(r * sub, sub), :] = jnp.dot(h, wd_b[...], preferred_element_type=F32)
        out_copy(i, slot).start(priority=1)

        @pl.when(i == n_used - 1)
        def _():
            @pl.when(i >= 1)
            def _():
                out_copy(i - 1, 1 - slot).wait()

            out_copy(i, slot).wait()


def _experts(block_expert, block_valid, n_used, xs, w_gate, w_up, w_down):
    n_blocks = xs.shape[0] // SLOT_BLOCK
    return pl.pallas_call(
        _expert_kernel,
        grid_spec=pltpu.PrefetchScalarGridSpec(
            num_scalar_prefetch=3,
            grid=(n_blocks,),
            in_specs=[pl.BlockSpec(memory_space=pl.ANY),
                      pl.BlockSpec((1, D_MODEL, EXPERT_HIDDEN), lambda i, be, bv, nu: (be[i], 0, 0)),
                      pl.BlockSpec((1, D_MODEL, EXPERT_HIDDEN), lambda i, be, bv, nu: (be[i], 0, 0)),
                      pl.BlockSpec((1, EXPERT_HIDDEN, D_MODEL), lambda i, be, bv, nu: (be[i], 0, 0))],
            out_specs=pl.BlockSpec(memory_space=pl.ANY),
            scratch_shapes=[pltpu.VMEM((N_IN_BUF, SLOT_BLOCK, D_MODEL), F32),
                            pltpu.VMEM((2, SLOT_BLOCK, D_MODEL), F32),
                            pltpu.VMEM((D_MODEL, EXPERT_HIDDEN), BF16),
                            pltpu.VMEM((D_MODEL, EXPERT_HIDDEN), BF16),
                            pltpu.VMEM((EXPERT_HIDDEN, D_MODEL), BF16),
                            pltpu.SemaphoreType.DMA((N_IN_BUF,)), pltpu.SemaphoreType.DMA((2,))],
        ),
        out_shape=jax.ShapeDtypeStruct(xs.shape, F32),
        compiler_params=pltpu.CompilerParams(dimension_semantics=("arbitrary",), vmem_limit_bytes=VMEM_LIMIT),
        name="experts",
    )(block_expert, block_valid, n_used, xs, w_gate, w_up, w_down)


def _combine_kernel(dest_ref, dnext_ref, x1_ref, gate_ref, p_ref, ys_hbm,
                    wsg_ref, wsu_ref, wsd_ref, g2_ref, b2_ref, wpg_ref, wpp_ref, g3_ref, b3_ref,
                    out_ref, *scratch, alpha):
    sem = scratch[-1]
    bufs = (scratch[:TOP_K], scratch[TOP_K:2 * TOP_K])
    tm = bufs[0][0].shape[0]
    i = pl.program_id(0)

    def issue(d_ref, t, k, col0, half):
        pltpu.make_async_copy(ys_hbm.at[d_ref[k, col0 + t]], bufs[half][k].at[t], sem.at[half]).start(priority=k % 2)

    def wait_rows(half):
        for k in range(TOP_K):
            pltpu.make_async_copy(ys_hbm.at[pl.ds(0, tm)], bufs[half][k], sem.at[half]).wait()

    def compute(half):
        rows = pl.ds(half * tm, tm)
        x1 = x1_ref[rows, :]
        xb = x1.astype(BF16)
        hs = (_silu(jnp.dot(xb, wsg_ref[...], preferred_element_type=F32))
              * jnp.dot(xb, wsu_ref[...], preferred_element_type=F32))
        ffn = jnp.dot(hs.astype(BF16), wsd_ref[...], preferred_element_type=F32)
        gate = gate_ref[rows, :]
        for k in range(TOP_K):
            ffn = ffn + gate[:, k:k + 1] * bufs[half][k][:, 0, :]
        x2 = _layer_norm(alpha * x1 + ffn, g2_ref[...], b2_ref[...])
        ple = (_sigmoid(jnp.dot(x2.astype(BF16), wpg_ref[...], preferred_element_type=F32))
               * jnp.dot(p_ref[rows, :].astype(BF16), wpp_ref[...], preferred_element_type=F32))
        out_ref[rows, :] = _layer_norm(alpha * x2 + ple, g3_ref[...], b3_ref[...])

    @pl.when(i == 0)
    def _():
        def first(g, c):
            t0 = pl.multiple_of(g * SUBLANES, SUBLANES)
            for j in range(SUBLANES):
                for k in range(TOP_K):
                    issue(dest_ref, t0 + j, k, 0, 0)
            return c

        lax.fori_loop(0, tm // SUBLANES, first, 0)

    wait_rows(0)
    for t in range(tm):
        for k in range(TOP_K):
            issue(dest_ref, t, k, tm, 1)
    compute(0)
    wait_rows(1)
    for t in range(tm):
        for k in range(TOP_K):
            issue(dnext_ref, t, k, 0, 0)
    compute(1)

    @pl.when(i == pl.num_programs(0) - 1)
    def _():
        wait_rows(0)


def _combine(dest, x1, gate_t, p2d, ys, wsg, wsu, wsd, g2, b2, wpg, wpp, g3, b3, alpha):
    T = x1.shape[0]
    tm = 256
    n = T // (2 * tm)
    row = lambda w: pl.BlockSpec((2 * tm, w), lambda i: (i, 0))
    return pl.pallas_call(
        functools.partial(_combine_kernel, alpha=alpha),
        grid=(n,),
        in_specs=[pl.BlockSpec((TOP_K, 2 * tm), lambda i: (0, i), memory_space=pltpu.SMEM),
                  pl.BlockSpec((TOP_K, tm), lambda i: (0, jnp.minimum(2 * i + 2, 2 * n - 1)), memory_space=pltpu.SMEM),
                  row(D_MODEL), row(TOP_K), row(PLE_DIM),
                  pl.BlockSpec(memory_space=pl.ANY),
                  _const_spec((D_MODEL, SHARED_HIDDEN)), _const_spec((D_MODEL, SHARED_HIDDEN)),
                  _const_spec((SHARED_HIDDEN, D_MODEL)), _const_spec((1, D_MODEL)), _const_spec((1, D_MODEL)),
                  _const_spec((D_MODEL, D_MODEL)), _const_spec((PLE_DIM, D_MODEL)),
                  _const_spec((1, D_MODEL)), _const_spec((1, D_MODEL))],
        out_specs=row(D_MODEL),
        out_shape=jax.ShapeDtypeStruct((T, D_MODEL), F32),
        scratch_shapes=[pltpu.VMEM((tm, 1, D_MODEL), F32)] * (2 * TOP_K) + [pltpu.SemaphoreType.DMA((2,))],
        compiler_params=pltpu.CompilerParams(dimension_semantics=("arbitrary",), vmem_limit_bytes=VMEM_LIMIT),
        name="combine",
    )(dest, dest, x1, gate_t, p2d, ys, wsg, wsu, wsd, g2, b2, wpg, wpp, g3, b3)


def _rotary_tables(seq_len):
    half = ROT_DIM // 2
    inv_freq = ROPE_THETA ** (-jnp.arange(half, dtype=F32) / half)
    ang = jnp.arange(seq_len, dtype=F32)[:, None] * inv_freq[None, :]
    cos, sin = jnp.cos(ang), jnp.sin(ang)
    pad = ATT_HEAD_DIM - ROT_DIM
    one = jnp.ones((seq_len, pad), F32)
    zero = jnp.zeros((seq_len, pad + half), F32)
    rc = jnp.concatenate([cos, cos, one], axis=1)
    rsa = jnp.concatenate([-sin, zero], axis=1)
    rsb = jnp.concatenate([jnp.zeros((seq_len, half), F32), sin, jnp.zeros((seq_len, pad), F32)], axis=1)
    rep = lambda a: jnp.tile(a, (1, LANES // ATT_HEAD_DIM))
    return rep(rc), rep(rsa), rep(rsb)


def _slot_layout(counts, n_blocks):
    padded = (counts + SLOT_BLOCK - 1) // SLOT_BLOCK * SLOT_BLOCK
    padded_end = jnp.cumsum(padded)
    padded_start = padded_end - padded
    blk0 = jnp.arange(n_blocks, dtype=jnp.int32) * SLOT_BLOCK
    block_expert = jnp.minimum(jnp.sum(blk0[:, None] >= padded_end[None, :], axis=1), N_EXPERTS - 1).astype(jnp.int32)
    n_used = (padded_end[-1] // SLOT_BLOCK).astype(jnp.int32)
    own = block_expert[:, None] == jnp.arange(N_EXPERTS, dtype=jnp.int32)[None, :]
    pick = lambda tab: jnp.sum(jnp.where(own, tab[None, :], 0), axis=1)
    valid = jnp.clip(pick(counts) - (blk0 - pick(padded_start)), 0, SLOT_BLOCK)
    block_valid = jnp.where(jnp.arange(n_blocks) < n_used, valid, 0).astype(jnp.int32)
    return padded_start.astype(jnp.int32), block_expert, block_valid, n_used.reshape(1)


def kernel(x, p, w_in, hg_lb_fwd, hg_lb_bwd, hg_norm_g, w_branch_hg, w_branch_att, w_out, ln1_g, ln1_b, w_router, router_bias, w_exp_gate, w_exp_up, w_exp_down, w_sh_gate, w_sh_up, w_sh_down, ln2_g, ln2_b, w_ple_gate, w_ple_proj, ln3_g, ln3_b):
    B, L, D = x.shape
    depth = w_in.shape[0]
    T = B * L
    alpha = (2 * depth) ** 0.25
    n_assign = T * TOP_K
    n_blocks = (n_assign + N_EXPERTS * (SLOT_BLOCK - 1) + SLOT_BLOCK - 1) // SLOT_BLOCK
    lb_fwd_all = jnp.cumsum(jax.nn.softmax(hg_lb_fwd.astype(F32), axis=0), axis=0)
    lb_bwd_all = jnp.cumsum(jax.nn.softmax(hg_lb_bwd.astype(F32), axis=0), axis=0)
    rc, rsa, rsb = _rotary_tables(L)
    row = lambda a: a.reshape(1, -1)

    x2d = x.reshape(T, D)
    for i in range(depth):
        (hq, hv, lff, lfb, hg, aq, ak, av, gh, ga) = _in_proj(
            x2d, w_in[i].astype(BF16), row(lb_fwd_all[i]), row(lb_bwd_all[i]), rc, rsa, rsb, L)
        y_h = _hgrn(hq, hv, lff, lfb, hg, hg_norm_g[i], B, L).reshape(T, HG_WIDTH)
        att = [_attn_group(aq, ak, av, g, B, L) for g in range(len(DIL_PAIRS))]
        x1, x1r, top_e, gate, rank, cnt = _mix(
            x2d, y_h, [a[0] for a in att], [a[1] for a in att], gh, ga,
            w_branch_hg[i].astype(BF16), w_branch_att[i].astype(BF16), w_out[i].astype(BF16),
            row(ln1_g[i]), row(ln1_b[i]), w_router[i].T.astype(BF16), router_bias[i].reshape(N_EXPERTS, 1), alpha)
        counts = cnt[:, 0].astype(jnp.int32)
        padded_start, block_expert, block_valid, n_used = _slot_layout(counts, n_blocks)
        dest = _dest(top_e, rank, padded_start)
        xs = _dispatch(dest, x1r, n_blocks * SLOT_BLOCK)
        ys = _experts(block_expert, block_valid, n_used, xs, w_exp_gate[i], w_exp_up[i], w_exp_down[i])
        x2d = _combine(dest, x1, gate.T, p[i].reshape(T, PLE_DIM), ys,
                       w_sh_gate[i].astype(BF16), w_sh_up[i].astype(BF16), w_sh_down[i].astype(BF16),
                       row(ln2_g[i]), row(ln2_b[i]), w_ple_gate[i].astype(BF16), w_ple_proj[i].astype(BF16),
                       row(ln3_g[i]), row(ln3_b[i]), alpha)
    return x2d.reshape(B, L, D)
```

```python
import functools

import jax
import jax.numpy as jnp
import numpy as np
from jax import lax
from jax.experimental import pallas as pl
from jax.experimental.pallas import tpu as pltpu

F32 = jnp.float32
BF16 = jnp.bfloat16

D_MODEL = 1024
HG_HEADS = 4
HG_DIM = 128
HG_WIDTH = HG_HEADS * HG_DIM
HG_CHUNK = 64
DIL_PAIRS = ((128, 1), (512, 4), (2048, 16))
ATT_SLOTS = 4
ATT_HEAD_DIM = 64
ATT_WIDTH = len(DIL_PAIRS) * ATT_SLOTS * ATT_HEAD_DIM
ATT_OUT = ATT_SLOTS * ATT_HEAD_DIM
ROT_DIM = ATT_HEAD_DIM // 4
ROPE_THETA = 500000.0
COL_SIZES = (HG_WIDTH,) * 5 + (ATT_WIDTH,) * 3 + (D_MODEL,) * 2
COL_STARTS = tuple(int(v) for v in np.cumsum((0,) + COL_SIZES)[:-1])
IN_COLS = sum(COL_SIZES)
N_EXPERTS = 256
TOP_K = 8
N_GROUPS = 8
TOPK_GROUPS = 4
PER_GROUP = N_EXPERTS // N_GROUPS
EXPERT_HIDDEN = 256
SHARED_HIDDEN = 256
ROUTED_SCALE = 2.5
SLOT_BLOCK = 1024
PLE_DIM = 256
LN_EPS = 1e-5
NEG_INF = -1e30

LANES = 128
SUBLANES = 8
VMEM_LIMIT = 56 * 1024 * 1024

NT_DIMS = (((1,), (1,)), ((), ()))
TN_DIMS = (((0,), (0,)), ((), ()))


def _sigmoid(v):
    return jax.nn.sigmoid(v)


def _silu(v):
    return v * jax.nn.sigmoid(v)


def _layer_norm(v, g, b):
    mu = jnp.mean(v, axis=-1, keepdims=True)
    vc = v - mu
    var = jnp.mean(vc * vc, axis=-1, keepdims=True)
    return vc * lax.rsqrt(var + LN_EPS) * g + b


def _const_spec(shape):
    return pl.BlockSpec(shape, lambda *_: (0,) * len(shape))


def _in_proj_kernel(x_ref, w_ref, lbf_ref, lbb_ref, rc_ref, rsa_ref, rsb_ref,
                    hq_ref, hv_ref, lff_ref, lfb_ref, hg_ref,
                    aq_ref, ak_ref, av_ref, gh_ref, ga_ref):
    xb = x_ref[...].astype(BF16)

    def proj(seg):
        c0, width = COL_STARTS[seg], COL_SIZES[seg]
        return jnp.dot(xb, w_ref[:, c0:c0 + width], preferred_element_type=F32)

    hq_ref[...] = _silu(proj(0)).astype(BF16)
    hv_ref[...] = proj(1).astype(BF16)
    lb = lbf_ref[...]
    lff_ref[...] = jnp.log(lb + (1.0 - lb) * _sigmoid(proj(2)))
    lb = lbb_ref[...]
    lfb_ref[...] = jnp.log(lb + (1.0 - lb) * _sigmoid(proj(3)))
    hg_ref[...] = _silu(proj(4)).astype(BF16)

    rc, rsa, rsb = rc_ref[...], rsa_ref[...], rsb_ref[...]

    def rotary(z, out_ref, scale):
        for s in range(ATT_WIDTH // LANES):
            t = z[:, s * LANES:(s + 1) * LANES]
            up = pltpu.roll(t, LANES - ROT_DIM // 2, 1)
            dn = pltpu.roll(t, ROT_DIM // 2, 1)
            r = t * rc + up * rsa + dn * rsb
            out_ref[:, s * LANES:(s + 1) * LANES] = (r * scale).astype(BF16)

    rotary(proj(5), aq_ref, ATT_HEAD_DIM ** -0.5)
    rotary(proj(6), ak_ref, 1.0)
    av_ref[...] = proj(7).astype(BF16)
    gh_ref[...] = _sigmoid(proj(8)).astype(BF16)
    ga_ref[...] = _sigmoid(proj(9)).astype(BF16)


def _in_proj(x2d, w_b, lbf, lbb, rc, rsa, rsb, seq_len):
    T = x2d.shape[0]
    tm = 512
    n_pos_blocks = seq_len // tm
    row = lambda w: pl.BlockSpec((tm, w), lambda i: (i, 0))
    tab = pl.BlockSpec((tm, LANES), lambda i: (i % n_pos_blocks, 0))
    out_w = (HG_WIDTH, HG_WIDTH, HG_WIDTH, HG_WIDTH, HG_WIDTH, ATT_WIDTH, ATT_WIDTH, ATT_WIDTH, D_MODEL, D_MODEL)
    out_dt = (BF16, BF16, F32, F32, BF16, BF16, BF16, BF16, BF16, BF16)
    return pl.pallas_call(
        _in_proj_kernel,
        grid=(T // tm,),
        in_specs=[row(D_MODEL),
                  pl.BlockSpec((D_MODEL, IN_COLS), lambda i: (0, 0), pipeline_mode=pl.Buffered(1)),
                  _const_spec((1, HG_WIDTH)), _const_spec((1, HG_WIDTH)), tab, tab, tab],
        out_specs=[row(w) for w in out_w],
        out_shape=[jax.ShapeDtypeStruct((T, w), dt) for w, dt in zip(out_w, out_dt)],
        compiler_params=pltpu.CompilerParams(dimension_semantics=("parallel",), vmem_limit_bytes=VMEM_LIMIT),
        name="in_proj",
    )(x2d, w_b, lbf, lbb, rc, rsa, rsb)


def _split3(a):
    hi = a.astype(BF16)
    r1 = a - hi.astype(F32)
    mid = r1.astype(BF16)
    lo = (r1 - mid.astype(F32)).astype(BF16)
    return jnp.concatenate([hi, mid, lo], axis=1)


HG_SUPER = 4 * HG_CHUNK


def _hgrn_kernel(q_ref, v_ref, lff_ref, lfb_ref, sg_ref, ng_ref, y_ref,
                 of_scr, ob_scr, qf_scr, qb_scr, df_scr, db_scr, mf_scr, mb_scr):
    L = q_ref.shape[1]
    C, SC = HG_CHUNK, HG_SUPER
    n = L // C
    r_i = lax.broadcasted_iota(jnp.int32, (SC, SC), 0)
    c_i = lax.broadcasted_iota(jnp.int32, (SC, SC), 1)
    same = (r_i // C) == (c_i // C)
    r_l, c_l = r_i % C, c_i % C
    mask_f = jnp.logical_and(same, c_l <= r_l)
    mask_b = jnp.logical_and(same, c_l >= r_l)
    tri_f = mask_f.astype(BF16)
    tri_b = mask_b.astype(BF16)

    def chunk_rows(b, row):
        return jnp.concatenate([jnp.broadcast_to(b[c * C + row:c * C + row + 1, :], (C, HG_DIM))
                                for c in range(SC // C)], axis=0)

    dirs = ((lff_ref, tri_f, mask_f, C // 2 - 1, C - 1, of_scr, qf_scr, df_scr, mf_scr),
            (lfb_ref, tri_b, mask_b, C // 2, 0, ob_scr, qb_scr, db_scr, mb_scr))

    def local_body(i, carry):
        i0 = pl.multiple_of(i * SC, SC)
        rows = pl.ds(i0, SC)
        q = q_ref[0, rows, :].astype(F32)
        v = v_ref[0, rows, :]
        lfs = [d[0][0, rows, :] for d in dirs]
        sums = [jnp.dot(d[1], _split3(lf), preferred_element_type=F32) for d, lf in zip(dirs, lfs)]
        a_qs, a_ks, k_outs = [], [], []
        for d, lf, p in zip(dirs, lfs, sums):
            b = p[:, :HG_DIM] + p[:, HG_DIM:2 * HG_DIM] + p[:, 2 * HG_DIM:]
            b_ref, b_last = chunk_rows(b, d[3]), chunk_rows(b, d[4])
            k = 1.0 - jnp.exp(lf)
            a_qs.append((q * jnp.exp(b - b_ref)).astype(BF16))
            a_ks.append((k * jnp.exp(b_ref - b)).astype(BF16))
            k_outs.append((k * jnp.exp(b_last - b)).astype(BF16))
            d[6][rows, :] = (q * jnp.exp(b)).astype(BF16)
            d[7][rows, :] = b_last
        scores = [lax.dot_general(a_q, a_k, NT_DIMS, preferred_element_type=F32) for a_q, a_k in zip(a_qs, a_ks)]
        states = [[lax.dot_general(v[c * C:(c + 1) * C], k_out[c * C:(c + 1) * C], TN_DIMS,
                                   preferred_element_type=F32) for c in range(SC // C)] for k_out in k_outs]
        masked = [jnp.where(d[2], s, 0.0).astype(BF16) for d, s in zip(dirs, scores)]
        intra = [jnp.dot(m, v, preferred_element_type=F32) for m in masked]
        for d, o, st in zip(dirs, intra, states):
            d[5][rows, :] = o
            for c in range(SC // C):
                d[8][i0 // C + c] = st[c]
        return carry

    lax.fori_loop(0, L // SC, local_body, 0, unroll=2)

    def carry_part(c, st, o_scr, qin_scr, dec_scr, m_scr):
        rows = pl.ds(pl.multiple_of(c * C, C), C)
        o_scr[rows, :] = o_scr[rows, :] + lax.dot_general(qin_scr[rows, :], st.astype(BF16), NT_DIMS,
                                                          preferred_element_type=F32)
        return st * jnp.exp(dec_scr[pl.ds(pl.multiple_of(c * C, C), 1), :]) + m_scr[c]

    def carry_body(c, carry):
        st_f, st_b = carry
        st_f = carry_part(c, st_f, of_scr, qf_scr, df_scr, mf_scr)
        st_b = carry_part(n - 1 - c, st_b, ob_scr, qb_scr, db_scr, mb_scr)
        return st_f, st_b

    z = jnp.zeros((HG_DIM, HG_DIM), F32)
    lax.fori_loop(0, n, carry_body, (z, z), unroll=8)

    o = of_scr[...] + ob_scr[...]
    o = o * lax.rsqrt(jnp.mean(o * o, axis=-1, keepdims=True) + LN_EPS) * ng_ref[...]
    y_ref[0] = (o * sg_ref[0].astype(F32)).astype(BF16)


def _hgrn(hq, hv, lff, lfb, hg, norm_g, batch, seq_len):
    blk = pl.BlockSpec((1, seq_len, HG_DIM), lambda b, h: (b, 0, h))
    r3 = lambda a: a.reshape(batch, seq_len, HG_WIDTH)
    per_dir = [pltpu.VMEM((seq_len, HG_DIM), F32), pltpu.VMEM((seq_len, HG_DIM), BF16),
               pltpu.VMEM((seq_len, HG_DIM), F32), pltpu.VMEM((seq_len // HG_CHUNK, HG_DIM, HG_DIM), F32)]
    scratch = [s for pair in zip(per_dir, per_dir) for s in pair]
    return pl.pallas_call(
        _hgrn_kernel,
        grid=(batch, HG_HEADS),
        in_specs=[blk, blk, blk, blk, blk, pl.BlockSpec((1, HG_DIM), lambda b, h: (0, h))],
        out_specs=blk,
        out_shape=jax.ShapeDtypeStruct((batch, seq_len, HG_WIDTH), BF16),
        scratch_shapes=scratch,
        compiler_params=pltpu.CompilerParams(dimension_semantics=("parallel", "parallel"),
                                             vmem_limit_bytes=VMEM_LIMIT),
        name="hgrn",
    )(r3(hq), r3(hv), r3(lff), r3(lfb), r3(hg), norm_g.reshape(1, HG_WIDTH))


def _attn_kernel(q_ref, k_ref, v_ref, o_ref, lse_ref, *scratch, m_len, dil, qb, kw, span):
    lane = lax.broadcasted_iota(jnp.int32, (1, LANES), 1)
    head0 = lane < ATT_HEAD_DIM
    qi_l = lax.broadcasted_iota(jnp.int32, (qb, kw), 0)
    ki_l = lax.broadcasted_iota(jnp.int32, (qb, kw), 1)

    def run_class(load_q, load_k, load_v, store_o, store_l):
        def block(i, carry):
            q0 = pl.multiple_of(i * qb, qb)
            ks = pl.multiple_of(jnp.clip(q0 - span, 0, m_len - kw), span)
            mask = jnp.abs((q0 + qi_l) - (ks + ki_l)) <= span
            slabs = [slice(sl * LANES, (sl + 1) * LANES) for sl in range(ATT_OUT // LANES)]
            heads = (head0, jnp.logical_not(head0))
            qkv = [(load_q(q0, qb, cols), load_k(ks, kw, cols), load_v(ks, kw, cols)) for cols in slabs]
            scores = [lax.dot_general(jnp.where(hm, q2, jnp.zeros_like(q2)), k2, NT_DIMS, preferred_element_type=F32)
                      for (q2, k2, _) in qkv for hm in heads]
            probs, dens, lses = [], [], []
            for s in scores:
                s = jnp.where(mask, s, NEG_INF)
                mx = jnp.max(s, axis=-1, keepdims=True)
                e = jnp.exp(s - mx)
                den = jnp.sum(e, axis=-1, keepdims=True)
                probs.append(e.astype(BF16))
                dens.append(den)
                lses.append(mx + jnp.log(den))
            outs = [jnp.dot(pr, qkv[c // 2][2], preferred_element_type=F32) / dens[c] for c, pr in enumerate(probs)]
            for sl, cols in enumerate(slabs):
                store_o(q0, qb, cols, jnp.where(head0, outs[2 * sl], outs[2 * sl + 1]))
                store_l(q0, qb, cols, jnp.where(head0, lses[2 * sl], lses[2 * sl + 1]))
            return carry

        lax.fori_loop(0, m_len // qb, block, 0, unroll=2 if m_len // qb > 1 else 1)

    def loader(ref):
        return lambda r0, n, cols: ref[0, pl.ds(r0, n), cols]

    if dil == 1:
        def store_o(r0, n, cols, val):
            o_ref[0, pl.ds(r0, n), cols] = val.astype(BF16)

        def store_l(r0, n, cols, val):
            lse_ref[0, pl.ds(r0, n), cols] = val

        run_class(loader(q_ref), loader(k_ref), loader(v_ref), store_o, store_l)
        return

    q32, k32, v32, o32, l32, qc, kc, vc, oc, lc = scratch
    n_slab = ATT_OUT // LANES
    for slab in range(n_slab):
        cols = slice(slab * LANES, (slab + 1) * LANES)
        q32[slab] = q_ref[0, :, cols].astype(F32)
        k32[slab] = k_ref[0, :, cols].astype(F32)
        v32[slab] = v_ref[0, :, cols].astype(F32)

    def cls_loader(ref):
        return lambda r0, n, cols: ref[pl.ds(r0, n), cols]

    def store_oc(r0, n, cols, val):
        oc[pl.ds(r0, n), cols] = val

    def store_lc(r0, n, cols, val):
        lc[pl.ds(r0, n), cols] = val

    for r in range(dil):
        rows = pl.ds(r, m_len, stride=dil)
        for slab in range(n_slab):
            cols = slice(slab * LANES, (slab + 1) * LANES)
            qc[:, cols] = q32[slab, rows, :].astype(BF16)
            kc[:, cols] = k32[slab, rows, :].astype(BF16)
            vc[:, cols] = v32[slab, rows, :].astype(BF16)
        run_class(cls_loader(qc), cls_loader(kc), cls_loader(vc), store_oc, store_lc)
        for slab in range(n_slab):
            cols = slice(slab * LANES, (slab + 1) * LANES)
            o32[slab, rows, :] = oc[:, cols]
            l32[slab, rows, :] = lc[:, cols]
    for slab in range(n_slab):
        cols = slice(slab * LANES, (slab + 1) * LANES)
        o_ref[0, :, cols] = o32[slab].astype(BF16)
        lse_ref[0, :, cols] = l32[slab]


def _attn_group(aq, ak, av, g, batch, seq_len):
    window, dil = DIL_PAIRS[g]
    span = window // (2 * dil)
    m_len = seq_len // dil
    qb = min(2 * span, m_len)
    kw = min(qb + 2 * span, m_len)
    view = lambda a: a.reshape(batch, seq_len, ATT_WIDTH)
    in_blk = pl.BlockSpec((1, seq_len, ATT_OUT), lambda b: (b, 0, g))
    out_blk = pl.BlockSpec((1, seq_len, ATT_OUT), lambda b: (b, 0, 0))
    scratch = []
    if dil > 1:
        scratch = ([pltpu.VMEM((ATT_OUT // LANES, seq_len, LANES), F32)] * 5 + [pltpu.VMEM((m_len, ATT_OUT), BF16)] * 3
                   + [pltpu.VMEM((m_len, ATT_OUT), F32)] * 2)
    o, lse = pl.pallas_call(
        functools.partial(_attn_kernel, m_len=m_len, dil=dil, qb=qb, kw=kw, span=span),
        grid=(batch,),
        in_specs=[in_blk, in_blk, in_blk],
        out_specs=[out_blk, out_blk],
        out_shape=[jax.ShapeDtypeStruct((batch, seq_len, ATT_OUT), BF16),
                   jax.ShapeDtypeStruct((batch, seq_len, ATT_OUT), F32)],
        scratch_shapes=scratch,
        compiler_params=pltpu.CompilerParams(dimension_semantics=("parallel",), vmem_limit_bytes=VMEM_LIMIT),
        name=f"attn_g{g}",
    )(view(aq), view(ak), view(av))
    T = batch * seq_len
    return o.reshape(T, ATT_OUT), lse.reshape(T, ATT_OUT)


def _mix_kernel(x_ref, yh_ref, o0_ref, o1_ref, o2_ref, l0_ref, l1_ref, l2_ref, gh_ref, ga_ref,
                wbh_ref, wba_ref, wo_ref, g1_ref, b1_ref, wrt_ref, rb_ref,
                x1_ref, x1r_ref, tope_ref, gate_ref, rank_ref, cnt_ref, carry_scr, *, alpha):
    tm = x_ref.shape[0]

    @pl.when(pl.program_id(0) == 0)
    def _():
        carry_scr[...] = jnp.zeros_like(carry_scr)

    l0, l1, l2 = l0_ref[...], l1_ref[...], l2_ref[...]
    lm = jnp.maximum(jnp.maximum(l0, l1), l2)
    e0, e1, e2 = jnp.exp(l0 - lm), jnp.exp(l1 - lm), jnp.exp(l2 - lm)
    es = e0 + e1 + e2
    y_a = ((e0 / es) * o0_ref[...].astype(F32) + (e1 / es) * o1_ref[...].astype(F32)
           + (e2 / es) * o2_ref[...].astype(F32)).astype(BF16)
    merged = (gh_ref[...].astype(F32) * jnp.dot(yh_ref[...], wbh_ref[...], preferred_element_type=F32)
              + ga_ref[...].astype(F32) * jnp.dot(y_a, wba_ref[...], preferred_element_type=F32))
    mixed = jnp.dot(merged.astype(BF16), wo_ref[...], preferred_element_type=F32)
    x1 = _layer_norm(alpha * x_ref[...] + mixed, g1_ref[...], b1_ref[...])
    x1_ref[...] = x1
    x1r_ref[:, 0, :] = x1

    logit = lax.dot_general(wrt_ref[...], x1.astype(BF16), NT_DIMS, preferred_element_type=F32)
    score = _sigmoid(logit)
    biased = score + rb_ref[...]
    b3 = biased.reshape(N_GROUPS, PER_GROUP, tm)
    i3 = lax.broadcasted_iota(jnp.int32, (N_GROUPS, PER_GROUP, tm), 1)
    m1 = jnp.max(b3, axis=1, keepdims=True)
    idx1 = jnp.min(jnp.where(b3 == m1, i3, PER_GROUP), axis=1, keepdims=True)
    m2 = jnp.max(jnp.where(i3 == idx1, -jnp.inf, b3), axis=1, keepdims=True)
    gs = (m1 + m2).reshape(N_GROUPS, tm)
    gi = lax.broadcasted_iota(jnp.int32, (N_GROUPS, tm), 0)
    sel = jnp.zeros((N_GROUPS, tm), F32)
    cur = gs
    for _ in range(TOPK_GROUPS):
        m = jnp.max(cur, axis=0, keepdims=True)
        ix = jnp.min(jnp.where(cur == m, gi, N_GROUPS), axis=0, keepdims=True)
        hit = gi == ix
        sel = jnp.where(hit, 1.0, sel)
        cur = jnp.where(hit, -jnp.inf, cur)
    cur = jnp.where(sel.reshape(N_GROUPS, 1, tm) > 0.0, b3, -jnp.inf).reshape(N_EXPERTS, tm)
    ei = lax.broadcasted_iota(jnp.int32, (N_EXPERTS, tm), 0)
    chosen = jnp.zeros((N_EXPERTS, tm), F32)
    top_e, top_s = [], []
    for _ in range(TOP_K):
        m = jnp.max(cur, axis=0, keepdims=True)
        ix = jnp.min(jnp.where(cur == m, ei, N_EXPERTS), axis=0, keepdims=True)
        hit = ei == ix
        top_e.append(ix)
        top_s.append(jnp.sum(jnp.where(hit, score, 0.0), axis=0, keepdims=True))
        chosen = jnp.where(hit, 1.0, chosen)
        cur = jnp.where(hit, -jnp.inf, cur)
    s_sum = top_s[0]
    for s in top_s[1:]:
        s_sum = s_sum + s
    tope_ref[...] = jnp.concatenate(top_e, axis=0)
    gate_ref[...] = jnp.concatenate([s / s_sum * ROUTED_SCALE for s in top_s], axis=0)

    su = lax.broadcasted_iota(jnp.int32, (tm, tm), 0)
    tu = lax.broadcasted_iota(jnp.int32, (tm, tm), 1)
    before = (su < tu).astype(BF16)
    prior = carry_scr[:, 0:1] + jnp.dot(chosen.astype(BF16), before, preferred_element_type=F32)
    ranks = [jnp.sum(jnp.where(ei == ix, prior, 0.0), axis=0, keepdims=True) for ix in top_e]
    rank_ref[...] = jnp.concatenate(ranks, axis=0).astype(jnp.int32)
    carry_scr[...] = carry_scr[...] + jnp.sum(chosen, axis=1, keepdims=True)
    cnt_ref[...] = carry_scr[...]


def _mix(x2d, y_h, att_o, att_l, gh, ga, wbh, wba, wo, g1, b1, wrt, rb, alpha):
    T = x2d.shape[0]
    tm = 256
    row = lambda w: pl.BlockSpec((tm, w), lambda i: (i, 0))
    col = pl.BlockSpec((TOP_K, tm), lambda i: (0, i))
    return pl.pallas_call(
        functools.partial(_mix_kernel, alpha=alpha),
        grid=(T // tm,),
        in_specs=[row(D_MODEL), row(HG_WIDTH)] + [row(ATT_OUT)] * 6 + [row(D_MODEL), row(D_MODEL),
                  _const_spec((HG_WIDTH, D_MODEL)), _const_spec((ATT_OUT, D_MODEL)),
                  _const_spec((D_MODEL, D_MODEL)), _const_spec((1, D_MODEL)), _const_spec((1, D_MODEL)),
                  _const_spec((N_EXPERTS, D_MODEL)), _const_spec((N_EXPERTS, 1))],
        out_specs=[row(D_MODEL), pl.BlockSpec((tm, 1, D_MODEL), lambda i: (i, 0, 0)), col, col, col,
                   _const_spec((N_EXPERTS, LANES))],
        out_shape=[jax.ShapeDtypeStruct((T, D_MODEL), F32),
                   jax.ShapeDtypeStruct((T, 1, D_MODEL), F32),
                   jax.ShapeDtypeStruct((TOP_K, T), jnp.int32),
                   jax.ShapeDtypeStruct((TOP_K, T), F32),
                   jax.ShapeDtypeStruct((TOP_K, T), jnp.int32),
                   jax.ShapeDtypeStruct((N_EXPERTS, LANES), F32)],
        scratch_shapes=[pltpu.VMEM((N_EXPERTS, LANES), F32)],
        compiler_params=pltpu.CompilerParams(dimension_semantics=("arbitrary",), vmem_limit_bytes=VMEM_LIMIT),
        name="mix",
    )(x2d, y_h, *att_o, *att_l, gh, ga, wbh, wba, wo, g1, b1, wrt, rb)


def _dest_kernel(tope_ref, rank_ref, ps_ref, dest_ref):
    tb = tope_ref.shape[1]
    ei = lax.broadcasted_iota(jnp.int32, (N_EXPERTS, tb), 0)
    ps = ps_ref[...]
    starts = [jnp.sum(jnp.where(ei == tope_ref[k:k + 1, :], ps, 0.0), axis=0, keepdims=True)
              for k in range(TOP_K)]
    dest_ref[...] = jnp.concatenate(starts, axis=0).astype(jnp.int32) + rank_ref[...]


def _dest(top_e, rank, padded_start):
    T = top_e.shape[1]
    tb = 512
    col = pl.BlockSpec((TOP_K, tb), lambda i: (0, i))
    return pl.pallas_call(
        _dest_kernel,
        grid=(T // tb,),
        in_specs=[col, col, _const_spec((N_EXPERTS, 1))],
        out_specs=col,
        out_shape=jax.ShapeDtypeStruct((TOP_K, T), jnp.int32),
        compiler_params=pltpu.CompilerParams(dimension_semantics=("parallel",)),
        name="dest",
    )(top_e, rank, padded_start.astype(F32).reshape(N_EXPERTS, 1))


def _dispatch_kernel(dest_ref, x_ref, xs_hbm, sem):
    tb = x_ref.shape[0]

    def issue(g, c):
        t0 = pl.multiple_of(g * SUBLANES, SUBLANES)
        for j in range(SUBLANES):
            for k in range(TOP_K):
                pltpu.make_async_copy(x_ref.at[t0 + j], xs_hbm.at[dest_ref[k, t0 + j]], sem).start(priority=k % 2)
        return c

    lax.fori_loop(0, tb // SUBLANES, issue, 0)
    for _ in range(TOP_K):
        pltpu.make_async_copy(x_ref, xs_hbm.at[pl.ds(0, tb)], sem).wait()


def _dispatch(dest, x1, n_slots):
    T = x1.shape[0]
    tb = 2048
    return pl.pallas_call(
        _dispatch_kernel,
        grid=(T // tb,),
        in_specs=[pl.BlockSpec((TOP_K, tb), lambda i: (0, i), memory_space=pltpu.SMEM),
                  pl.BlockSpec((tb, 1, D_MODEL), lambda i: (i, 0, 0))],
        out_specs=pl.BlockSpec(memory_space=pl.ANY),
        out_shape=jax.ShapeDtypeStruct((n_slots, 1, D_MODEL), F32),
        scratch_shapes=[pltpu.SemaphoreType.DMA],
        compiler_params=pltpu.CompilerParams(dimension_semantics=("arbitrary",)),
        name="dispatch",
    )(dest, x1)


EXPERT_SUBBLOCKS = 2
N_IN_BUF = 3


def _expert_kernel(be_ref, bv_ref, nu_ref, xs_hbm, wg_ref, wu_ref, wd_ref, ys_hbm,
                   xbuf, ybuf, wg_b, wu_b, wd_b, in_sem, out_sem):
    i = pl.program_id(0)
    n_used = nu_ref[0]
    slot = i % 2
    in_slot = i % N_IN_BUF

    def live_rows(blk):
        return pl.multiple_of((bv_ref[blk] + SUBLANES - 1) // SUBLANES * SUBLANES, SUBLANES)

    def in_copy(blk, s):
        n = live_rows(blk)
        return pltpu.make_async_copy(xs_hbm.at[pl.ds(blk * SLOT_BLOCK, n), 0, :], xbuf.at[s, pl.ds(0, n), :], in_sem.at[s])

    def out_copy(blk, s):
        n = live_rows(blk)
        return pltpu.make_async_copy(ybuf.at[s, pl.ds(0, n), :], ys_hbm.at[pl.ds(blk * SLOT_BLOCK, n), 0, :], out_sem.at[s])

    @pl.when(i == 0)
    def _():
        for b in range(N_IN_BUF - 1):
            @pl.when(b < n_used)
            def _():
                in_copy(b, b).start()

    @pl.when(i < n_used)
    def _():
        ahead = i + N_IN_BUF - 1

        @pl.when(ahead < n_used)
        def _():
            in_copy(ahead, ahead % N_IN_BUF).start()

        in_copy(i, in_slot).wait()

        prev = be_ref[jnp.maximum(i - 1, 0)]

        @pl.when(jnp.logical_or(i == 0, be_ref[i] != prev))
        def _():
            wg_b[...] = wg_ref[0].astype(BF16)
            wu_b[...] = wu_ref[0].astype(BF16)
            wd_b[...] = wd_ref[0].astype(BF16)

        @pl.when(i >= 2)
        def _():
            out_copy(i - 2, slot).wait()

        sub = SLOT_BLOCK // EXPERT_SUBBLOCKS
        rows = lax.broadcasted_iota(jnp.int32, (sub, 1), 0)
        xbs = [jnp.where(rows + r * sub < bv_ref[i], xbuf[in_slot, pl.ds(r * sub, sub), :], 0.0).astype(BF16)
               for r in range(EXPERT_SUBBLOCKS)]
        hs = [(_silu(jnp.dot(xb, wg_b[...], preferred_element_type=F32))
               * jnp.dot(xb, wu_b[...], preferred_element_type=F32)).astype(BF16) for xb in xbs]
        for r, h in enumerate(hs):
            ybuf[slot, pl.ds(r * sub, sub), :] = jnp.dot(h, wd_b[...], preferred_element_type=F32)
        out_copy(i, slot).start(priority=1)

        @pl.when(i == n_used - 1)
        def _():
            @pl.when(i >= 1)
            def _():
                out_copy(i - 1, 1 - slot).wait()

            out_copy(i, slot).wait()


def _experts(block_expert, block_valid, n_used, xs, w_gate, w_up, w_down):
    n_blocks = xs.shape[0] // SLOT_BLOCK
    return pl.pallas_call(
        _expert_kernel,
        grid_spec=pltpu.PrefetchScalarGridSpec(
            num_scalar_prefetch=3,
            grid=(n_blocks,),
            in_specs=[pl.BlockSpec(memory_space=pl.ANY),
                      pl.BlockSpec((1, D_MODEL, EXPERT_HIDDEN), lambda i, be, bv, nu: (be[i], 0, 0)),
                      pl.BlockSpec((1, D_MODEL, EXPERT_HIDDEN), lambda i, be, bv, nu: (be[i], 0, 0)),
                      pl.BlockSpec((1, EXPERT_HIDDEN, D_MODEL), lambda i, be, bv, nu: (be[i], 0, 0))],
            out_specs=pl.BlockSpec(memory_space=pl.ANY),
            scratch_shapes=[pltpu.VMEM((N_IN_BUF, SLOT_BLOCK, D_MODEL), F32),
                            pltpu.VMEM((2, SLOT_BLOCK, D_MODEL), F32),
                            pltpu.VMEM((D_MODEL, EXPERT_HIDDEN), BF16),
                            pltpu.VMEM((D_MODEL, EXPERT_HIDDEN), BF16),
                            pltpu.VMEM((EXPERT_HIDDEN, D_MODEL), BF16),
                            pltpu.SemaphoreType.DMA((N_IN_BUF,)), pltpu.SemaphoreType.DMA((2,))],
        ),
        out_shape=jax.ShapeDtypeStruct(xs.shape, F32),
        compiler_params=pltpu.CompilerParams(dimension_semantics=("arbitrary",), vmem_limit_bytes=VMEM_LIMIT),
        name="experts",
    )(block_expert, block_valid, n_used, xs, w_gate, w_up, w_down)


def _combine_kernel(dest_ref, dnext_ref, x1_ref, gate_ref, p_ref, ys_hbm,
                    wsg_ref, wsu_ref, wsd_ref, g2_ref, b2_ref, wpg_ref, wpp_ref, g3_ref, b3_ref,
                    out_ref, *scratch, alpha):
    sem = scratch[-1]
    bufs = (scratch[:TOP_K], scratch[TOP_K:2 * TOP_K])
    tm = bufs[0][0].shape[0]
    i = pl.program_id(0)

    def issue(d_ref, t, k, col0, half):
        pltpu.make_async_copy(ys_hbm.at[d_ref[k, col0 + t]], bufs[half][k].at[t], sem.at[half]).start(priority=k % 2)

    def wait_rows(half):
        for k in range(TOP_K):
            pltpu.make_async_copy(ys_hbm.at[pl.ds(0, tm)], bufs[half][k], sem.at[half]).wait()

    def compute(half):
        rows = pl.ds(half * tm, tm)
        x1 = x1_ref[rows, :]
        xb = x1.astype(BF16)
        hs = (_silu(jnp.dot(xb, wsg_ref[...], preferred_element_type=F32))
              * jnp.dot(xb, wsu_ref[...], preferred_element_type=F32))
        ffn = jnp.dot(hs.astype(BF16), wsd_ref[...], preferred_element_type=F32)
        gate = gate_ref[rows, :]
        for k in range(TOP_K):
            ffn = ffn + gate[:, k:k + 1] * bufs[half][k][:, 0, :]
        x2 = _layer_norm(alpha * x1 + ffn, g2_ref[...], b2_ref[...])
        ple = (_sigmoid(jnp.dot(x2.astype(BF16), wpg_ref[...], preferred_element_type=F32))
               * jnp.dot(p_ref[rows, :].astype(BF16), wpp_ref[...], preferred_element_type=F32))
        out_ref[rows, :] = _layer_norm(alpha * x2 + ple, g3_ref[...], b3_ref[...])

    @pl.when(i == 0)
    def _():
        def first(g, c):
            t0 = pl.multiple_of(g * SUBLANES, SUBLANES)
            for j in range(SUBLANES):
                for k in range(TOP_K):
                    issue(dest_ref, t0 + j, k, 0, 0)
            return c

        lax.fori_loop(0, tm // SUBLANES, first, 0)

    wait_rows(0)
    for t in range(tm):
        for k in range(TOP_K):
            issue(dest_ref, t, k, tm, 1)
    compute(0)
    wait_rows(1)
    for t in range(tm):
        for k in range(TOP_K):
            issue(dnext_ref, t, k, 0, 0)
    compute(1)

    @pl.when(i == pl.num_programs(0) - 1)
    def _():
        wait_rows(0)


def _combine(dest, x1, gate_t, p2d, ys, wsg, wsu, wsd, g2, b2, wpg, wpp, g3, b3, alpha):
    T = x1.shape[0]
    tm = 256
    n = T // (2 * tm)
    row = lambda w: pl.BlockSpec((2 * tm, w), lambda i: (i, 0))
    return pl.pallas_call(
        functools.partial(_combine_kernel, alpha=alpha),
        grid=(n,),
        in_specs=[pl.BlockSpec((TOP_K, 2 * tm), lambda i: (0, i), memory_space=pltpu.SMEM),
                  pl.BlockSpec((TOP_K, tm), lambda i: (0, jnp.minimum(2 * i + 2, 2 * n - 1)), memory_space=pltpu.SMEM),
                  row(D_MODEL), row(TOP_K), row(PLE_DIM),
                  pl.BlockSpec(memory_space=pl.ANY),
                  _const_spec((D_MODEL, SHARED_HIDDEN)), _const_spec((D_MODEL, SHARED_HIDDEN)),
                  _const_spec((SHARED_HIDDEN, D_MODEL)), _const_spec((1, D_MODEL)), _const_spec((1, D_MODEL)),
                  _const_spec((D_MODEL, D_MODEL)), _const_spec((PLE_DIM, D_MODEL)),
                  _const_spec((1, D_MODEL)), _const_spec((1, D_MODEL))],
        out_specs=row(D_MODEL),
        out_shape=jax.ShapeDtypeStruct((T, D_MODEL), F32),
        scratch_shapes=[pltpu.VMEM((tm, 1, D_MODEL), F32)] * (2 * TOP_K) + [pltpu.SemaphoreType.DMA((2,))],
        compiler_params=pltpu.CompilerParams(dimension_semantics=("arbitrary",), vmem_limit_bytes=VMEM_LIMIT),
        name="combine",
    )(dest, dest, x1, gate_t, p2d, ys, wsg, wsu, wsd, g2, b2, wpg, wpp, g3, b3)


def _rotary_tables(seq_len):
    half = ROT_DIM // 2
    inv_freq = ROPE_THETA ** (-jnp.arange(half, dtype=F32) / half)
    ang = jnp.arange(seq_len, dtype=F32)[:, None] * inv_freq[None, :]
    cos, sin = jnp.cos(ang), jnp.sin(ang)
    pad = ATT_HEAD_DIM - ROT_DIM
    one = jnp.ones((seq_len, pad), F32)
    zero = jnp.zeros((seq_len, pad + half), F32)
    rc = jnp.concatenate([cos, cos, one], axis=1)
    rsa = jnp.concatenate([-sin, zero], axis=1)
    rsb = jnp.concatenate([jnp.zeros((seq_len, half), F32), sin, jnp.zeros((seq_len, pad), F32)], axis=1)
    rep = lambda a: jnp.tile(a, (1, LANES // ATT_HEAD_DIM))
    return rep(rc), rep(rsa), rep(rsb)


def _slot_layout(counts, n_blocks):
    padded = (counts + SLOT_BLOCK - 1) // SLOT_BLOCK * SLOT_BLOCK
    padded_end = jnp.cumsum(padded)
    padded_start = padded_end - padded
    blk0 = jnp.arange(n_blocks, dtype=jnp.int32) * SLOT_BLOCK
    block_expert = jnp.minimum(jnp.sum(blk0[:, None] >= padded_end[None, :], axis=1), N_EXPERTS - 1).astype(jnp.int32)
    n_used = (padded_end[-1] // SLOT_BLOCK).astype(jnp.int32)
    own = block_expert[:, None] == jnp.arange(N_EXPERTS, dtype=jnp.int32)[None, :]
    pick = lambda tab: jnp.sum(jnp.where(own, tab[None, :], 0), axis=1)
    valid = jnp.clip(pick(counts) - (blk0 - pick(padded_start)), 0, SLOT_BLOCK)
    block_valid = jnp.where(jnp.arange(n_blocks) < n_used, valid, 0).astype(jnp.int32)
    return padded_start.astype(jnp.int32), block_expert, block_valid, n_used.reshape(1)


def kernel(x, p, w_in, hg_lb_fwd, hg_lb_bwd, hg_norm_g, w_branch_hg, w_branch_att, w_out, ln1_g, ln1_b, w_router, router_bias, w_exp_gate, w_exp_up, w_exp_down, w_sh_gate, w_sh_up, w_sh_down, ln2_g, ln2_b, w_ple_gate, w_ple_proj, ln3_g, ln3_b):
    B, L, D = x.shape
    depth = w_in.shape[0]
    T = B * L
    alpha = (2 * depth) ** 0.25
    n_assign = T * TOP_K
    n_blocks = (n_assign + N_EXPERTS * (SLOT_BLOCK - 1) + SLOT_BLOCK - 1) // SLOT_BLOCK
    lb_fwd_all = jnp.cumsum(jax.nn.softmax(hg_lb_fwd.astype(F32), axis=0), axis=0)
    lb_bwd_all = jnp.cumsum(jax.nn.softmax(hg_lb_bwd.astype(F32), axis=0), axis=0)
    rc, rsa, rsb = _rotary_tables(L)
    row = lambda a: a.reshape(1, -1)

    x2d = x.reshape(T, D)
    for i in range(depth):
        (hq, hv, lff, lfb, hg, aq, ak, av, gh, ga) = _in_proj(
            x2d, w_in[i].astype(BF16), row(lb_fwd_all[i]), row(lb_bwd_all[i]), rc, rsa, rsb, L)
        y_h = _hgrn(hq, hv, lff, lfb, hg, hg_norm_g[i], B, L).reshape(T, HG_WIDTH)
        att = [_attn_group(aq, ak, av, g, B, L) for g in range(len(DIL_PAIRS))]
        x1, x1r, top_e, gate, rank, cnt = _mix(
            x2d, y_h, [a[0] for a in att], [a[1] for a in att], gh, ga,
            w_branch_hg[i].astype(BF16), w_branch_att[i].astype(BF16), w_out[i].astype(BF16),
            row(ln1_g[i]), row(ln1_b[i]), w_router[i].T.astype(BF16), router_bias[i].reshape(N_EXPERTS, 1), alpha)
        counts = cnt[:, 0].astype(jnp.int32)
        padded_start, block_expert, block_valid, n_used = _slot_layout(counts, n_blocks)
        dest = _dest(top_e, rank, padded_start)
        xs = _dispatch(dest, x1r, n_blocks * SLOT_BLOCK)
        ys = _experts(block_expert, block_valid, n_used, xs, w_exp_gate[i], w_exp_up[i], w_exp_down[i])
        x2d = _combine(dest, x1, gate.T, p[i].reshape(T, PLE_DIM), ys,
                       w_sh_gate[i].astype(BF16), w_sh_up[i].astype(BF16), w_sh_down[i].astype(BF16),
                       row(ln2_g[i]), row(ln2_b[i]), w_ple_gate[i].astype(BF16), w_ple_proj[i].astype(BF16),
                       row(ln3_g[i]), row(ln3_b[i]), alpha)
    return x2d.reshape(B, L, D)
```
